```python
import jax, jax.numpy as jnp
from jax import lax
import numpy as np

D_MODEL = 1024
BATCH = 16
SEQ = 2048
DEPTH = 2

GRID_W = 64
CTX_LEN = 256
EPS = 1e-6
D_BRANCH = D_MODEL // 2
N_BRANCH = 3
HEAD_DIM = 64
ATT_HEADS = D_BRANCH // HEAD_DIM
ATT_KV_HEADS = ATT_HEADS // 4
WINDOW = 128
ATT_BLOCK = 128
ROPE_BASE = 10000.0
HG_DK = 128
HG_HEADS = D_BRANCH // HG_DK
HG_DV = D_BRANCH // HG_HEADS
HG_CHUNK = 64
MLP_CHUNK = 128
MLP_GROUP_DIM = 128
MLP_GROUPS = D_BRANCH // MLP_GROUP_DIM
D_FF = -(-8 * D_MODEL // (3 * 256)) * 256
IN_SPLITS = (ATT_HEADS * HEAD_DIM, ATT_KV_HEADS * HEAD_DIM, ATT_KV_HEADS * HEAD_DIM,
             D_BRANCH, D_BRANCH, D_BRANCH, D_BRANCH, D_BRANCH,
             D_BRANCH, D_BRANCH, N_BRANCH * D_MODEL)
D_IN = sum(IN_SPLITS)

kernel_name = "hybrid_gated_parallel_dit_block"


def rmsnorm(x, g):
    xf = x.astype(jnp.float32)
    y = xf * lax.rsqrt(jnp.mean(xf * xf, axis=-1, keepdims=True) + EPS)
    return (y * g.astype(jnp.float32)).astype(x.dtype)


def layernorm(x, g, b):
    xf = x.astype(jnp.float32)
    mu = jnp.mean(xf, axis=-1, keepdims=True)
    var = jnp.mean(jnp.square(xf - mu), axis=-1, keepdims=True)
    y = (xf - mu) * lax.rsqrt(var + EPS)
    return (y * g.astype(jnp.float32) + b.astype(jnp.float32)).astype(x.dtype)


def split_proj(p):
    out, off = [], 0
    for w in IN_SPLITS:
        out.append(p[..., off:off + w])
        off += w
    return out


def heads(t, n):
    return t.reshape(t.shape[0], t.shape[1], n, -1)


def axial_rope(x):
    L = x.shape[1]
    rows_n = L // GRID_W
    rows = jnp.repeat(jnp.arange(rows_n), GRID_W).astype(jnp.float32)
    cols = jnp.tile(jnp.arange(GRID_W), rows_n).astype(jnp.float32)
    half = HEAD_DIM // 2
    nf = half // 2
    inv = ROPE_BASE ** (-jnp.arange(nf, dtype=jnp.float32) / nf)

    def rot(xp, pos):
        ang = pos[:, None] * inv[None, :]
        cos = jnp.cos(ang)[None, :, None, :]
        sin = jnp.sin(ang)[None, :, None, :]
        x1, x2 = xp[..., :nf], xp[..., nf:]
        return jnp.concatenate([x1 * cos - x2 * sin, x2 * cos + x1 * sin], axis=-1)

    out = jnp.concatenate([rot(x[..., :half], rows), rot(x[..., half:], cols)], axis=-1)
    return out.astype(x.dtype)


def sink_softmax(s, sink_b):
    full = jnp.concatenate([s, jnp.broadcast_to(sink_b, s.shape[:-1] + (1,))], axis=-1)
    return jax.nn.softmax(full, axis=-1)[..., :-1]


def window_attention(q, k, v, kc, vc, sink):
    B_, S_ = q.shape[0], q.shape[1]
    nblk = S_ // ATT_BLOCK
    rep = ATT_HEADS // ATT_KV_HEADS
    scale = HEAD_DIM ** -0.5
    band = ATT_BLOCK + 2 * WINDOW
    qg = q.reshape(B_, S_, ATT_KV_HEADS, rep, HEAD_DIM)
    pad = ((0, 0), (WINDOW, WINDOW), (0, 0), (0, 0))
    kp = jnp.pad(k, pad)
    vp = jnp.pad(v, pad)
    sink_b = sink.astype(jnp.float32).reshape(ATT_KV_HEADS, rep)[None, :, :, None, None]
    n_loc = band

    def block(j):
        start = j * ATT_BLOCK
        qb = lax.dynamic_slice_in_dim(qg, start, ATT_BLOCK, axis=1)
        kb = lax.dynamic_slice_in_dim(kp, start, band, axis=1)
        vb = lax.dynamic_slice_in_dim(vp, start, band, axis=1)
        qpos = start + jnp.arange(ATT_BLOCK)
        kpos = start - WINDOW + jnp.arange(band)
        mask = ((jnp.abs(qpos[:, None] - kpos[None, :]) <= WINDOW)
                & (kpos >= 0)[None, :] & (kpos < S_)[None, :])
        s_loc = jnp.einsum('bqgrd,bkgd->bgrqk', qb, kb).astype(jnp.float32) * scale
        s_loc = jnp.where(mask, s_loc, -jnp.inf)
        s_ctx = jnp.einsum('bqgrd,bkgd->bgrqk', qb, kc).astype(jnp.float32) * scale
        p = sink_softmax(jnp.concatenate([s_loc, s_ctx], axis=-1), sink_b).astype(v.dtype)
        out = (jnp.einsum('bgrqk,bkgd->bqgrd', p[..., :n_loc], vb)
               + jnp.einsum('bgrqk,bkgd->bqgrd', p[..., n_loc:], vc))
        return out.reshape(B_, ATT_BLOCK, ATT_HEADS * HEAD_DIM)

    out = lax.map(block, jnp.arange(nblk))
    return out.transpose(1, 0, 2, 3).reshape(B_, S_, ATT_HEADS * HEAD_DIM)


def context_attention(qc, kc, vc, sink):
    B_, Lc = qc.shape[0], qc.shape[1]
    rep = ATT_HEADS // ATT_KV_HEADS
    qg = qc.reshape(B_, Lc, ATT_KV_HEADS, rep, HEAD_DIM)
    s = jnp.einsum('bqgrd,bkgd->bgrqk', qg, kc).astype(jnp.float32) * HEAD_DIM ** -0.5
    sink_b = sink.astype(jnp.float32).reshape(ATT_KV_HEADS, rep)[None, :, :, None, None]
    p = sink_softmax(s, sink_b).astype(vc.dtype)
    out = jnp.einsum('bgrqk,bkgd->bqgrd', p, vc)
    return out.reshape(B_, Lc, ATT_HEADS * HEAD_DIM)


def forget_terms(z, lb):
    zf = z.astype(jnp.float32)
    lbf = lb.astype(jnp.float32)
    logf = jnp.logaddexp(jnp.log(lbf), jnp.log1p(-lbf) + jax.nn.log_sigmoid(zf))
    k = (1.0 - lbf) * jax.nn.sigmoid(-zf)
    return logf, k


def hgrn_inputs(bq, bzf, bzb, bi, lb_f, lb_b):
    q = jax.nn.silu(bq.astype(jnp.float32)) * HG_DK ** -0.5
    logf_f, k_f = forget_terms(bzf, lb_f)
    logf_b, k_b = forget_terms(bzb, lb_b)
    v = bi.astype(jnp.float32)
    return (heads(q, HG_HEADS), heads(k_f, HG_HEADS), heads(k_b, HG_HEADS), heads(v, HG_HEADS),
            heads(logf_f, HG_HEADS), heads(logf_b, HG_HEADS))


def gla_chunk_scan(q, k, v, logf, s0):
    B_, L, H, _ = q.shape
    DV = v.shape[-1]
    n = L // HG_CHUNK

    def chunks(t):
        return t.reshape(B_, n, HG_CHUNK, H, t.shape[-1]).transpose(1, 0, 3, 2, 4)

    lower = jnp.tril(jnp.ones((HG_CHUNK, HG_CHUNK), dtype=bool))[:, :, None]

    def step(state, inp):
        qc, kc, vc, gc = inp
        G = jnp.cumsum(gc, axis=2)
        inter = jnp.einsum('bhtk,bhkv->bhtv', qc * jnp.exp(G), state)
        diff = G[:, :, :, None, :] - G[:, :, None, :, :]
        decay = jnp.exp(jnp.where(lower, diff, -jnp.inf))
        att = jnp.einsum('bhtk,bhtsk,bhsk->bhts', qc, decay, kc)
        intra = jnp.einsum('bhts,bhsv->bhtv', att, vc)
        G_last = G[:, :, -1:, :]
        new = (jnp.exp(G_last[:, :, 0, :])[..., None] * state
               + jnp.einsum('bhsk,bhsv->bhkv', kc * jnp.exp(G_last - G), vc))
        return new, inter + intra

    s_fin, o = lax.scan(step, s0, (chunks(q), chunks(k), chunks(v), chunks(logf)))
    return o.transpose(1, 0, 3, 2, 4).reshape(B_, L, H, DV), s_fin


def hgrn2_bidir(q, k_f, k_b, v, logf_f, logf_b, s_f, s_b):
    o_f, sf = gla_chunk_scan(q, k_f, v, logf_f, s_f)
    flip = lambda t: jnp.flip(t, axis=1)
    o_b, sb = gla_chunk_scan(flip(q), flip(k_b), flip(v), flip(logf_b), s_b)
    return o_f + flip(o_b), sf, sb


def hgrn_readout(o, g, hg_g, dtype):
    of = o * lax.rsqrt(jnp.mean(o * o, axis=-1, keepdims=True) + EPS)
    of = of * hg_g.astype(jnp.float32).reshape(HG_HEADS, HG_DV)
    of = of.reshape(o.shape[0], o.shape[1], D_BRANCH)
    return (of * jax.nn.silu(g.astype(jnp.float32))).astype(dtype)


def chunk_mlp(u, v, v_g, v_b, ws, bs):
    B_, L = v.shape[0], v.shape[1]
    n = L // MLP_CHUNK
    vn = layernorm(v, v_g, v_b).reshape(B_, n, MLP_CHUNK, MLP_GROUPS, MLP_GROUP_DIM)
    s = jnp.einsum('gts,bnsgc->bntgc', ws, vn) + bs.T[:, :, None]
    return u * s.reshape(B_, L, D_BRANCH)


def merge(a, b, c, gate, w_br, w_o):
    g = jax.nn.sigmoid(gate.reshape(gate.shape[:-1] + (N_BRANCH, D_MODEL)))
    m = g[..., 0, :] * (a @ w_br[0]) + g[..., 1, :] * (b @ w_br[1]) + g[..., 2, :] * (c @ w_br[2])
    return m @ w_o


def hybrid_mixer(h, hc, w_in, sink, lb_f, lb_b, hg_g, v_g, v_b, ws, bs, w_br, w_o, need_ctx):
    aq, ak, av, bq, bzf, bzb, bi, bg, cu, cv, gt = split_proj(h @ w_in)
    aqc, akc, avc, bqc, bzfc, bzbc, bic, bgc, cuc, cvc, gtc = split_proj(hc @ w_in)
    kc = heads(akc, ATT_KV_HEADS)
    vc = heads(avc, ATT_KV_HEADS)
    a = window_attention(axial_rope(heads(aq, ATT_HEADS)), axial_rope(heads(ak, ATT_KV_HEADS)),
                         heads(av, ATT_KV_HEADS), kc, vc, sink)
    zero = jnp.zeros((hc.shape[0], HG_HEADS, HG_DK, HG_DV), jnp.float32)
    oc, s_f, s_b = hgrn2_bidir(*hgrn_inputs(bqc, bzfc, bzbc, bic, lb_f, lb_b), zero, zero)
    o, _, _ = hgrn2_bidir(*hgrn_inputs(bq, bzf, bzb, bi, lb_f, lb_b), s_f, s_b)
    b = hgrn_readout(o, bg, hg_g, h.dtype)
    c = chunk_mlp(cu, cv, v_g, v_b, ws, bs)
    y = merge(a, b, c, gt, w_br, w_o)
    if not need_ctx:
        return y, None
    ac = context_attention(heads(aqc, ATT_HEADS), kc, vc, sink)
    bc = hgrn_readout(oc, bgc, hg_g, hc.dtype)
    cc = chunk_mlp(cuc, cvc, v_g, v_b, ws, bs)
    yc = merge(ac, bc, cc, gtc, w_br, w_o)
    return y, yc


def swiglu(h, w_gate, w_up, w_down):
    return (jax.nn.silu(h @ w_gate) * (h @ w_up)) @ w_down


def setup_inputs(seed: int = 0) -> dict:
    key = jax.random.key(seed)
    ks = jax.random.split(key, 24)
    nrm = lambda k, shape, s: jax.random.normal(k, shape, jnp.float32) * s
    D = D_MODEL
    return {
        'x': nrm(ks[0], (BATCH, SEQ, D), 1.0),
        'c': nrm(ks[1], (BATCH, D), 1.0),
        'ctx': nrm(ks[2], (BATCH, CTX_LEN, D), 1.0),
        'c_ctx': nrm(ks[3], (D,), 1.0),
        'w_ada': nrm(ks[4], (DEPTH, D, 6 * D), 0.5 * D ** -0.5),
        'b_ada': nrm(ks[5], (DEPTH, 6 * D), 0.02),
        'norm1_g': 1.0 + nrm(ks[6], (DEPTH, D), 0.02),
        'norm2_g': 1.0 + nrm(ks[7], (DEPTH, D), 0.02),
        'w_in': nrm(ks[8], (DEPTH, D, D_IN), D ** -0.5),
        'attn_sink': nrm(ks[9], (DEPTH, ATT_HEADS), 1.0),
        'hg_lb_logits': nrm(ks[10], (DEPTH, 2, D_BRANCH), 1.0),
        'hg_norm_g': 1.0 + nrm(ks[11], (DEPTH, D_BRANCH), 0.02),
        'mlp_v_norm_g': 1.0 + nrm(ks[12], (DEPTH, D_BRANCH), 0.02),
        'mlp_v_norm_b': nrm(ks[13], (DEPTH, D_BRANCH), 0.02),
        'mlp_ws': nrm(ks[14], (DEPTH, MLP_GROUPS, MLP_CHUNK, MLP_CHUNK), MLP_CHUNK ** -0.5),
        'mlp_bs': 1.0 + nrm(ks[15], (DEPTH, MLP_GROUPS, MLP_CHUNK), 0.02),
        'w_branch': nrm(ks[16], (DEPTH, N_BRANCH, D_BRANCH, D), D_BRANCH ** -0.5),
        'w_out': nrm(ks[17], (DEPTH, D, D), D ** -0.5),
        'w_ffn_gate': nrm(ks[18], (DEPTH, D, D_FF), D ** -0.5),
        'w_ffn_up': nrm(ks[19], (DEPTH, D, D_FF), D ** -0.5),
        'w_ffn_down': nrm(ks[20], (DEPTH, D_FF, D), D_FF ** -0.5),
        'final_norm_g': 1.0 + nrm(ks[21], (D,), 0.02),
    }


def reference(x, c, ctx, c_ctx, w_ada, b_ada, norm1_g, norm2_g, w_in, attn_sink, hg_lb_logits,
              hg_norm_g, mlp_v_norm_g, mlp_v_norm_b, mlp_ws, mlp_bs, w_branch, w_out,
              w_ffn_gate, w_ffn_up, w_ffn_down, final_norm_g):
    lb = jnp.cumsum(jax.nn.softmax(hg_lb_logits.astype(jnp.float32), axis=0), axis=0)
    lb = lb - lb[0]
    sc = jax.nn.silu(c)[:, None, :]
    scc = jax.nn.silu(c_ctx)
    for l in range(DEPTH):
        need_ctx = l < DEPTH - 1
        sh1, sc1, g1, sh2, sc2, g2 = jnp.split(sc @ w_ada[l] + b_ada[l], 6, axis=-1)
        csh1, csc1, cg1, csh2, csc2, cg2 = jnp.split(scc @ w_ada[l] + b_ada[l], 6, axis=-1)
        h = rmsnorm(x, norm1_g[l]) * (1.0 + sc1) + sh1
        hc = rmsnorm(ctx, norm1_g[l]) * (1.0 + csc1) + csh1
        y, yc = hybrid_mixer(h, hc, w_in[l], attn_sink[l], lb[l, 0], lb[l, 1], hg_norm_g[l],
                             mlp_v_norm_g[l], mlp_v_norm_b[l], mlp_ws[l], mlp_bs[l],
                             w_branch[l], w_out[l], need_ctx)
        x = x + g1 * y
        h = rmsnorm(x, norm2_g[l]) * (1.0 + sc2) + sh2
        x = x + g2 * swiglu(h, w_ffn_gate[l], w_ffn_up[l], w_ffn_down[l])
        if need_ctx:
            ctx = ctx + cg1 * yc
            hc = rmsnorm(ctx, norm2_g[l]) * (1.0 + csc2) + csh2
            ctx = ctx + cg2 * swiglu(hc, w_ffn_gate[l], w_ffn_up[l], w_ffn_down[l])
    return rmsnorm(x, final_norm_g)
```

```python
import functools

import jax
import jax.numpy as jnp
from jax import lax
from jax.experimental import pallas as pl
from jax.experimental.pallas import tpu as pltpu

F32 = jnp.float32
BF16 = jnp.bfloat16

EPS = 1e-6
GRID_W = 64
HEAD_DIM = 64
ATT_HEADS = 8
ATT_KV_HEADS = 2
WINDOW = 128
ATT_BLOCK = 128
ROPE_BASE = 10000.0
HG_DK = 128
HG_HEADS = 4
HG_CHUNK = 64
HG_DIAG = 8
MLP_CHUNK = 128
MLP_GROUPS = 4
N_BRANCH = 3

LANES = 128
MOD_ROWS = 24
VMEM_LIMIT = 56 * 1024 * 1024
NEG = -1e30

Q_HEAD_ORDER = (0, 4, 1, 5, 2, 6, 3, 7)


def _sigmoid(x):
    return 1.0 / (1.0 + jnp.exp(-x))


def _silu(x):
    return x * _sigmoid(x)


def _rms_mod(x, g, sc, sh):
    y = x * lax.rsqrt(jnp.mean(x * x, axis=-1, keepdims=True) + EPS)
    return (y * g) * (1.0 + sc) + sh


def _dot(a, b):
    return jnp.dot(a, b, preferred_element_type=F32)


def _dot_nt(a, b):
    return lax.dot_general(a, b, (((1,), (1,)), ((), ())), preferred_element_type=F32)


def _dot_tn(a, b):
    return lax.dot_general(a, b, (((0,), (0,)), ((), ())), preferred_element_type=F32)


def _resident(shape):
    nd = len(shape)
    return pl.BlockSpec(shape, lambda *_: (0,) * nd, pipeline_mode=pl.Buffered(1))


def _params(*sem):
    return pltpu.CompilerParams(dimension_semantics=sem, vmem_limit_bytes=VMEM_LIMIT)


def _ada_kernel(c_ref, w_ref, b_ref, o_ref):
    s = _silu(c_ref[...])
    o_ref[...] = jnp.dot(s, w_ref[...], precision=lax.Precision.HIGHEST,
                         preferred_element_type=F32) + b_ref[...]


def _ada(cs, w_ada, b_ada):
    depth, d, d6 = w_ada.shape
    bn = 1024
    return pl.pallas_call(
        _ada_kernel,
        grid=(depth, d6 // bn),
        in_specs=[
            pl.BlockSpec((MOD_ROWS, d), lambda l, j: (0, 0)),
            pl.BlockSpec((None, d, bn), lambda l, j: (l, 0, j)),
            pl.BlockSpec((None, 1, bn), lambda l, j: (l, 0, j)),
        ],
        out_specs=pl.BlockSpec((None, MOD_ROWS, bn), lambda l, j: (l, 0, j)),
        out_shape=jax.ShapeDtypeStruct((depth, MOD_ROWS, d6), F32),
        compiler_params=_params("parallel", "parallel"),
        name="ada",
    )(cs, w_ada, b_ada.reshape(depth, 1, d6))


def _rope(x, cos, sa, sb):
    return x * cos + pltpu.roll(x, 16, 1) * sa + pltpu.roll(x, LANES - 16, 1) * sb


def _forget_terms(z, lb):
    la = jnp.maximum(jnp.log(lb), NEG)
    b = jnp.log1p(-lb) + (jnp.minimum(z, 0.0) - jnp.log1p(jnp.exp(-jnp.abs(z))))
    logf = jnp.maximum(la, b) + jnp.log1p(jnp.exp(-jnp.abs(la - b)))
    k = (1.0 - lb) * _sigmoid(-z)
    return logf, k


def _in_proj_kernel(x_ref, mod_ref, g_ref, w_ref, lb_ref, vg_ref, vb_ref, cos_ref, sa_ref, sb_ref,
                    qa_ref, ka_ref, va_ref, hq_ref, hkf_ref, hkb_ref, hlf_ref, hlb_ref, hv_ref,
                    hg_ref, mu_ref, mvn_ref, gate_ref, *, rope, d_br):
    h = _rms_mod(x_ref[...], g_ref[...], mod_ref[0, 1:2, :], mod_ref[0, 0:1, :]).astype(BF16)

    def proj(lo, width):
        return _dot(h, w_ref[:, lo:lo + width])

    def maybe_rope(t):
        if not rope:
            return t
        return _rope(t, cos_ref[...], sa_ref[...], sb_ref[...])

    off = 0
    acc = proj(off, d_br)
    for c in range(d_br // LANES):
        sl = slice(c * LANES, (c + 1) * LANES)
        qa_ref[:, sl] = (maybe_rope(acc[:, sl]) * HEAD_DIM ** -0.5).astype(BF16)
    off += d_br
    acc = proj(off, 2 * LANES)
    ka_ref[...] = maybe_rope(acc[:, :LANES]).astype(BF16)
    va_ref[...] = acc[:, LANES:].astype(BF16)
    off += 2 * LANES
    hq_ref[...] = (_silu(proj(off, d_br)) * HG_DK ** -0.5).astype(BF16)
    off += d_br
    for which, (l_ref, k_ref) in enumerate(((hlf_ref, hkf_ref), (hlb_ref, hkb_ref))):
        logf, k = _forget_terms(proj(off, d_br), lb_ref[which:which + 1, :])
        l_ref[...] = logf
        k_ref[...] = k.astype(BF16)
        off += d_br
    hv_ref[...] = proj(off, d_br).astype(BF16)
    off += d_br
    hg_ref[...] = _silu(proj(off, d_br)).astype(BF16)
    off += d_br
    mu_ref[...] = proj(off, d_br).astype(BF16)
    off += d_br
    v = proj(off, d_br)
    mean = jnp.mean(v, axis=-1, keepdims=True)
    var = jnp.mean(jnp.square(v - mean), axis=-1, keepdims=True)
    mvn_ref[...] = ((v - mean) * lax.rsqrt(var + EPS) * vg_ref[...] + vb_ref[...]).astype(BF16)
    off += d_br
    for c in range(gate_ref.shape[1] // d_br):
        gate_ref[:, c * d_br:(c + 1) * d_br] = _sigmoid(proj(off, d_br)).astype(BF16)
        off += d_br


def _in_proj(x2, mod, norm_g, w_in, lb, v_g, v_b, tables, *, seq, tm, rope):
    rows, d = x2.shape
    d_in = w_in.shape[1]
    d_br = d // 2
    per = seq // tm
    row = lambda w: pl.BlockSpec((tm, w), lambda i: (i, 0))
    tab = pl.BlockSpec((tm, LANES), lambda i: (i % per, 0))
    widths = dict(qa=d_br, ka=LANES, va=LANES, hq=d_br, hkf=d_br, hkb=d_br, hlf=d_br, hlb=d_br,
                  hv=d_br, hg=d_br, mu=d_br, mvn=d_br, gate=N_BRANCH * d)
    dtypes = {k: (F32 if k in ("hlf", "hlb") else BF16) for k in widths}
    outs = pl.pallas_call(
        functools.partial(_in_proj_kernel, rope=rope, d_br=d_br),
        grid=(rows // tm,),
        in_specs=[
            row(d),
            pl.BlockSpec((1, 6, d), lambda i: (i // per, 0, 0)),
            _resident((1, d)),
            _resident((d, d_in)),
            _resident((2, d_br)),
            _resident((1, d_br)),
            _resident((1, d_br)),
            tab, tab, tab,
        ],
        out_specs=[row(w) for w in widths.values()],
        out_shape=[jax.ShapeDtypeStruct((rows, w), dtypes[k]) for k, w in widths.items()],
        compiler_params=_params("parallel"),
        name="in_proj_rope" if rope else "in_proj_ctx",
    )(x2, mod, norm_g, w_in, lb, v_g, v_b, *tables)
    return dict(zip(widths, outs))


def _attn_kernel(sink_ref, q_ref, *refs, band):
    if band:
        kl_ref, km_ref, kr_ref, vl_ref, vm_ref, vr_ref, kc_ref, vc_ref, o_ref = refs
        j = pl.program_id(1)
        nq = pl.num_programs(1)
        keys = jnp.concatenate([kl_ref[...], km_ref[...], kr_ref[...], kc_ref[...]], axis=0)
        vals = jnp.concatenate([vl_ref[...], vm_ref[...], vr_ref[...], vc_ref[...]], axis=0)
        qi = lax.broadcasted_iota(jnp.int32, (ATT_BLOCK, ATT_BLOCK), 0)
        kj = lax.broadcasted_iota(jnp.int32, (ATT_BLOCK, ATT_BLOCK), 1)
        left = jnp.where(jnp.logical_and(kj >= qi, j > 0), 0.0, NEG)
        right = jnp.where(jnp.logical_and(kj <= qi, j < nq - 1), 0.0, NEG)
        bias = jnp.concatenate([left, jnp.zeros((ATT_BLOCK, ATT_BLOCK), F32), right,
                                jnp.zeros((ATT_BLOCK, kc_ref.shape[0]), F32)], axis=1)
        bias = jnp.concatenate([bias, bias], axis=0)
    else:
        kc_ref, vc_ref, o_ref = refs
        keys = kc_ref[...]
        vals = vc_ref[...]
        bias = None
    tq = q_ref.shape[0]
    left_half = lax.broadcasted_iota(jnp.int32, (tq, LANES), 1) < HEAD_DIM
    top = lax.broadcasted_iota(jnp.int32, (2 * tq, 1), 0) < tq
    rep = ATT_HEADS // ATT_KV_HEADS
    for g in range(q_ref.shape[1] // LANES):
        qc = q_ref[:, g * LANES:(g + 1) * LANES]
        zero = jnp.zeros_like(qc)
        q2 = jnp.concatenate([jnp.where(left_half, qc, zero), jnp.where(left_half, zero, qc)], axis=0)
        s = _dot_nt(q2, keys)
        if bias is not None:
            s = s + bias
        sink = jnp.where(top, sink_ref[g], sink_ref[g + rep])
        m = jnp.maximum(jnp.max(s, axis=-1, keepdims=True), sink)
        e = jnp.exp(s - m)
        den = jnp.sum(e, axis=-1, keepdims=True) + jnp.exp(sink - m)
        o = _dot(e.astype(BF16), vals) * (1.0 / den)
        o_ref[:, g * LANES:(g + 1) * LANES] = jnp.where(left_half, o[:tq], o[tq:]).astype(BF16)


def _attention(q, k, v, kc, vc, sink, *, batch, seq, ctx_len, band):
    d_q = q.shape[1]
    tq = ATT_BLOCK
    nq = seq // tq
    smem = pl.BlockSpec(memory_space=pltpu.SMEM)
    qspec = pl.BlockSpec((tq, d_q), lambda b, j: (b * nq + j, 0))
    cspec = pl.BlockSpec((ctx_len, LANES), lambda b, j: (b, 0))
    if band:
        kvspecs = [
            pl.BlockSpec((tq, LANES), lambda b, j: (b * nq + jnp.maximum(j - 1, 0), 0)),
            pl.BlockSpec((tq, LANES), lambda b, j: (b * nq + j, 0)),
            pl.BlockSpec((tq, LANES), lambda b, j: (b * nq + jnp.minimum(j + 1, nq - 1), 0)),
        ]
        in_specs = [smem, qspec] + kvspecs + kvspecs + [cspec, cspec]
        args = (sink, q, k, k, k, v, v, v, kc, vc)
    else:
        in_specs = [smem, qspec, cspec, cspec]
        args = (sink, q, kc, vc)
    return pl.pallas_call(
        functools.partial(_attn_kernel, band=band),
        grid=(batch, nq),
        in_specs=in_specs,
        out_specs=qspec,
        out_shape=jax.ShapeDtypeStruct(q.shape, BF16),
        compiler_params=_params("parallel", "parallel"),
        name="attn_band" if band else "attn_ctx",
    )(*args)


def _level_masks(rev):
    c = HG_CHUNK
    ti = lax.broadcasted_iota(jnp.int32, (c, c), 0)
    si = lax.broadcasted_iota(jnp.int32, (c, c), 1)
    masks = []
    b = c // 2
    while b >= HG_DIAG:
        same = (ti & -(2 * b)) == (si & -(2 * b))
        t_hi = (ti & b) != 0
        s_hi = (si & b) != 0
        if rev:
            pair = jnp.logical_and(jnp.logical_not(t_hi), s_hi)
        else:
            pair = jnp.logical_and(t_hi, jnp.logical_not(s_hi))
        masks.append((b, jnp.logical_and(same, pair)))
        b //= 2
    return masks


def _hgrn_head(q, k, v, g_ref, lanes, state, masks, rev):
    c = HG_CHUNK
    g = g_ref[:, lanes]
    edge = 0 if rev else c - 1
    g_end = g_ref[edge:edge + 1, lanes]
    inter = _dot((q * jnp.exp(g)).astype(BF16), state.astype(BF16))
    k_dec = k * jnp.exp(g_end - g)
    dec_t = jnp.exp(jnp.broadcast_to(g_end, (HG_DK, LANES)).T)
    new_state = dec_t * state + _dot_tn(k_dec.astype(BF16), v.astype(BF16))

    att = jnp.zeros((c, c), F32)
    for b, mask in masks:
        pieces = []
        for p in range(c // (2 * b)):
            r = (2 * p + 1) * b - (0 if rev else 1)
            pieces.append(jnp.broadcast_to(g_ref[r:r + 1, lanes], (2 * b, LANES)))
        e = jnp.exp(-jnp.abs(g - jnp.concatenate(pieces, axis=0)))
        att = att + jnp.where(mask, _dot_nt((q * e).astype(BF16), (k * e).astype(BF16)), 0.0)
    out = inter + _dot(att.astype(BF16), v.astype(BF16))

    tmod = lax.broadcasted_iota(jnp.int32, (c, 1), 0) & (HG_DIAG - 1)
    out = out + jnp.sum(q * k, axis=-1, keepdims=True) * v
    for d in range(1, HG_DIAG):
        shift = c - d if rev else d
        valid = (tmod < HG_DIAG - d) if rev else (tmod >= d)
        kd = pltpu.roll(k, shift, 0)
        gd = pltpu.roll(g, shift, 0)
        vd = pltpu.roll(v, shift, 0)
        p = jnp.where(valid, q * kd * jnp.exp(g - gd), 0.0)
        out = out + jnp.sum(p, axis=-1, keepdims=True) * vd
    return out, new_state


def _hgrn_kernel(qf_ref, kf_ref, lf_ref, vf_ref, qb_ref, kb_ref, lb_ref, vb_ref, s0_ref,
                 of_ref, ob_ref, sfin_ref, state_ref, g_ref):
    i = pl.program_id(1)

    @pl.when(i == 0)
    def _():
        state_ref[...] = s0_ref[...]

    c = HG_CHUNK
    ti = lax.broadcasted_iota(jnp.int32, (c, c), 0)
    si = lax.broadcasted_iota(jnp.int32, (c, c), 1)
    dirs = ((qf_ref, kf_ref, lf_ref, vf_ref, of_ref, False),
            (qb_ref, kb_ref, lb_ref, vb_ref, ob_ref, True))
    for d, (q_ref, k_ref, l_ref, v_ref, o_ref, rev) in enumerate(dirs):
        tri = ((si >= ti) if rev else (si <= ti)).astype(F32)
        g_ref[...] = jnp.dot(tri, l_ref[...], precision=lax.Precision.HIGHEST,
                             preferred_element_type=F32)
        masks = _level_masks(rev)
        for h in range(HG_HEADS):
            lanes = slice(h * LANES, (h + 1) * LANES)
            out, new_state = _hgrn_head(q_ref[:, lanes].astype(F32), k_ref[:, lanes].astype(F32),
                                        v_ref[:, lanes].astype(F32), g_ref, lanes,
                                        state_ref[d, h], masks, rev)
            state_ref[d, h] = new_state
            o_ref[:, lanes] = out

    @pl.when(i == pl.num_programs(1) - 1)
    def _():
        sfin_ref[...] = state_ref[...]


def _hgrn(p, s0, *, batch, seq):
    c = HG_CHUNK
    n = seq // c
    width = p["hq"].shape[1]
    fwd = pl.BlockSpec((c, width), lambda b, i: (b * n + i, 0))
    bwd = pl.BlockSpec((c, width), lambda b, i: (b * n + (n - 1 - i), 0))
    st = pl.BlockSpec((None, 2, HG_HEADS, HG_DK, LANES), lambda b, i: (b, 0, 0, 0, 0))
    rows = batch * seq
    return pl.pallas_call(
        _hgrn_kernel,
        grid=(batch, n),
        in_specs=[fwd, fwd, fwd, fwd, bwd, bwd, bwd, bwd, st],
        out_specs=[fwd, bwd, st],
        out_shape=[jax.ShapeDtypeStruct((rows, width), F32),
                   jax.ShapeDtypeStruct((rows, width), F32),
                   jax.ShapeDtypeStruct(s0.shape, F32)],
        scratch_shapes=[pltpu.VMEM((2, HG_HEADS, HG_DK, LANES), F32), pltpu.VMEM((c, width), F32)],
        compiler_params=_params("parallel", "arbitrary"),
        name="hgrn_scan",
    )(p["hq"], p["hkf"], p["hlf"], p["hv"], p["hq"], p["hkb"], p["hlb"], p["hv"], s0)


def _merge_kernel(x_ref, mod_ref, a_ref, of_ref, ob_ref, hg_ref, mu_ref, mvn_ref, gate_ref,
                  hgn_ref, ws_ref, bs_ref, wbr_ref, wo_ref, o_ref):
    tm, d = x_ref.shape
    o = of_ref[...] + ob_ref[...]
    parts = []
    for h in range(HG_HEADS):
        oh = o[:, h * LANES:(h + 1) * LANES]
        parts.append(oh * lax.rsqrt(jnp.mean(oh * oh, axis=-1, keepdims=True) + EPS))
    b_br = (jnp.concatenate(parts, axis=1) * hgn_ref[...] * hg_ref[...].astype(F32)).astype(BF16)
    rows = []
    for r in range(tm // MLP_CHUNK):
        rs = slice(r * MLP_CHUNK, (r + 1) * MLP_CHUNK)
        cols = []
        for grp in range(MLP_GROUPS):
            cs = slice(grp * LANES, (grp + 1) * LANES)
            cols.append(_dot(ws_ref[grp], mvn_ref[rs, cs]) + bs_ref[grp])
        rows.append(jnp.concatenate(cols, axis=1))
    c_br = (mu_ref[...].astype(F32) * jnp.concatenate(rows, axis=0)).astype(BF16)
    m = jnp.zeros((tm, d), F32)
    for n, br in enumerate((a_ref[...], b_br, c_br)):
        m = m + gate_ref[:, n * d:(n + 1) * d].astype(F32) * _dot(br, wbr_ref[n])
    y = _dot(m.astype(BF16), wo_ref[...])
    o_ref[...] = x_ref[...] + mod_ref[0, 2:3, :] * y


def _merge(x2, mod, a, of, ob, p, hg_g, ws, bs, w_br, w_o, *, seq, tm):
    rows, d = x2.shape
    d_br = d // 2
    per = seq // tm
    row = lambda w: pl.BlockSpec((tm, w), lambda i: (i, 0))
    return pl.pallas_call(
        _merge_kernel,
        grid=(rows // tm,),
        in_specs=[
            row(d),
            pl.BlockSpec((1, 6, d), lambda i: (i // per, 0, 0)),
            row(d_br), row(d_br), row(d_br), row(d_br), row(d_br), row(d_br), row(N_BRANCH * d),
            _resident((1, d_br)),
            _resident(ws.shape),
            _resident(bs.shape),
            _resident(w_br.shape),
            _resident(w_o.shape),
        ],
        out_specs=row(d),
        out_shape=jax.ShapeDtypeStruct((rows, d), F32),
        compiler_params=_params("parallel"),
        name="merge",
    )(x2, mod, a, of, ob, p["hg"], p["mu"], p["mvn"], p["gate"], hg_g, ws, bs, w_br, w_o)


def _ffn_kernel(x_ref, mod_ref, g_ref, wg_ref, wu_ref, wd_ref, fg_ref, o_ref, act_ref, *, final, bf):
    x = x_ref[...]
    h = _rms_mod(x, g_ref[...], mod_ref[0, 4:5, :], mod_ref[0, 3:4, :]).astype(BF16)
    for c in range(wg_ref.shape[1] // bf):
        sl = slice(c * bf, (c + 1) * bf)
        act_ref[:, sl] = (_silu(_dot(h, wg_ref[:, sl])) * _dot(h, wu_ref[:, sl])).astype(BF16)
    y = x + mod_ref[0, 5:6, :] * _dot(act_ref[...], wd_ref[...])
    if final:
        y = y * lax.rsqrt(jnp.mean(y * y, axis=-1, keepdims=True) + EPS) * fg_ref[...]
    o_ref[...] = y


def _ffn(x2, mod, norm_g, w_gate, w_up, w_down, final_g, *, seq, tm, final):
    rows, d = x2.shape
    d_ff = w_gate.shape[1]
    per = seq // tm
    row = pl.BlockSpec((tm, d), lambda i: (i, 0))
    return pl.pallas_call(
        functools.partial(_ffn_kernel, final=final, bf=256),
        grid=(rows // tm,),
        in_specs=[
            row,
            pl.BlockSpec((1, 6, d), lambda i: (i // per, 0, 0)),
            _resident((1, d)),
            _resident(w_gate.shape),
            _resident(w_up.shape),
            _resident(w_down.shape),
            _resident((1, d)),
        ],
        out_specs=row,
        out_shape=jax.ShapeDtypeStruct((rows, d), F32),
        scratch_shapes=[pltpu.VMEM((tm, d_ff), BF16)],
        compiler_params=_params("parallel"),
        name="ffn_final" if final else "ffn",
    )(x2, mod, norm_g, w_gate, w_up, w_down, final_g)


def _rope_tables(seq):
    pos = jnp.arange(seq)
    nf = HEAD_DIM // 4
    inv = ROPE_BASE ** (-jnp.arange(nf, dtype=F32) / nf)
    ang_r = (pos // GRID_W).astype(F32)[:, None] * inv[None, :]
    ang_c = (pos % GRID_W).astype(F32)[:, None] * inv[None, :]
    zero = jnp.zeros_like(ang_r)
    cos = jnp.concatenate([jnp.cos(ang_r)] * 2 + [jnp.cos(ang_c)] * 2, axis=1)
    sa = jnp.concatenate([zero, jnp.sin(ang_r), zero, jnp.sin(ang_c)], axis=1)
    sb = jnp.concatenate([-jnp.sin(ang_r), zero, -jnp.sin(ang_c), zero], axis=1)
    reps = LANES // HEAD_DIM
    return tuple(jnp.tile(t, (1, reps)) for t in (cos, sa, sb))


def kernel(x, c, ctx, c_ctx, w_ada, b_ada, norm1_g, norm2_g, w_in, attn_sink, hg_lb_logits, hg_norm_g, mlp_v_norm_g, mlp_v_norm_b, mlp_ws, mlp_bs, w_branch, w_out, w_ffn_gate, w_ffn_up, w_ffn_down, final_norm_g):
    batch, seq, d = x.shape
    ctx_len = ctx.shape[1]
    depth = w_ada.shape[0]
    d_br = d // 2
    assert batch + 1 <= MOD_ROWS and seq % 512 == 0 and ctx_len % 256 == 0

    lb = jnp.cumsum(jax.nn.softmax(hg_lb_logits.astype(F32), axis=0), axis=0)
    lb = lb - lb[0]
    cs = jnp.concatenate([c, c_ctx[None, :], jnp.zeros((MOD_ROWS - batch - 1, d), F32)], axis=0)
    mod = _ada(cs, w_ada, b_ada).reshape(depth, MOD_ROWS, 6, d)
    tables = _rope_tables(seq)
    q_cols = jnp.concatenate([jnp.arange(h * HEAD_DIM, (h + 1) * HEAD_DIM) for h in Q_HEAD_ORDER])
    zero_state = jnp.zeros((batch, 2, HG_HEADS, HG_DK, LANES), F32)

    xs = x.reshape(batch * seq, d)
    cx = ctx.reshape(batch * ctx_len, d)
    for l in range(depth):
        need_ctx = l < depth - 1
        mod_x = mod[l, :batch]
        mod_c = jnp.broadcast_to(mod[l, batch:batch + 1], (batch, 6, d))
        w_in_l = jnp.concatenate([w_in[l][:, q_cols], w_in[l][:, d_br:]], axis=1).astype(BF16)
        w_br_l = jnp.concatenate([w_branch[l, :1][:, q_cols], w_branch[l, 1:]], axis=0).astype(BF16)
        ws_l = mlp_ws[l].astype(BF16)
        bs_l = jnp.broadcast_to(mlp_bs[l][:, :, None], (MLP_GROUPS, MLP_CHUNK, LANES))
        n1 = norm1_g[l][None, :]
        n2 = norm2_g[l][None, :]
        vg = mlp_v_norm_g[l][None, :]
        vb = mlp_v_norm_b[l][None, :]
        hgn = hg_norm_g[l][None, :]
        w_o_l = w_out[l].astype(BF16)
        wg_l, wu_l, wd_l = (w[l].astype(BF16) for w in (w_ffn_gate, w_ffn_up, w_ffn_down))
        fg = final_norm_g[None, :]

        pc = _in_proj(cx, mod_c, n1, w_in_l, lb[l], vg, vb, tables, seq=ctx_len, tm=256, rope=False)
        px = _in_proj(xs, mod_x, n1, w_in_l, lb[l], vg, vb, tables, seq=seq, tm=512, rope=True)
        a = _attention(px["qa"], px["ka"], px["va"], pc["ka"], pc["va"], attn_sink[l],
                       batch=batch, seq=seq, ctx_len=ctx_len, band=True)
        ocf, ocb, s_ctx = _hgrn(pc, zero_state, batch=batch, seq=ctx_len)
        of, ob, _ = _hgrn(px, s_ctx, batch=batch, seq=seq)
        xs = _merge(xs, mod_x, a, of, ob, px, hgn, ws_l, bs_l, w_br_l, w_o_l, seq=seq, tm=256)
        xs = _ffn(xs, mod_x, n2, wg_l, wu_l, wd_l, fg, seq=seq, tm=512, final=not need_ctx)
        if need_ctx:
            ac = _attention(pc["qa"], None, None, pc["ka"], pc["va"], attn_sink[l],
                            batch=batch, seq=ctx_len, ctx_len=ctx_len, band=False)
            cx = _merge(cx, mod_c, ac, ocf, ocb, pc, hgn, ws_l, bs_l, w_br_l, w_o_l,
                        seq=ctx_len, tm=256)
            cx = _ffn(cx, mod_c, n2, wg_l, wu_l, wd_l, fg, seq=ctx_len, tm=256, final=False)
    return xs.reshape(batch, seq, d)
```

```python
import functools

import jax
import jax.numpy as jnp
import numpy as np
from jax import lax
from jax.experimental import pallas as pl
from jax.experimental.pallas import tpu as pltpu

F32 = jnp.float32
BF16 = jnp.bfloat16

EPS = 1e-6
GRID_W = 64
HEAD_DIM = 64
ATT_HEADS = 8
ATT_KV_HEADS = 2
WINDOW = 128
ATT_BLOCK = 128
ROPE_BASE = 10000.0
HG_DK = 128
HG_HEADS = 4
HG_CHUNK = 128
MLP_CHUNK = 128
MLP_GROUPS = 4
N_BRANCH = 3

LANES = 128
MOD_ROWS = 24
VMEM_LIMIT = 56 * 1024 * 1024
NEG = -1e30
LOG2E = 1.4426950408889634

Q_HEAD_ORDER = (0, 4, 1, 5, 2, 6, 3, 7)


def _sigmoid(x):
    return 1.0 / (1.0 + jnp.exp(-x))


def _silu(x):
    return x * _sigmoid(x)


def _rms_mod(x, g, sc, sh):
    y = x * lax.rsqrt(jnp.mean(x * x, axis=-1, keepdims=True) + EPS)
    return (y * g) * (1.0 + sc) + sh


def _dot(a, b):
    return jnp.dot(a, b, preferred_element_type=F32)


def _dot_nt(a, b):
    return lax.dot_general(a, b, (((1,), (1,)), ((), ())), preferred_element_type=F32)


def _dot_tn(a, b):
    return lax.dot_general(a, b, (((0,), (0,)), ((), ())), preferred_element_type=F32)


def _resident(shape):
    nd = len(shape)
    return pl.BlockSpec(shape, lambda *_: (0,) * nd, pipeline_mode=pl.Buffered(1))


def _params(*sem):
    return pltpu.CompilerParams(dimension_semantics=sem, vmem_limit_bytes=VMEM_LIMIT)


def _ada_kernel(c_ref, w_ref, b_ref, o_ref):
    s = _silu(c_ref[...])
    o_ref[...] = jnp.dot(s, w_ref[...], precision=lax.Precision.HIGHEST,
                         preferred_element_type=F32) + b_ref[...]


def _ada(cs, w_ada, b_ada):
    depth, d, d6 = w_ada.shape
    bn = 1024
    return pl.pallas_call(
        _ada_kernel,
        grid=(depth, d6 // bn),
        in_specs=[
            pl.BlockSpec((MOD_ROWS, d), lambda l, j: (0, 0)),
            pl.BlockSpec((None, d, bn), lambda l, j: (l, 0, j)),
            pl.BlockSpec((None, 1, bn), lambda l, j: (l, 0, j)),
        ],
        out_specs=pl.BlockSpec((None, MOD_ROWS, bn), lambda l, j: (l, 0, j)),
        out_shape=jax.ShapeDtypeStruct((depth, MOD_ROWS, d6), F32),
        compiler_params=_params("parallel", "parallel"),
        name="ada",
    )(cs, w_ada, b_ada.reshape(depth, 1, d6))


def _rope(x, cos, sa, sb):
    return x * cos + pltpu.roll(x, 16, 1) * sa + pltpu.roll(x, LANES - 16, 1) * sb


def _forget_terms(z, lb):
    la = jnp.maximum(jnp.log(lb), NEG)
    b = jnp.log1p(-lb) + (jnp.minimum(z, 0.0) - jnp.log1p(jnp.exp(-jnp.abs(z))))
    logf = jnp.maximum(la, b) + jnp.log1p(jnp.exp(-jnp.abs(la - b)))
    k = (1.0 - lb) * _sigmoid(-z)
    return logf, k


def _in_proj_kernel(x_ref, mod_ref, g_ref, w_ref, lb_ref, vg_ref, vb_ref, cos_ref, sa_ref, sb_ref,
                    qa_ref, ka_ref, va_ref, hq_ref, hkf_ref, hkb_ref, hlf_ref, hlb_ref, hv_ref,
                    hg_ref, mu_ref, mvn_ref, gate_ref, *, rope, d_br):
    h = _rms_mod(x_ref[...], g_ref[...], mod_ref[0, 1:2, :], mod_ref[0, 0:1, :]).astype(BF16)

    def proj(lo, width):
        return _dot(h, w_ref[:, lo:lo + width])

    def maybe_rope(t):
        if not rope:
            return t
        return _rope(t, cos_ref[...], sa_ref[...], sb_ref[...])

    off = 0
    acc = proj(off, d_br)
    for c in range(d_br // LANES):
        sl = slice(c * LANES, (c + 1) * LANES)
        qa_ref[:, sl] = (maybe_rope(acc[:, sl]) * HEAD_DIM ** -0.5).astype(BF16)
    off += d_br
    acc = proj(off, 2 * LANES)
    ka_ref[...] = maybe_rope(acc[:, :LANES]).astype(BF16)
    va_ref[...] = acc[:, LANES:].astype(BF16)
    off += 2 * LANES
    hq_ref[...] = (_silu(proj(off, d_br)) * HG_DK ** -0.5).astype(BF16)
    off += d_br
    for which, (l_ref, k_ref) in enumerate(((hlf_ref, hkf_ref), (hlb_ref, hkb_ref))):
        logf, k = _forget_terms(proj(off, d_br), lb_ref[which:which + 1, :])
        l_ref[...] = logf * LOG2E
        k_ref[...] = k.astype(BF16)
        off += d_br
    hv_ref[...] = proj(off, d_br).astype(BF16)
    off += d_br
    hg_ref[...] = _silu(proj(off, d_br)).astype(BF16)
    off += d_br
    mu_ref[...] = proj(off, d_br).astype(BF16)
    off += d_br
    v = proj(off, d_br)
    mean = jnp.mean(v, axis=-1, keepdims=True)
    var = jnp.mean(jnp.square(v - mean), axis=-1, keepdims=True)
    mvn_ref[...] = ((v - mean) * lax.rsqrt(var + EPS) * vg_ref[...] + vb_ref[...]).astype(BF16)
    off += d_br
    for c in range(gate_ref.shape[1] // d_br):
        gate_ref[:, c * d_br:(c + 1) * d_br] = _sigmoid(proj(off, d_br)).astype(BF16)
        off += d_br


def _in_proj(x2, mod, norm_g, w_in, lb, v_g, v_b, tables, *, seq, tm, rope):
    rows, d = x2.shape
    d_in = w_in.shape[1]
    d_br = d // 2
    per = seq // tm
    row = lambda w: pl.BlockSpec((tm, w), lambda i: (i, 0))
    tab = pl.BlockSpec((tm, LANES), lambda i: (i % per, 0))
    widths = dict(qa=d_br, ka=LANES, va=LANES, hq=d_br, hkf=d_br, hkb=d_br, hlf=d_br, hlb=d_br,
                  hv=d_br, hg=d_br, mu=d_br, mvn=d_br, gate=N_BRANCH * d)
    dtypes = {k: (F32 if k in ("hlf", "hlb") else BF16) for k in widths}
    outs = pl.pallas_call(
        functools.partial(_in_proj_kernel, rope=rope, d_br=d_br),
        grid=(rows // tm,),
        in_specs=[
            row(d),
            pl.BlockSpec((1, 6, d), lambda i: (i // per, 0, 0)),
            _resident((1, d)),
            _resident((d, d_in)),
            _resident((2, d_br)),
            _resident((1, d_br)),
            _resident((1, d_br)),
            tab, tab, tab,
        ],
        out_specs=[row(w) for w in widths.values()],
        out_shape=[jax.ShapeDtypeStruct((rows, w), dtypes[k]) for k, w in widths.items()],
        compiler_params=_params("parallel"),
        name="in_proj_rope" if rope else "in_proj_ctx",
    )(x2, mod, norm_g, w_in, lb, v_g, v_b, *tables)
    return dict(zip(widths, outs))


def _attn_kernel(sink_ref, q_ref, *refs, band):
    if band:
        kl_ref, km_ref, kr_ref, vl_ref, vm_ref, vr_ref, kc_ref, vc_ref, o_ref = refs
        j = pl.program_id(1)
        nq = pl.num_programs(1)
        keys = jnp.concatenate([kl_ref[...], km_ref[...], kr_ref[...], kc_ref[...]], axis=0)
        vals = jnp.concatenate([vl_ref[...], vm_ref[...], vr_ref[...], vc_ref[...]], axis=0)
        qi = lax.broadcasted_iota(jnp.int32, (ATT_BLOCK, ATT_BLOCK), 0)
        kj = lax.broadcasted_iota(jnp.int32, (ATT_BLOCK, ATT_BLOCK), 1)
        left = jnp.where(jnp.logical_and(kj >= qi, j > 0), 0.0, NEG)
        right = jnp.where(jnp.logical_and(kj <= qi, j < nq - 1), 0.0, NEG)
        bias = jnp.concatenate([left, jnp.zeros((ATT_BLOCK, ATT_BLOCK), F32), right,
                                jnp.zeros((ATT_BLOCK, kc_ref.shape[0]), F32)], axis=1)
        bias = jnp.concatenate([bias, bias], axis=0)
    else:
        kc_ref, vc_ref, o_ref = refs
        keys = kc_ref[...]
        vals = vc_ref[...]
        bias = None
    tq = q_ref.shape[0]
    left_half = lax.broadcasted_iota(jnp.int32, (tq, LANES), 1) < HEAD_DIM
    top = lax.broadcasted_iota(jnp.int32, (2 * tq, 1), 0) < tq
    rep = ATT_HEADS // ATT_KV_HEADS
    for g in range(q_ref.shape[1] // LANES):
        qc = q_ref[:, g * LANES:(g + 1) * LANES]
        zero = jnp.zeros_like(qc)
        q2 = jnp.concatenate([jnp.where(left_half, qc, zero), jnp.where(left_half, zero, qc)], axis=0)
        s = _dot_nt(q2, keys)
        if bias is not None:
            s = s + bias
        sink = jnp.where(top, sink_ref[g], sink_ref[g + rep])
        m = jnp.maximum(jnp.max(s, axis=-1, keepdims=True), sink)
        e = jnp.exp(s - m)
        den = jnp.sum(e, axis=-1, keepdims=True) + jnp.exp(sink - m)
        o = _dot(e.astype(BF16), vals) * (1.0 / den)
        o_ref[:, g * LANES:(g + 1) * LANES] = jnp.where(left_half, o[:tq], o[tq:]).astype(BF16)


def _attention(q, k, v, kc, vc, sink, *, batch, seq, ctx_len, band):
    d_q = q.shape[1]
    tq = ATT_BLOCK
    nq = seq // tq
    smem = pl.BlockSpec(memory_space=pltpu.SMEM)
    qspec = pl.BlockSpec((tq, d_q), lambda b, j: (b * nq + j, 0))
    cspec = pl.BlockSpec((ctx_len, LANES), lambda b, j: (b, 0))
    if band:
        kvspecs = [
            pl.BlockSpec((tq, LANES), lambda b, j: (b * nq + jnp.maximum(j - 1, 0), 0)),
            pl.BlockSpec((tq, LANES), lambda b, j: (b * nq + j, 0)),
            pl.BlockSpec((tq, LANES), lambda b, j: (b * nq + jnp.minimum(j + 1, nq - 1), 0)),
        ]
        in_specs = [smem, qspec] + kvspecs + kvspecs + [cspec, cspec]
        args = (sink, q, k, k, k, v, v, v, kc, vc)
    else:
        in_specs = [smem, qspec, cspec, cspec]
        args = (sink, q, kc, vc)
    return pl.pallas_call(
        functools.partial(_attn_kernel, band=band),
        grid=(batch, nq),
        in_specs=in_specs,
        out_specs=qspec,
        out_shape=jax.ShapeDtypeStruct(q.shape, BF16),
        compiler_params=_params("parallel", "parallel"),
        name="attn_band" if band else "attn_ctx",
    )(*args)


HG_LEVELS = tuple(HG_CHUNK >> i for i in range(1, HG_CHUNK.bit_length()))


def _hgrn_tables():
    c = HG_CHUNK
    t = np.arange(c)[:, None]
    s = np.arange(c)[None, :]
    mats, lvls = [], []
    for rev in (False, True):
        lvl = np.full((c, c), -1, np.int32)
        lvl[np.arange(c), np.arange(c)] = 0
        for idx, b in enumerate(HG_LEVELS):
            same = (t // (2 * b)) == (s // (2 * b))
            t_hi, s_hi = (t % (2 * b)) >= b, (s % (2 * b)) >= b
            lvl[same & ((~t_hi & s_hi) if rev else (t_hi & ~s_hi))] = idx + 1
        mat = ((s >= t) if rev else (s <= t)).astype(np.float32)
        mats.append(np.concatenate([mat, mat], axis=1))
        lvls.append(lvl)
    return np.stack(mats), np.stack(lvls)


def _boundary_exponent(g, g_ref, lg, lanes, b, rev):
    c = g.shape[0]
    row = lax.broadcasted_iota(jnp.int32, (c, 1), 0)
    bound = lambda g0: g0 + b - (0 if rev else 1)
    if b == 1:
        return jnp.where(((row & 1) != 0) != rev, lg, 0.0)
    if b >= 8:
        pieces = []
        for g0 in range(0, c, 2 * b):
            gr = g_ref[bound(g0):bound(g0) + 1, lanes]
            first, second = g[g0:g0 + b], g[g0 + b:g0 + 2 * b]
            pieces += [first - gr, gr - second] if rev else [gr - first, second - gr]
        return jnp.concatenate(pieces, axis=0)
    sub = 8
    pieces = []
    for r0 in range(0, c, sub):
        refs = [jnp.broadcast_to(g_ref[bound(g0):bound(g0) + 1, lanes], (sub, LANES))
                for g0 in range(r0, r0 + sub, 2 * b)]
        gr = refs[0]
        for n in range(1, len(refs)):
            gr = jnp.where((row[:sub] & (sub - 1)) >= n * 2 * b, refs[n], gr)
        pieces.append(gr)
    return -jnp.abs(g - jnp.concatenate(pieces, axis=0))


def _mixed_rows(q, k, b, rev):
    c = q.shape[0]
    if b >= 8:
        pieces = []
        for g0 in range(0, c, 2 * b):
            lo, hi = slice(g0, g0 + b), slice(g0 + b, g0 + 2 * b)
            pieces += [q[lo], k[hi]] if rev else [k[lo], q[hi]]
        return jnp.concatenate(pieces, axis=0)
    later = (lax.broadcasted_iota(jnp.int32, (c, 1), 0) & b) != 0
    return jnp.where(later, k, q) if rev else jnp.where(later, q, k)


def _hgrn_head(q, k, v, lg, g_ref, lanes, state, lvl, rev):
    c = HG_CHUNK
    g = g_ref[:, lanes]
    end = 0 if rev else c - 1
    g_end = g_ref[end:end + 1, lanes]
    qf, kf = q.astype(F32), k.astype(F32)
    blocks = [(q, k)]
    for b in HG_LEVELS:
        e = jnp.exp2(_boundary_exponent(g, g_ref, lg, lanes, b, rev))
        y = (_mixed_rows(qf, kf, b, rev) * e).astype(BF16)
        blocks.append((y, y))
    zero = jnp.zeros_like(k)
    att = jnp.zeros((c, c), F32)
    for i in range(0, len(blocks), 2):
        (la, ra), (lb, rb) = blocks[i], blocks[i + 1]
        rhs = jnp.concatenate([jnp.concatenate([ra, zero], axis=1),
                               jnp.concatenate([zero, rb], axis=1)], axis=0)
        s = _dot_nt(jnp.concatenate([la, lb], axis=1), rhs)
        att = jnp.where(lvl == i, s[:, :c], jnp.where(lvl == i + 1, s[:, c:], att))
    lhs = jnp.concatenate([(qf * jnp.exp2(g)).astype(BF16), att.astype(BF16)], axis=1)
    rhs = jnp.concatenate([state.astype(BF16), v], axis=0)
    out = _dot(lhs, rhs)
    dec_t = jnp.broadcast_to(jnp.exp2(g_end), (HG_DK, LANES)).T
    new_state = dec_t * state + _dot_tn((kf * jnp.exp2(g_end - g)).astype(BF16), v)
    return out, new_state


def _hgrn_kernel(qf_ref, kf_ref, lf_ref, vf_ref, qb_ref, kb_ref, lb_ref, vb_ref, s0_ref, m_ref, lvl_ref,
                 of_ref, ob_ref, sfin_ref, state_ref, g_ref):
    i = pl.program_id(1)

    @pl.when(i == 0)
    def _():
        state_ref[...] = s0_ref[...]

    dirs =((qf_ref, kf_ref, lf_ref, vf_ref, of_ref), (qb_ref, kb_ref, lb_ref, vb_ref, ob_ref))
    for d, (q_ref, k_ref, l_ref, v_ref, o_ref) in enumerate(dirs):
        lg = l_ref[...]
        hi = lg.astype(BF16)
        lo = (lg - hi.astype(F32)).astype(BF16)
        gd_ref = g_ref.at[d]
        gd_ref[...] = _dot(m_ref[d], jnp.concatenate([hi, lo], axis=0))
        lvl = lvl_ref[d]
        for h in range(HG_HEADS):
            lanes = slice(h * LANES, (h + 1) * LANES)
            out, new_state = _hgrn_head(q_ref[:, lanes], k_ref[:, lanes], v_ref[:, lanes], l_ref[:, lanes],
                                        gd_ref, lanes, state_ref[d, h], lvl, bool(d))
            state_ref[d, h] = new_state
            o_ref[:, lanes] = out

    @pl.when(i == pl.num_programs(1) - 1)
    def _():
        sfin_ref[...] = state_ref[...]


def _hgrn(p, s0, *, batch, seq):
    c = HG_CHUNK
    n = seq // c
    width = p["hq"].shape[1]
    mats, lvls = _hgrn_tables()
    fwd = pl.BlockSpec((c, width), lambda b, i: (b * n + i, 0))
    bwd = pl.BlockSpec((c, width), lambda b, i: (b * n + (n - 1 - i), 0))
    st = pl.BlockSpec((None, 2, HG_HEADS, HG_DK, LANES), lambda b, i: (b, 0, 0, 0, 0))
    rows = batch * seq
    return pl.pallas_call(
        _hgrn_kernel,
        grid=(batch, n),
        in_specs=[fwd, fwd, fwd, fwd, bwd, bwd, bwd, bwd, st, _resident(mats.shape), _resident(lvls.shape)],
        out_specs=[fwd, bwd, st],
        out_shape=[jax.ShapeDtypeStruct((rows, width), F32),
                   jax.ShapeDtypeStruct((rows, width), F32),
                   jax.ShapeDtypeStruct(s0.shape, F32)],
        scratch_shapes=[pltpu.VMEM((2, HG_HEADS, HG_DK, LANES), F32),
                        pltpu.VMEM((2, mats.shape[1], width), F32)],
        compiler_params=_params("parallel", "arbitrary"),
        name="hgrn_scan",
    )(p["hq"], p["hkf"], p["hlf"], p["hv"], p["hq"], p["hkb"], p["hlb"], p["hv"], s0,
      jnp.asarray(mats, BF16), jnp.asarray(lvls))


def _merge_kernel(x_ref, mod_ref, a_ref, of_ref, ob_ref, hg_ref, mu_ref, mvn_ref, gate_ref,
                  hgn_ref, ws_ref, bs_ref, wbr_ref, wo_ref, o_ref):
    tm, d = x_ref.shape
    o = of_ref[...] + ob_ref[...]
    parts = []
    for h in range(HG_HEADS):
        oh = o[:, h * LANES:(h + 1) * LANES]
        parts.append(oh * lax.rsqrt(jnp.mean(oh * oh, axis=-1, keepdims=True) + EPS))
    b_br = (jnp.concatenate(parts, axis=1) * hgn_ref[...] * hg_ref[...].astype(F32)).astype(BF16)
    rows = []
    for r in range(tm // MLP_CHUNK):
        rs = slice(r * MLP_CHUNK, (r + 1) * MLP_CHUNK)
        cols = []
        for grp in range(MLP_GROUPS):
            cs = slice(grp * LANES, (grp + 1) * LANES)
            cols.append(_dot(ws_ref[grp], mvn_ref[rs, cs]) + bs_ref[grp])
        rows.append(jnp.concatenate(cols, axis=1))
    c_br = (mu_ref[...].astype(F32) * jnp.concatenate(rows, axis=0)).astype(BF16)
    m = jnp.zeros((tm, d), F32)
    for n, br in enumerate((a_ref[...], b_br, c_br)):
        m = m + gate_ref[:, n * d:(n + 1) * d].astype(F32) * _dot(br, wbr_ref[n])
    y = _dot(m.astype(BF16), wo_ref[...])
    o_ref[...] = x_ref[...] + mod_ref[0, 2:3, :] * y


def _merge(x2, mod, a, of, ob, p, hg_g, ws, bs, w_br, w_o, *, seq, tm):
    rows, d = x2.shape
    d_br = d // 2
    per = seq // tm
    row = lambda w: pl.BlockSpec((tm, w), lambda i: (i, 0))
    return pl.pallas_call(
        _merge_kernel,
        grid=(rows // tm,),
        in_specs=[
            row(d),
            pl.BlockSpec((1, 6, d), lambda i: (i // per, 0, 0)),
            row(d_br), row(d_br), row(d_br), row(d_br), row(d_br), row(d_br), row(N_BRANCH * d),
            _resident((1, d_br)),
            _resident(ws.shape),
            _resident(bs.shape),
            _resident(w_br.shape),
            _resident(w_o.shape),
        ],
        out_specs=row(d),
        out_shape=jax.ShapeDtypeStruct((rows, d), F32),
        compiler_params=_params("parallel"),
        name="merge",
    )(x2, mod, a, of, ob, p["hg"], p["mu"], p["mvn"], p["gate"], hg_g, ws, bs, w_br, w_o)


def _ffn_kernel(x_ref, mod_ref, g_ref, wg_ref, wu_ref, wd_ref, fg_ref, o_ref, act_ref, *, final, bf):
    x = x_ref[...]
    h = _rms_mod(x, g_ref[...], mod_ref[0, 4:5, :], mod_ref[0, 3:4, :]).astype(BF16)
    for c in range(wg_ref.shape[1] // bf):
        sl = slice(c * bf, (c + 1) * bf)
        act_ref[:, sl] = (_silu(_dot(h, wg_ref[:, sl])) * _dot(h, wu_ref[:, sl])).astype(BF16)
    y = x + mod_ref[0, 5:6, :] * _dot(act_ref[...], wd_ref[...])
    if final:
        y = y * lax.rsqrt(jnp.mean(y * y, axis=-1, keepdims=True) + EPS) * fg_ref[...]
    o_ref[...] = y


def _ffn(x2, mod, norm_g, w_gate, w_up, w_down, final_g, *, seq, tm, final):
    rows, d = x2.shape
    d_ff = w_gate.shape[1]
    per = seq // tm
    row = pl.BlockSpec((tm, d), lambda i: (i, 0))
    return pl.pallas_call(
        functools.partial(_ffn_kernel, final=final, bf=256),
        grid=(rows // tm,),
        in_specs=[
            row,
            pl.BlockSpec((1, 6, d), lambda i: (i // per, 0, 0)),
            _resident((1, d)),
            _resident(w_gate.shape),
            _resident(w_up.shape),
            _resident(w_down.shape),
            _resident((1, d)),
        ],
        out_specs=row,
        out_shape=jax.ShapeDtypeStruct((rows, d), F32),
        scratch_shapes=[pltpu.VMEM((tm, d_ff), BF16)],
        compiler_params=_params("parallel"),
        name="ffn_final" if final else "ffn",
    )(x2, mod, norm_g, w_gate, w_up, w_down, final_g)


def _rope_tables(seq):
    pos = jnp.arange(seq)
    nf = HEAD_DIM // 4
    inv = ROPE_BASE ** (-jnp.arange(nf, dtype=F32) / nf)
    ang_r = (pos // GRID_W).astype(F32)[:, None] * inv[None, :]
    ang_c = (pos % GRID_W).astype(F32)[:, None] * inv[None, :]
    zero = jnp.zeros_like(ang_r)
    cos = jnp.concatenate([jnp.cos(ang_r)] * 2 + [jnp.cos(ang_c)] * 2, axis=1)
    sa = jnp.concatenate([zero, jnp.sin(ang_r), zero, jnp.sin(ang_c)], axis=1)
    sb = jnp.concatenate([-jnp.sin(ang_r), zero, -jnp.sin(ang_c), zero], axis=1)
    reps = LANES // HEAD_DIM
    return tuple(jnp.tile(t, (1, reps)) for t in (cos, sa, sb))


def kernel(x, c, ctx, c_ctx, w_ada, b_ada, norm1_g, norm2_g, w_in, attn_sink, hg_lb_logits, hg_norm_g, mlp_v_norm_g, mlp_v_norm_b, mlp_ws, mlp_bs, w_branch, w_out, w_ffn_gate, w_ffn_up, w_ffn_down, final_norm_g):
    batch, seq, d = x.shape
    ctx_len = ctx.shape[1]
    depth = w_ada.shape[0]
    d_br = d // 2
    assert batch + 1 <= MOD_ROWS and seq % 512 == 0 and ctx_len % 256 == 0

    lb = jnp.cumsum(jax.nn.softmax(hg_lb_logits.astype(F32), axis=0), axis=0)
    lb = lb - lb[0]
    cs = jnp.concatenate([c, c_ctx[None, :], jnp.zeros((MOD_ROWS - batch - 1, d), F32)], axis=0)
    mod = _ada(cs, w_ada, b_ada).reshape(depth, MOD_ROWS, 6, d)
    tables = _rope_tables(seq)
    q_cols = jnp.concatenate([jnp.arange(h * HEAD_DIM, (h + 1) * HEAD_DIM) for h in Q_HEAD_ORDER])
    zero_state = jnp.zeros((batch, 2, HG_HEADS, HG_DK, LANES), F32)

    xs = x.reshape(batch * seq, d)
    cx = ctx.reshape(batch * ctx_len, d)
    for l in range(depth):
        need_ctx = l < depth - 1
        mod_x = mod[l, :batch]
        mod_c = jnp.broadcast_to(mod[l, batch:batch + 1], (batch, 6, d))
        w_in_l = jnp.concatenate([w_in[l][:, q_cols], w_in[l][:, d_br:]], axis=1).astype(BF16)
        w_br_l = jnp.concatenate([w_branch[l, :1][:, q_cols], w_branch[l, 1:]], axis=0).astype(BF16)
        ws_l = mlp_ws[l].astype(BF16)
        bs_l = jnp.broadcast_to(mlp_bs[l][:, :, None], (MLP_GROUPS, MLP_CHUNK, LANES))
        n1 = norm1_g[l][None, :]
        n2 = norm2_g[l][None, :]
        vg = mlp_v_norm_g[l][None, :]
        vb = mlp_v_norm_b[l][None, :]
        hgn = hg_norm_g[l][None, :]
        w_o_l = w_out[l].astype(BF16)
        wg_l, wu_l, wd_l = (w[l].astype(BF16) for w in (w_ffn_gate, w_ffn_up, w_ffn_down))
        fg = final_norm_g[None, :]

        pc = _in_proj(cx, mod_c, n1, w_in_l, lb[l], vg, vb, tables, seq=ctx_len, tm=256, rope=False)
        px = _in_proj(xs, mod_x, n1, w_in_l, lb[l], vg, vb, tables, seq=seq, tm=512, rope=True)
        a = _attention(px["qa"], px["ka"], px["va"], pc["ka"], pc["va"], attn_sink[l],
                       batch=batch, seq=seq, ctx_len=ctx_len, band=True)
        ocf, ocb, s_ctx = _hgrn(pc, zero_state, batch=batch, seq=ctx_len)
        of, ob, _ = _hgrn(px, s_ctx, batch=batch, seq=seq)
        xs = _merge(xs, mod_x, a, of, ob, px, hgn, ws_l, bs_l, w_br_l, w_o_l, seq=seq, tm=256)
        xs = _ffn(xs, mod_x, n2, wg_l, wu_l, wd_l, fg, seq=seq, tm=512, final=not need_ctx)
        if need_ctx:
            ac = _attention(pc["qa"], None, None, pc["ka"], pc["va"], attn_sink[l],
                            batch=batch, seq=ctx_len, ctx_len=ctx_len, band=False)
            cx = _merge(cx, mod_c, ac, ocf, ocb, pc, hgn, ws_l, bs_l, w_br_l, w_o_l,
                        seq=ctx_len, tm=256)
            cx = _ffn(cx, mod_c, n2, wg_l, wu_l, wd_l, fg, seq=ctx_len, tm=256, final=False)
    return xs.reshape(batch, seq, d)
```

```python
import functools

import jax
import jax.numpy as jnp
import numpy as np
from jax import lax
from jax.experimental import pallas as pl
from jax.experimental.pallas import tpu as pltpu

F32 = jnp.float32
BF16 = jnp.bfloat16

EPS = 1e-6
GRID_W = 64
HEAD_DIM = 64
ATT_HEADS = 8
ATT_KV_HEADS = 2
WINDOW = 128
ATT_BLOCK = 128
ROPE_BASE = 10000.0
HG_DK = 128
HG_HEADS = 4
HG_CHUNK = 128
MLP_CHUNK = 128
MLP_GROUPS = 4
N_BRANCH = 3

LANES = 128
MXU_COLS = 256
MOD_ROWS = 24
VMEM_LIMIT = 56 * 1024 * 1024
NEG = -1e30
LOG2E = 1.4426950408889634

Q_HEAD_ORDER = (0, 4, 1, 5, 2, 6, 3, 7)


def _sigmoid(x):
    return 0.5 * jnp.tanh(0.5 * x) + 0.5


def _silu(x):
    return x * _sigmoid(x)


def _rms_mod(x, g, sc, sh):
    y = x * lax.rsqrt(jnp.mean(x * x, axis=-1, keepdims=True) + EPS)
    return (y * g) * (1.0 + sc) + sh


def _dot(a, b):
    return jnp.dot(a, b, preferred_element_type=F32)


def _dot_nt(a, b):
    return lax.dot_general(a, b, (((1,), (1,)), ((), ())), preferred_element_type=F32)


def _dot_tn(a, b):
    return lax.dot_general(a, b, (((0,), (0,)), ((), ())), preferred_element_type=F32)


def _resident(shape):
    nd = len(shape)
    return pl.BlockSpec(shape, lambda *_: (0,) * nd, pipeline_mode=pl.Buffered(1))


def _params(*sem):
    return pltpu.CompilerParams(dimension_semantics=sem, vmem_limit_bytes=VMEM_LIMIT)


def _ada_kernel(c_ref, w_ref, b_ref, o_ref):
    s = _silu(c_ref[...])
    o_ref[...] = jnp.dot(s, w_ref[...], precision=lax.Precision.HIGHEST,
                         preferred_element_type=F32) + b_ref[...]


def _ada(cs, w_ada, b_ada):
    depth, d, d6 = w_ada.shape
    bn = 1024
    return pl.pallas_call(
        _ada_kernel,
        grid=(depth, d6 // bn),
        in_specs=[
            pl.BlockSpec((MOD_ROWS, d), lambda l, j: (0, 0)),
            pl.BlockSpec((None, d, bn), lambda l, j: (l, 0, j)),
            pl.BlockSpec((None, 1, bn), lambda l, j: (l, 0, j)),
        ],
        out_specs=pl.BlockSpec((None, MOD_ROWS, bn), lambda l, j: (l, 0, j)),
        out_shape=jax.ShapeDtypeStruct((depth, MOD_ROWS, d6), F32),
        compiler_params=_params("parallel", "parallel"),
        name="ada",
    )(cs, w_ada, b_ada.reshape(depth, 1, d6))


def _rope(x, cos, sa, sb):
    return x * cos + pltpu.roll(x, 16, 1) * sa + pltpu.roll(x, LANES - 16, 1) * sb


def _forget_terms(z, lb):
    u = jnp.exp(-jnp.abs(z))
    one_u = 1.0 + u
    log_sig = jnp.minimum(z, 0.0) - jnp.log(one_u)
    la = jnp.maximum(jnp.log(lb), NEG)
    b = jnp.log1p(-lb) + log_sig
    logf = jnp.maximum(la, b) + jnp.log(1.0 + jnp.exp(-jnp.abs(la - b)))
    r = 1.0 / one_u
    k = (1.0 - lb) * jnp.where(z >= 0.0, u * r, r)
    return logf, k


def _in_proj_kernel(x_ref, mod_ref, g_ref, w_ref, lb_ref, vg_ref, vb_ref, cos_ref, sa_ref, sb_ref,
                    qa_ref, ka_ref, va_ref, hq_ref, hkf_ref, hkb_ref, hlf_ref, hlb_ref, hv_ref,
                    hg_ref, mu_ref, mvn_ref, gate_ref, *, rope, d_br):
    h = _rms_mod(x_ref[...], g_ref[...], mod_ref[0, 1:2, :], mod_ref[0, 0:1, :]).astype(BF16)
    cw = MXU_COLS

    def proj(lo):
        return _dot(h, w_ref[:, lo:lo + cw])

    def maybe_rope(t):
        if not rope:
            return t
        return _rope(t, cos_ref[...], sa_ref[...], sb_ref[...])

    def each(o_ref, lo, fn):
        width = o_ref.shape[1]
        for c in range(0, width, cw):
            o_ref[:, c:c + cw] = fn(proj(lo + c)).astype(o_ref.dtype)
        return lo + width

    def rope_lanes(acc):
        parts = [maybe_rope(acc[:, s:s + LANES]) for s in range(0, cw, LANES)]
        return jnp.concatenate(parts, axis=1) * HEAD_DIM ** -0.5

    off = each(qa_ref, 0, rope_lanes)
    acc = proj(off)
    ka_ref[...] = maybe_rope(acc[:, :LANES]).astype(BF16)
    va_ref[...] = acc[:, LANES:].astype(BF16)
    off += cw
    off = each(hq_ref, off, lambda a: _silu(a) * HG_DK ** -0.5)
    off_f, off_b, off_v, off_g, off_u = (off + n * d_br for n in range(5))
    plain = iter([(r, o + c, c) for r, o in ((hv_ref, off_v), (mu_ref, off_u)) for c in range(0, d_br, cw)])
    for which, (l_ref, k_ref, lo) in enumerate(((hlf_ref, hkf_ref, off_f), (hlb_ref, hkb_ref, off_b))):
        for c in range(0, d_br, cw):
            logf, k = _forget_terms(proj(lo + c), lb_ref[which:which + 1, c:c + cw])
            l_ref[:, c:c + cw] = logf * LOG2E
            k_ref[:, c:c + cw] = k.astype(BF16)
            p_ref, p_lo, p_c = next(plain)
            p_ref[:, p_c:p_c + cw] = proj(p_lo).astype(BF16)
    each(hg_ref, off_g, lambda a: _silu(a))
    off = off_u + d_br
    v = jnp.concatenate([proj(off + c) for c in range(0, d_br, cw)], axis=1)
    mean = jnp.mean(v, axis=-1, keepdims=True)
    var = jnp.mean(jnp.square(v - mean), axis=-1, keepdims=True)
    mvn_ref[...] = ((v - mean) * lax.rsqrt(var + EPS) * vg_ref[...] + vb_ref[...]).astype(BF16)
    off += d_br
    each(gate_ref, off, lambda a: _sigmoid(a))


def _in_proj(x2, mod, norm_g, w_in, lb, v_g, v_b, tables, *, seq, tm, rope):
    rows, d = x2.shape
    d_in = w_in.shape[1]
    d_br = d // 2
    per = seq // tm
    row = lambda w: pl.BlockSpec((tm, w), lambda i: (i, 0))
    tab = pl.BlockSpec((tm, LANES), lambda i: (i % per, 0))
    widths = dict(qa=d_br, ka=LANES, va=LANES, hq=d_br, hkf=d_br, hkb=d_br, hlf=d_br, hlb=d_br,
                  hv=d_br, hg=d_br, mu=d_br, mvn=d_br, gate=N_BRANCH * d)
    dtypes = {k: (F32 if k in ("hlf", "hlb") else BF16) for k in widths}
    outs = pl.pallas_call(
        functools.partial(_in_proj_kernel, rope=rope, d_br=d_br),
        grid=(rows // tm,),
        in_specs=[
            row(d),
            pl.BlockSpec((1, 6, d), lambda i: (i // per, 0, 0)),
            _resident((1, d)),
            _resident((d, d_in)),
            _resident((2, d_br)),
            _resident((1, d_br)),
            _resident((1, d_br)),
            tab, tab, tab,
        ],
        out_specs=[row(w) for w in widths.values()],
        out_shape=[jax.ShapeDtypeStruct((rows, w), dtypes[k]) for k, w in widths.items()],
        compiler_params=_params("parallel"),
        name="in_proj_rope" if rope else "in_proj_ctx",
    )(x2, mod, norm_g, w_in, lb, v_g, v_b, *tables)
    return dict(zip(widths, outs))


def _attn_kernel(sink_ref, q_ref, *refs, band):
    if band:
        kl_ref, km_ref, kr_ref, vl_ref, vm_ref, vr_ref, kc_ref, vc_ref, o_ref = refs
        j = pl.program_id(1)
        nq = pl.num_programs(1)
        keys = jnp.concatenate([kl_ref[...], km_ref[...], kr_ref[...], kc_ref[...]], axis=0)
        vals = jnp.concatenate([vl_ref[...], vm_ref[...], vr_ref[...], vc_ref[...]], axis=0)
        qi = lax.broadcasted_iota(jnp.int32, (ATT_BLOCK, ATT_BLOCK), 0)
        kj = lax.broadcasted_iota(jnp.int32, (ATT_BLOCK, ATT_BLOCK), 1)
        left = jnp.where(jnp.logical_and(kj >= qi, j > 0), 0.0, NEG)
        right = jnp.where(jnp.logical_and(kj <= qi, j < nq - 1), 0.0, NEG)
        bias = jnp.concatenate([left, jnp.zeros((ATT_BLOCK, ATT_BLOCK), F32), right,
                                jnp.zeros((ATT_BLOCK, kc_ref.shape[0]), F32)], axis=1)
        bias = jnp.concatenate([bias, bias], axis=0)
    else:
        kc_ref, vc_ref, o_ref = refs
        keys = kc_ref[...]
        vals = vc_ref[...]
        bias = None
    tq = q_ref.shape[0]
    left_half = lax.broadcasted_iota(jnp.int32, (tq, LANES), 1) < HEAD_DIM
    top = lax.broadcasted_iota(jnp.int32, (2 * tq, 1), 0) < tq
    rep = ATT_HEADS // ATT_KV_HEADS
    for g in range(q_ref.shape[1] // LANES):
        qc = q_ref[:, g * LANES:(g + 1) * LANES]
        zero = jnp.zeros_like(qc)
        q2 = jnp.concatenate([jnp.where(left_half, qc, zero), jnp.where(left_half, zero, qc)], axis=0)
        s = _dot_nt(q2, keys)
        if bias is not None:
            s = s + bias
        sink = jnp.where(top, sink_ref[g], sink_ref[g + rep])
        m = jnp.maximum(jnp.max(s, axis=-1, keepdims=True), sink)
        e = jnp.exp(s - m)
        den = jnp.sum(e, axis=-1, keepdims=True) + jnp.exp(sink - m)
        o = _dot(e.astype(BF16), vals) * (1.0 / den)
        o_ref[:, g * LANES:(g + 1) * LANES] = jnp.where(left_half, o[:tq], o[tq:]).astype(BF16)


def _attention(q, k, v, kc, vc, sink, *, batch, seq, ctx_len, band):
    d_q = q.shape[1]
    tq = ATT_BLOCK
    nq = seq // tq
    smem = pl.BlockSpec(memory_space=pltpu.SMEM)
    qspec = pl.BlockSpec((tq, d_q), lambda b, j: (b * nq + j, 0))
    cspec = pl.BlockSpec((ctx_len, LANES), lambda b, j: (b, 0))
    if band:
        kvspecs = [
            pl.BlockSpec((tq, LANES), lambda b, j: (b * nq + jnp.maximum(j - 1, 0), 0)),
            pl.BlockSpec((tq, LANES), lambda b, j: (b * nq + j, 0)),
            pl.BlockSpec((tq, LANES), lambda b, j: (b * nq + jnp.minimum(j + 1, nq - 1), 0)),
        ]
        in_specs = [smem, qspec] + kvspecs + kvspecs + [cspec, cspec]
        args = (sink, q, k, k, k, v, v, v, kc, vc)
    else:
        in_specs = [smem, qspec, cspec, cspec]
        args = (sink, q, kc, vc)
    return pl.pallas_call(
        functools.partial(_attn_kernel, band=band),
        grid=(batch, nq),
        in_specs=in_specs,
        out_specs=qspec,
        out_shape=jax.ShapeDtypeStruct(q.shape, BF16),
        compiler_params=_params("parallel", "parallel"),
        name="attn_band" if band else "attn_ctx",
    )(*args)


HG_LEVELS = tuple(HG_CHUNK >> i for i in range(1, HG_CHUNK.bit_length()))


def _hgrn_tables():
    c = HG_CHUNK
    t = np.arange(c)[:, None]
    s = np.arange(c)[None, :]
    mats, lvls = [], []
    for rev in (False, True):
        lvl = np.full((c, c), -1, np.int32)
        lvl[np.arange(c), np.arange(c)] = 0
        for idx, b in enumerate(HG_LEVELS):
            same = (t // (2 * b)) == (s // (2 * b))
            t_hi, s_hi = (t % (2 * b)) >= b, (s % (2 * b)) >= b
            lvl[same & ((~t_hi & s_hi) if rev else (t_hi & ~s_hi))] = idx + 1
        mat = ((s >= t) if rev else (s <= t)).astype(np.float32)
        mats.append(np.concatenate([mat, mat], axis=1))
        lvls.append(lvl)
    return np.stack(mats), np.stack(lvls)


def _boundary_exponent(g, g_ref, lg, lanes, b, rev):
    c = g.shape[0]
    row = lax.broadcasted_iota(jnp.int32, (c, 1), 0)
    bound = lambda g0: g0 + b - (0 if rev else 1)
    if b == 1:
        return jnp.where(((row & 1) != 0) != rev, lg, 0.0)
    if b >= 8:
        pieces = []
        for g0 in range(0, c, 2 * b):
            gr = g_ref[bound(g0):bound(g0) + 1, lanes]
            first, second = g[g0:g0 + b], g[g0 + b:g0 + 2 * b]
            pieces += [first - gr, gr - second] if rev else [gr - first, second - gr]
        return jnp.concatenate(pieces, axis=0)
    sub = 8
    pieces = []
    for r0 in range(0, c, sub):
        refs = [jnp.broadcast_to(g_ref[bound(g0):bound(g0) + 1, lanes], (sub, LANES))
                for g0 in range(r0, r0 + sub, 2 * b)]
        gr = refs[0]
        for n in range(1, len(refs)):
            gr = jnp.where((row[:sub] & (sub - 1)) >= n * 2 * b, refs[n], gr)
        pieces.append(gr)
    return -jnp.abs(g - jnp.concatenate(pieces, axis=0))


def _mixed_rows(q, k, b, rev):
    c = q.shape[0]
    if b >= 8:
        pieces = []
        for g0 in range(0, c, 2 * b):
            lo, hi = slice(g0, g0 + b), slice(g0 + b, g0 + 2 * b)
            pieces += [q[lo], k[hi]] if rev else [k[lo], q[hi]]
        return jnp.concatenate(pieces, axis=0)
    later = (lax.broadcasted_iota(jnp.int32, (c, 1), 0) & b) != 0
    return jnp.where(later, k, q) if rev else jnp.where(later, q, k)


def _hgrn_scores(q, k, lg, g_ref, lanes, lvl, rev):
    c = HG_CHUNK
    g = g_ref[:, lanes]
    qf, kf = q.astype(F32), k.astype(F32)
    blocks = [(q, k)]
    for b in HG_LEVELS:
        e = jnp.exp2(_boundary_exponent(g, g_ref, lg, lanes, b, rev))
        y = (_mixed_rows(qf, kf, b, rev) * e).astype(BF16)
        blocks.append((y, y))
    zero = jnp.zeros_like(k)
    att = jnp.zeros((c, c), F32)
    for i in range(0, len(blocks), 2):
        (la, ra), (lb, rb) = blocks[i], blocks[i + 1]
        rhs = jnp.concatenate([jnp.concatenate([ra, zero], axis=1),
                               jnp.concatenate([zero, rb], axis=1)], axis=0)
        s = _dot_nt(jnp.concatenate([la, lb], axis=1), rhs)
        att = jnp.where(lvl == i, s[:, :c], jnp.where(lvl == i + 1, s[:, c:], att))
    return att.astype(BF16)


def _hgrn_apply(q, k, v, att, g_ref, lanes, state, rev):
    g = g_ref[:, lanes]
    end = 0 if rev else HG_CHUNK - 1
    g_end = g_ref[end:end + 1, lanes]
    lhs = jnp.concatenate([(q.astype(F32) * jnp.exp2(g)).astype(BF16), att], axis=1)
    rhs = jnp.concatenate([state.astype(BF16), v], axis=0)
    out = _dot(lhs, rhs)
    dec_t = jnp.broadcast_to(jnp.exp2(g_end), (HG_DK, LANES)).T
    new_state = dec_t * state + _dot_tn((k.astype(F32) * jnp.exp2(g_end - g)).astype(BF16), v)
    return out, new_state


def _hgrn_kernel(qf_ref, kf_ref, lf_ref, vf_ref, qb_ref, kb_ref, lb_ref, vb_ref, s0_ref, m_ref, lvl_ref,
                 of_ref, ob_ref, sfin_ref, state_ref, g_ref):
    i = pl.program_id(1)

    @pl.when(i == 0)
    def _():
        state_ref[...] = s0_ref[...]

    dirs = ((qf_ref, kf_ref, lf_ref, vf_ref, of_ref), (qb_ref, kb_ref, lb_ref, vb_ref, ob_ref))
    work = []
    for d, (q_ref, k_ref, l_ref, v_ref, o_ref) in enumerate(dirs):
        lg = l_ref[...]
        hi = lg.astype(BF16)
        lo = (lg - hi.astype(F32)).astype(BF16)
        g_ref[d] = _dot(m_ref[d], jnp.concatenate([hi, lo], axis=0))
        for h in range(HG_HEADS):
            lanes = slice(h * LANES, (h + 1) * LANES)
            work.append((d, h, lanes, o_ref, q_ref[:, lanes], k_ref[:, lanes], v_ref[:, lanes],
                         lg[:, lanes], state_ref[d, h]))
    atts = [_hgrn_scores(q, k, lg, g_ref.at[d], lanes, lvl_ref[d], bool(d))
            for d, h, lanes, o_ref, q, k, v, lg, state in work]
    done = []
    for att, (d, h, lanes, o_ref, q, k, v, lg, state) in zip(atts, work):
        out, new_state = _hgrn_apply(q, k, v, att, g_ref.at[d], lanes, state, bool(d))
        done.append((d, h, lanes, o_ref, out, new_state))
    for d, h, lanes, o_ref, out, new_state in done:
        state_ref[d, h] = new_state
        o_ref[:, lanes] = out

    @pl.when(i == pl.num_programs(1) - 1)
    def _():
        sfin_ref[...] = state_ref[...]


def _hgrn(p, s0, *, batch, seq):
    c = HG_CHUNK
    n = seq // c
    width = p["hq"].shape[1]
    mats, lvls = _hgrn_tables()
    fwd = pl.BlockSpec((c, width), lambda b, i: (b * n + i, 0))
    bwd = pl.BlockSpec((c, width), lambda b, i: (b * n + (n - 1 - i), 0))
    st = pl.BlockSpec((None, 2, HG_HEADS, HG_DK, LANES), lambda b, i: (b, 0, 0, 0, 0))
    rows = batch * seq
    return pl.pallas_call(
        _hgrn_kernel,
        grid=(batch, n),
        in_specs=[fwd, fwd, fwd, fwd, bwd, bwd, bwd, bwd, st, _resident(mats.shape), _resident(lvls.shape)],
        out_specs=[fwd, bwd, st],
        out_shape=[jax.ShapeDtypeStruct((rows, width), F32),
                   jax.ShapeDtypeStruct((rows, width), F32),
                   jax.ShapeDtypeStruct(s0.shape, F32)],
        scratch_shapes=[pltpu.VMEM((2, HG_HEADS, HG_DK, LANES), F32),
                        pltpu.VMEM((2, mats.shape[1], width), F32)],
        compiler_params=_params("parallel", "arbitrary"),
        name="hgrn_scan",
    )(p["hq"], p["hkf"], p["hlf"], p["hv"], p["hq"], p["hkb"], p["hlb"], p["hv"], s0,
      jnp.asarray(mats, BF16), jnp.asarray(lvls))


def _merge_kernel(x_ref, mod_ref, a_ref, of_ref, ob_ref, hg_ref, mu_ref, mvn_ref, gate_ref,
                  hgn_ref, ws_ref, bs_ref, wbr_ref, wo_ref, o_ref):
    tm, d = x_ref.shape
    o = of_ref[...] + ob_ref[...]
    parts = []
    for h in range(HG_HEADS):
        oh = o[:, h * LANES:(h + 1) * LANES]
        parts.append(oh * lax.rsqrt(jnp.mean(oh * oh, axis=-1, keepdims=True) + EPS))
    b_br = (jnp.concatenate(parts, axis=1) * hgn_ref[...] * hg_ref[...].astype(F32)).astype(BF16)
    rows = []
    for r in range(tm // MLP_CHUNK):
        rs = slice(r * MLP_CHUNK, (r + 1) * MLP_CHUNK)
        cols = []
        for grp in range(MLP_GROUPS):
            cs = slice(grp * LANES, (grp + 1) * LANES)
            cols.append(_dot(ws_ref[grp], mvn_ref[rs, cs]) + bs_ref[grp])
        rows.append(jnp.concatenate(cols, axis=1))
    c_br = (mu_ref[...].astype(F32) * jnp.concatenate(rows, axis=0)).astype(BF16)
    m = jnp.zeros((tm, d), F32)
    for n, br in enumerate((a_ref[...], b_br, c_br)):
        m = m + gate_ref[:, n * d:(n + 1) * d].astype(F32) * _dot(br, wbr_ref[n])
    y = _dot(m.astype(BF16), wo_ref[...])
    o_ref[...] = x_ref[...] + mod_ref[0, 2:3, :] * y


def _merge(x2, mod, a, of, ob, p, hg_g, ws, bs, w_br, w_o, *, seq, tm):
    rows, d = x2.shape
    d_br = d // 2
    per = seq // tm
    row = lambda w: pl.BlockSpec((tm, w), lambda i: (i, 0))
    return pl.pallas_call(
        _merge_kernel,
        grid=(rows // tm,),
        in_specs=[
            row(d),
            pl.BlockSpec((1, 6, d), lambda i: (i // per, 0, 0)),
            row(d_br), row(d_br), row(d_br), row(d_br), row(d_br), row(d_br), row(N_BRANCH * d),
            _resident((1, d_br)),
            _resident(ws.shape),
            _resident(bs.shape),
            _resident(w_br.shape),
            _resident(w_o.shape),
        ],
        out_specs=row(d),
        out_shape=jax.ShapeDtypeStruct((rows, d), F32),
        compiler_params=_params("parallel"),
        name="merge",
    )(x2, mod, a, of, ob, p["hg"], p["mu"], p["mvn"], p["gate"], hg_g, ws, bs, w_br, w_o)


def _ffn_kernel(x_ref, mod_ref, g_ref, wg_ref, wu_ref, wd_ref, fg_ref, o_ref, act_ref, *, final, bf):
    x = x_ref[...]
    h = _rms_mod(x, g_ref[...], mod_ref[0, 4:5, :], mod_ref[0, 3:4, :]).astype(BF16)
    for c in range(wg_ref.shape[1] // bf):
        sl = slice(c * bf, (c + 1) * bf)
        act_ref[:, sl] = (_silu(_dot(h, wg_ref[:, sl])) * _dot(h, wu_ref[:, sl])).astype(BF16)
    y = x + mod_ref[0, 5:6, :] * _dot(act_ref[...], wd_ref[...])
    if final:
        y = y * lax.rsqrt(jnp.mean(y * y, axis=-1, keepdims=True) + EPS) * fg_ref[...]
    o_ref[...] = y


def _ffn(x2, mod, norm_g, w_gate, w_up, w_down, final_g, *, seq, tm, final):
    rows, d = x2.shape
    d_ff = w_gate.shape[1]
    per = seq // tm
    row = pl.BlockSpec((tm, d), lambda i: (i, 0))
    return pl.pallas_call(
        functools.partial(_ffn_kernel, final=final, bf=256),
        grid=(rows // tm,),
        in_specs=[
            row,
            pl.BlockSpec((1, 6, d), lambda i: (i // per, 0, 0)),
            _resident((1, d)),
            _resident(w_gate.shape),
            _resident(w_up.shape),
            _resident(w_down.shape),
            _resident((1, d)),
        ],
        out_specs=row,
        out_shape=jax.ShapeDtypeStruct((rows, d), F32),
        scratch_shapes=[pltpu.VMEM((tm, d_ff), BF16)],
        compiler_params=_params("parallel"),
        name="ffn_final" if final else "ffn",
    )(x2, mod, norm_g, w_gate, w_up, w_down, final_g)


def _rope_tables(seq):
    pos = jnp.arange(seq)
    nf = HEAD_DIM // 4
    inv = ROPE_BASE ** (-jnp.arange(nf, dtype=F32) / nf)
    ang_r = (pos // GRID_W).astype(F32)[:, None] * inv[None, :]
    ang_c = (pos % GRID_W).astype(F32)[:, None] * inv[None, :]
    zero = jnp.zeros_like(ang_r)
    cos = jnp.concatenate([jnp.cos(ang_r)] * 2 + [jnp.cos(ang_c)] * 2, axis=1)
    sa = jnp.concatenate([zero, jnp.sin(ang_r), zero, jnp.sin(ang_c)], axis=1)
    sb = jnp.concatenate([-jnp.sin(ang_r), zero, -jnp.sin(ang_c), zero], axis=1)
    reps = LANES // HEAD_DIM
    return tuple(jnp.tile(t, (1, reps)) for t in (cos, sa, sb))


def kernel(x, c, ctx, c_ctx, w_ada, b_ada, norm1_g, norm2_g, w_in, attn_sink, hg_lb_logits, hg_norm_g, mlp_v_norm_g, mlp_v_norm_b, mlp_ws, mlp_bs, w_branch, w_out, w_ffn_gate, w_ffn_up, w_ffn_down, final_norm_g):
    batch, seq, d = x.shape
    ctx_len = ctx.shape[1]
    depth = w_ada.shape[0]
    d_br = d // 2
    assert batch + 1 <= MOD_ROWS and seq % 512 == 0 and ctx_len % 256 == 0

    lb = jnp.cumsum(jax.nn.softmax(hg_lb_logits.astype(F32), axis=0), axis=0)
    lb = lb - lb[0]
    cs = jnp.concatenate([c, c_ctx[None, :], jnp.zeros((MOD_ROWS - batch - 1, d), F32)], axis=0)
    mod = _ada(cs, w_ada, b_ada).reshape(depth, MOD_ROWS, 6, d)
    tables = _rope_tables(seq)
    q_cols = jnp.concatenate([jnp.arange(h * HEAD_DIM, (h + 1) * HEAD_DIM) for h in Q_HEAD_ORDER])
    zero_state = jnp.zeros((batch, 2, HG_HEADS, HG_DK, LANES), F32)

    xs = x.reshape(batch * seq, d)
    cx = ctx.reshape(batch * ctx_len, d)
    for l in range(depth):
        need_ctx = l < depth - 1
        mod_x = mod[l, :batch]
        mod_c = jnp.broadcast_to(mod[l, batch:batch + 1], (batch, 6, d))
        w_in_l = jnp.concatenate([w_in[l][:, q_cols], w_in[l][:, d_br:]], axis=1).astype(BF16)
        w_br_l = jnp.concatenate([w_branch[l, :1][:, q_cols], w_branch[l, 1:]], axis=0).astype(BF16)
        ws_l = mlp_ws[l].astype(BF16)
        bs_l = jnp.broadcast_to(mlp_bs[l][:, :, None], (MLP_GROUPS, MLP_CHUNK, LANES))
        n1 = norm1_g[l][None, :]
        n2 = norm2_g[l][None, :]
        vg = mlp_v_norm_g[l][None, :]
        vb = mlp_v_norm_b[l][None, :]
        hgn = hg_norm_g[l][None, :]
        w_o_l = w_out[l].astype(BF16)
        wg_l, wu_l, wd_l = (w[l].astype(BF16) for w in (w_ffn_gate, w_ffn_up, w_ffn_down))
        fg = final_norm_g[None, :]

        pc = _in_proj(cx, mod_c, n1, w_in_l, lb[l], vg, vb, tables, seq=ctx_len, tm=256, rope=False)
        px = _in_proj(xs, mod_x, n1, w_in_l, lb[l], vg, vb, tables, seq=seq, tm=512, rope=True)
        a = _attention(px["qa"], px["ka"], px["va"], pc["ka"], pc["va"], attn_sink[l],
                       batch=batch, seq=seq, ctx_len=ctx_len, band=True)
        ocf, ocb, s_ctx = _hgrn(pc, zero_state, batch=batch, seq=ctx_len)
        of, ob, _ = _hgrn(px, s_ctx, batch=batch, seq=seq)
        xs = _merge(xs, mod_x, a, of, ob, px, hgn, ws_l, bs_l, w_br_l, w_o_l, seq=seq, tm=256)
        xs = _ffn(xs, mod_x, n2, wg_l, wu_l, wd_l, fg, seq=seq, tm=512, final=not need_ctx)
        if need_ctx:
            ac = _attention(pc["qa"], None, None, pc["ka"], pc["va"], attn_sink[l],
                            batch=batch, seq=ctx_len, ctx_len=ctx_len, band=False)
            cx = _merge(cx, mod_c, ac, ocf, ocb, pc, hgn, ws_l, bs_l, w_br_l, w_o_l,
                        seq=ctx_len, tm=256)
            cx = _ffn(cx, mod_c, n2, wg_l, wu_l, wd_l, fg, seq=ctx_len, tm=256, final=False)
    return xs.reshape(batch, seq, d)
```

```python
import functools

import jax
import jax.numpy as jnp
import numpy as np
from jax import lax
from jax.experimental import pallas as pl
from jax.experimental.pallas import tpu as pltpu

F32 = jnp.float32
BF16 = jnp.bfloat16

EPS = 1e-6
GRID_W = 64
HEAD_DIM = 64
ATT_HEADS = 8
ATT_KV_HEADS = 2
WINDOW = 128
ATT_BLOCK = 128
ROPE_BASE = 10000.0
HG_DK = 128
HG_HEADS = 4
HG_CHUNK = 128
MLP_CHUNK = 128
MLP_GROUPS = 4
N_BRANCH = 3

LANES = 128
MXU_COLS = 256
MOD_ROWS = 24
VMEM_LIMIT = 56 * 1024 * 1024
NEG = -1e30
LOG2E = 1.4426950408889634

Q_HEAD_ORDER = (0, 4, 1, 5, 2, 6, 3, 7)


def _sigmoid(x):
    return 0.5 * jnp.tanh(0.5 * x) + 0.5


def _silu(x):
    return x * _sigmoid(x)


def _rms_mod(x, g, sc, sh):
    y = x * lax.rsqrt(jnp.mean(x * x, axis=-1, keepdims=True) + EPS)
    return (y * g) * (1.0 + sc) + sh


def _dot(a, b):
    return jnp.dot(a, b, preferred_element_type=F32)


def _dot_nt(a, b):
    return lax.dot_general(a, b, (((1,), (1,)), ((), ())), preferred_element_type=F32)


def _dot_tn(a, b):
    return lax.dot_general(a, b, (((0,), (0,)), ((), ())), preferred_element_type=F32)


def _resident(shape):
    nd = len(shape)
    return pl.BlockSpec(shape, lambda *_: (0,) * nd, pipeline_mode=pl.Buffered(1))


def _params(*sem):
    return pltpu.CompilerParams(dimension_semantics=sem, vmem_limit_bytes=VMEM_LIMIT)


def _ada_kernel(c_ref, w_ref, b_ref, o_ref):
    s = _silu(c_ref[...])
    o_ref[...] = jnp.dot(s, w_ref[...], precision=lax.Precision.HIGHEST,
                         preferred_element_type=F32) + b_ref[...]


def _ada(cs, w_ada, b_ada):
    depth, d, d6 = w_ada.shape
    bn = 1024
    return pl.pallas_call(
        _ada_kernel,
        grid=(depth, d6 // bn),
        in_specs=[
            pl.BlockSpec((MOD_ROWS, d), lambda l, j: (0, 0)),
            pl.BlockSpec((None, d, bn), lambda l, j: (l, 0, j)),
            pl.BlockSpec((None, 1, bn), lambda l, j: (l, 0, j)),
        ],
        out_specs=pl.BlockSpec((None, MOD_ROWS, bn), lambda l, j: (l, 0, j)),
        out_shape=jax.ShapeDtypeStruct((depth, MOD_ROWS, d6), F32),
        compiler_params=_params("parallel", "parallel"),
        name="ada",
    )(cs, w_ada, b_ada.reshape(depth, 1, d6))


def _rope(x, cos, sa, sb):
    return x * cos + pltpu.roll(x, 16, 1) * sa + pltpu.roll(x, LANES - 16, 1) * sb


def _forget_terms(z, lb):
    u = jnp.exp(-jnp.abs(z))
    one_u = 1.0 + u
    log_sig = jnp.minimum(z, 0.0) - jnp.log(one_u)
    la = jnp.maximum(jnp.log(lb), NEG)
    b = jnp.log1p(-lb) + log_sig
    logf = jnp.maximum(la, b) + jnp.log(1.0 + jnp.exp(-jnp.abs(la - b)))
    r = 1.0 / one_u
    k = (1.0 - lb) * jnp.where(z >= 0.0, u * r, r)
    return logf, k


def _in_proj_kernel(x_ref, mod_ref, g_ref, w_ref, lb_ref, vg_ref, vb_ref, cos_ref, sa_ref, sb_ref,
                    qa_ref, ka_ref, va_ref, hq_ref, hkf_ref, hkb_ref, hlf_ref, hlb_ref, hv_ref,
                    hg_ref, mu_ref, mvn_ref, gate_ref, *, rope, d_br):
    h = _rms_mod(x_ref[...], g_ref[...], mod_ref[0, 1:2, :], mod_ref[0, 0:1, :]).astype(BF16)
    cw = MXU_COLS

    def proj(lo):
        return _dot(h, w_ref[:, lo:lo + cw])

    def maybe_rope(t):
        if not rope:
            return t
        return _rope(t, cos_ref[...], sa_ref[...], sb_ref[...])

    def each(o_ref, lo, fn):
        width = o_ref.shape[1]
        for c in range(0, width, cw):
            o_ref[:, c:c + cw] = fn(proj(lo + c)).astype(o_ref.dtype)
        return lo + width

    def rope_lanes(acc):
        parts = [maybe_rope(acc[:, s:s + LANES]) for s in range(0, cw, LANES)]
        return jnp.concatenate(parts, axis=1) * (HEAD_DIM ** -0.5 * LOG2E)

    off = each(qa_ref, 0, rope_lanes)
    acc = proj(off)
    ka_ref[...] = maybe_rope(acc[:, :LANES]).astype(BF16)
    va_ref[...] = acc[:, LANES:].astype(BF16)
    off += cw
    off = each(hq_ref, off, lambda a: _silu(a) * HG_DK ** -0.5)
    off_f, off_b, off_v, off_g, off_u = (off + n * d_br for n in range(5))
    plain = iter([(r, o + c, c) for r, o in ((hv_ref, off_v), (mu_ref, off_u)) for c in range(0, d_br, cw)])
    for which, (l_ref, k_ref, lo) in enumerate(((hlf_ref, hkf_ref, off_f), (hlb_ref, hkb_ref, off_b))):
        for c in range(0, d_br, cw):
            logf, k = _forget_terms(proj(lo + c), lb_ref[which:which + 1, c:c + cw])
            l_ref[:, c:c + cw] = logf * LOG2E
            k_ref[:, c:c + cw] = k.astype(BF16)
            p_ref, p_lo, p_c = next(plain)
            p_ref[:, p_c:p_c + cw] = proj(p_lo).astype(BF16)
    each(hg_ref, off_g, lambda a: _silu(a))
    off = off_u + d_br
    v = jnp.concatenate([proj(off + c) for c in range(0, d_br, cw)], axis=1)
    mean = jnp.mean(v, axis=-1, keepdims=True)
    var = jnp.mean(jnp.square(v - mean), axis=-1, keepdims=True)
    mvn_ref[...] = ((v - mean) * lax.rsqrt(var + EPS) * vg_ref[...] + vb_ref[...]).astype(BF16)
    off += d_br
    each(gate_ref, off, lambda a: _sigmoid(a))


def _in_proj(x2, mod, norm_g, w_in, lb, v_g, v_b, tables, *, seq, tm, rope):
    rows, d = x2.shape
    d_in = w_in.shape[1]
    d_br = d // 2
    per = seq // tm
    row = lambda w: pl.BlockSpec((tm, w), lambda i: (i, 0))
    tab = pl.BlockSpec((tm, LANES), lambda i: (i % per, 0))
    widths = dict(qa=d_br, ka=LANES, va=LANES, hq=d_br, hkf=d_br, hkb=d_br, hlf=d_br, hlb=d_br,
                  hv=d_br, hg=d_br, mu=d_br, mvn=d_br, gate=N_BRANCH * d)
    dtypes = {k: (F32 if k in ("hlf", "hlb") else BF16) for k in widths}
    outs = pl.pallas_call(
        functools.partial(_in_proj_kernel, rope=rope, d_br=d_br),
        grid=(rows // tm,),
        in_specs=[
            row(d),
            pl.BlockSpec((1, 6, d), lambda i: (i // per, 0, 0)),
            _resident((1, d)),
            _resident((d, d_in)),
            _resident((2, d_br)),
            _resident((1, d_br)),
            _resident((1, d_br)),
            tab, tab, tab,
        ],
        out_specs=[row(w) for w in widths.values()],
        out_shape=[jax.ShapeDtypeStruct((rows, w), dtypes[k]) for k, w in widths.items()],
        compiler_params=_params("parallel"),
        name="in_proj_rope" if rope else "in_proj_ctx",
    )(x2, mod, norm_g, w_in, lb, v_g, v_b, *tables)
    return dict(zip(widths, outs))


def _attn_kernel(sink_ref, q_ref, *refs, band):
    if band:
        kl_ref, km_ref, kr_ref, vl_ref, vm_ref, vr_ref, kc_ref, vc_ref, o_ref = refs
        j = pl.program_id(1)
        nq = pl.num_programs(1)
        keys = jnp.concatenate([kl_ref[...], km_ref[...], kr_ref[...], kc_ref[...]], axis=0)
        vals = jnp.concatenate([vl_ref[...], vm_ref[...], vr_ref[...], vc_ref[...]], axis=0)
        qi = lax.broadcasted_iota(jnp.int32, (ATT_BLOCK, ATT_BLOCK), 0)
        kj = lax.broadcasted_iota(jnp.int32, (ATT_BLOCK, ATT_BLOCK), 1)
        left = jnp.where(jnp.logical_and(kj >= qi, j > 0), 0.0, NEG)
        right = jnp.where(jnp.logical_and(kj <= qi, j < nq - 1), 0.0, NEG)
    else:
        kc_ref, vc_ref, o_ref = refs
        keys = kc_ref[...]
        vals = vc_ref[...]
    tq = q_ref.shape[0]
    groups = q_ref.shape[1] // LANES
    rep = ATT_HEADS // ATT_KV_HEADS
    left_half = lax.broadcasted_iota(jnp.int32, (tq, LANES), 1) < HEAD_DIM
    top = lax.broadcasted_iota(jnp.int32, (2 * tq, 1), 0) < tq
    scores = []
    for g in range(groups):
        qc = q_ref[:, g * LANES:(g + 1) * LANES]
        zero = jnp.zeros_like(qc)
        q2 = jnp.concatenate([jnp.where(left_half, qc, zero), jnp.where(left_half, zero, qc)], axis=0)
        scores.append(_dot_nt(q2, keys))
    for g, s in enumerate(scores):
        if band:
            two = lambda t: jnp.concatenate([t, t], axis=0)
            s = jnp.concatenate([s[:, :ATT_BLOCK] + two(left), s[:, ATT_BLOCK:2 * ATT_BLOCK],
                                 s[:, 2 * ATT_BLOCK:3 * ATT_BLOCK] + two(right), s[:, 3 * ATT_BLOCK:]],
                                axis=1)
        sink = jnp.where(top, sink_ref[g], sink_ref[g + rep]) * LOG2E
        m = jnp.maximum(jnp.max(s, axis=-1, keepdims=True), sink)
        e = jnp.exp2(s - m)
        den = jnp.sum(e, axis=-1, keepdims=True) + jnp.exp2(sink - m)
        o = _dot(e.astype(BF16), vals) * (1.0 / den)
        o_ref[:, g * LANES:(g + 1) * LANES] = jnp.where(left_half, o[:tq], o[tq:]).astype(BF16)


def _attention(q, k, v, kc, vc, sink, *, batch, seq, ctx_len, band):
    d_q = q.shape[1]
    tq = ATT_BLOCK
    nq = seq // tq
    smem = pl.BlockSpec(memory_space=pltpu.SMEM)
    qspec = pl.BlockSpec((tq, d_q), lambda b, j: (b * nq + j, 0))
    cspec = pl.BlockSpec((ctx_len, LANES), lambda b, j: (b, 0))
    if band:
        kvspecs = [
            pl.BlockSpec((tq, LANES), lambda b, j: (b * nq + jnp.maximum(j - 1, 0), 0)),
            pl.BlockSpec((tq, LANES), lambda b, j: (b * nq + j, 0)),
            pl.BlockSpec((tq, LANES), lambda b, j: (b * nq + jnp.minimum(j + 1, nq - 1), 0)),
        ]
        in_specs = [smem, qspec] + kvspecs + kvspecs + [cspec, cspec]
        args = (sink, q, k, k, k, v, v, v, kc, vc)
    else:
        in_specs = [smem, qspec, cspec, cspec]
        args = (sink, q, kc, vc)
    return pl.pallas_call(
        functools.partial(_attn_kernel, band=band),
        grid=(batch, nq),
        in_specs=in_specs,
        out_specs=qspec,
        out_shape=jax.ShapeDtypeStruct(q.shape, BF16),
        compiler_params=_params("parallel", "parallel"),
        name="attn_band" if band else "attn_ctx",
    )(*args)


HG_LEVELS = tuple(HG_CHUNK >> i for i in range(1, HG_CHUNK.bit_length()))


def _hgrn_tables():
    c = HG_CHUNK
    t = np.arange(c)[:, None]
    s = np.arange(c)[None, :]
    mats, lvls = [], []
    for rev in (False, True):
        lvl = np.full((c, c), -1, np.int32)
        lvl[np.arange(c), np.arange(c)] = 0
        for idx, b in enumerate(HG_LEVELS):
            same = (t // (2 * b)) == (s // (2 * b))
            t_hi, s_hi = (t % (2 * b)) >= b, (s % (2 * b)) >= b
            lvl[same & ((~t_hi & s_hi) if rev else (t_hi & ~s_hi))] = idx + 1
        mat = ((s >= t) if rev else (s <= t)).astype(np.float32)
        mats.append(np.concatenate([mat, mat], axis=1))
        lvls.append(lvl)
    return np.stack(mats), np.stack(lvls)


def _boundary_exponent(g, g_ref, lg, lanes, b, rev):
    c = g.shape[0]
    row = lax.broadcasted_iota(jnp.int32, (c, 1), 0)
    bound = lambda g0: g0 + b - (0 if rev else 1)
    if b == 1:
        return jnp.where(((row & 1) != 0) != rev, lg, 0.0)
    if b >= 8:
        pieces = []
        for g0 in range(0, c, 2 * b):
            gr = g_ref[bound(g0):bound(g0) + 1, lanes]
            first, second = g[g0:g0 + b], g[g0 + b:g0 + 2 * b]
            pieces += [first - gr, gr - second] if rev else [gr - first, second - gr]
        return jnp.concatenate(pieces, axis=0)
    sub = 8
    pieces = []
    for r0 in range(0, c, sub):
        refs = [jnp.broadcast_to(g_ref[bound(g0):bound(g0) + 1, lanes], (sub, LANES))
                for g0 in range(r0, r0 + sub, 2 * b)]
        gr = refs[0]
        for n in range(1, len(refs)):
            gr = jnp.where((row[:sub] & (sub - 1)) >= n * 2 * b, refs[n], gr)
        pieces.append(gr)
    return -jnp.abs(g - jnp.concatenate(pieces, axis=0))


def _mixed_rows(q, k, b, rev):
    c = q.shape[0]
    if b >= 8:
        pieces = []
        for g0 in range(0, c, 2 * b):
            lo, hi = slice(g0, g0 + b), slice(g0 + b, g0 + 2 * b)
            pieces += [q[lo], k[hi]] if rev else [k[lo], q[hi]]
        return jnp.concatenate(pieces, axis=0)
    later = (lax.broadcasted_iota(jnp.int32, (c, 1), 0) & b) != 0
    return jnp.where(later, k, q) if rev else jnp.where(later, q, k)


def _hgrn_scores(q, k, lg, g_ref, lanes, lvl, rev):
    c = HG_CHUNK
    g = g_ref[:, lanes]
    qf, kf = q.astype(F32), k.astype(F32)
    blocks = [(q, k)]
    for b in HG_LEVELS:
        e = jnp.exp2(_boundary_exponent(g, g_ref, lg, lanes, b, rev))
        y = (_mixed_rows(qf, kf, b, rev) * e).astype(BF16)
        blocks.append((y, y))
    zero = jnp.zeros_like(k)
    att = jnp.zeros((c, c), F32)
    for i in range(0, len(blocks), 2):
        (la, ra), (lb, rb) = blocks[i], blocks[i + 1]
        rhs = jnp.concatenate([jnp.concatenate([ra, zero], axis=1),
                               jnp.concatenate([zero, rb], axis=1)], axis=0)
        s = _dot_nt(jnp.concatenate([la, lb], axis=1), rhs)
        att = jnp.where(lvl == i, s[:, :c], jnp.where(lvl == i + 1, s[:, c:], att))
    return att.astype(BF16)


def _hgrn_apply(q, k, v, att, g_ref, lanes, state, rev):
    g = g_ref[:, lanes]
    end = 0 if rev else HG_CHUNK - 1
    g_end = g_ref[end:end + 1, lanes]
    lhs = jnp.concatenate([(q.astype(F32) * jnp.exp2(g)).astype(BF16), att], axis=1)
    rhs = jnp.concatenate([state.astype(BF16), v], axis=0)
    out = _dot(lhs, rhs)
    dec_t = jnp.broadcast_to(jnp.exp2(g_end), (HG_DK, LANES)).T
    new_state = dec_t * state + _dot_tn((k.astype(F32) * jnp.exp2(g_end - g)).astype(BF16), v)
    return out, new_state


def _hgrn_kernel(qf_ref, kf_ref, lf_ref, vf_ref, qb_ref, kb_ref, lb_ref, vb_ref, s0_ref, m_ref, lvl_ref,
                 of_ref, ob_ref, sfin_ref, state_ref, g_ref):
    i = pl.program_id(1)

    @pl.when(i == 0)
    def _():
        state_ref[...] = s0_ref[...]

    dirs = ((qf_ref, kf_ref, lf_ref, vf_ref, of_ref), (qb_ref, kb_ref, lb_ref, vb_ref, ob_ref))
    work = []
    for d, (q_ref, k_ref, l_ref, v_ref, o_ref) in enumerate(dirs):
        lg = l_ref[...]
        hi = lg.astype(BF16)
        lo = (lg - hi.astype(F32)).astype(BF16)
        g_ref[d] = _dot(m_ref[d], jnp.concatenate([hi, lo], axis=0))
        for h in range(HG_HEADS):
            lanes = slice(h * LANES, (h + 1) * LANES)
            work.append((d, h, lanes, o_ref, q_ref[:, lanes], k_ref[:, lanes], v_ref[:, lanes],
                         lg[:, lanes], state_ref[d, h]))
    atts = [_hgrn_scores(q, k, lg, g_ref.at[d], lanes, lvl_ref[d], bool(d))
            for d, h, lanes, o_ref, q, k, v, lg, state in work]
    done = []
    for att, (d, h, lanes, o_ref, q, k, v, lg, state) in zip(atts, work):
        out, new_state = _hgrn_apply(q, k, v, att, g_ref.at[d], lanes, state, bool(d))
        done.append((d, h, lanes, o_ref, out, new_state))
    for d, h, lanes, o_ref, out, new_state in done:
        state_ref[d, h] = new_state
        o_ref[:, lanes] = out.astype(o_ref.dtype)

    @pl.when(i == pl.num_programs(1) - 1)
    def _():
        sfin_ref[...] = state_ref[...]


def _hgrn(p, s0, *, batch, seq):
    c = HG_CHUNK
    n = seq // c
    width = p["hq"].shape[1]
    mats, lvls = _hgrn_tables()
    fwd = pl.BlockSpec((c, width), lambda b, i: (b * n + i, 0))
    bwd = pl.BlockSpec((c, width), lambda b, i: (b * n + (n - 1 - i), 0))
    st = pl.BlockSpec((None, 2, HG_HEADS, HG_DK, LANES), lambda b, i: (b, 0, 0, 0, 0))
    rows = batch * seq
    return pl.pallas_call(
        _hgrn_kernel,
        grid=(batch, n),
        in_specs=[fwd, fwd, fwd, fwd, bwd, bwd, bwd, bwd, st, _resident(mats.shape), _resident(lvls.shape)],
        out_specs=[fwd, bwd, st],
        out_shape=[jax.ShapeDtypeStruct((rows, width), BF16),
                   jax.ShapeDtypeStruct((rows, width), BF16),
                   jax.ShapeDtypeStruct(s0.shape, F32)],
        scratch_shapes=[pltpu.VMEM((2, HG_HEADS, HG_DK, LANES), F32),
                        pltpu.VMEM((2, mats.shape[1], width), F32)],
        compiler_params=_params("parallel", "arbitrary"),
        name="hgrn_scan",
    )(p["hq"], p["hkf"], p["hlf"], p["hv"], p["hq"], p["hkb"], p["hlb"], p["hv"], s0,
      jnp.asarray(mats, BF16), jnp.asarray(lvls))


def _merge_kernel(x_ref, mod_ref, a_ref, of_ref, ob_ref, hg_ref, mu_ref, mvn_ref, gate_ref,
                  hgn_ref, ws_ref, bs_ref, wbr_ref, wo_ref, o_ref):
    tm, d = x_ref.shape
    o = of_ref[...].astype(F32) + ob_ref[...].astype(F32)
    parts = []
    for h in range(HG_HEADS):
        oh = o[:, h * LANES:(h + 1) * LANES]
        parts.append(oh * lax.rsqrt(jnp.mean(oh * oh, axis=-1, keepdims=True) + EPS))
    b_br = (jnp.concatenate(parts, axis=1) * hgn_ref[...] * hg_ref[...].astype(F32)).astype(BF16)
    rows = []
    for r in range(tm // MLP_CHUNK):
        rs = slice(r * MLP_CHUNK, (r + 1) * MLP_CHUNK)
        cols = []
        for grp in range(MLP_GROUPS):
            cs = slice(grp * LANES, (grp + 1) * LANES)
            cols.append(_dot(ws_ref[grp], mvn_ref[rs, cs]) + bs_ref[grp])
        rows.append(jnp.concatenate(cols, axis=1))
    c_br = (mu_ref[...].astype(F32) * jnp.concatenate(rows, axis=0)).astype(BF16)
    m = jnp.zeros((tm, d), F32)
    for n, br in enumerate((a_ref[...], b_br, c_br)):
        m = m + gate_ref[:, n * d:(n + 1) * d].astype(F32) * _dot(br, wbr_ref[n])
    y = _dot(m.astype(BF16), wo_ref[...])
    o_ref[...] = x_ref[...] + mod_ref[0, 2:3, :] * y


def _merge(x2, mod, a, of, ob, p, hg_g, ws, bs, w_br, w_o, *, seq, tm):
    rows, d = x2.shape
    d_br = d // 2
    per = seq // tm
    row = lambda w: pl.BlockSpec((tm, w), lambda i: (i, 0))
    return pl.pallas_call(
        _merge_kernel,
        grid=(rows // tm,),
        in_specs=[
            row(d),
            pl.BlockSpec((1, 6, d), lambda i: (i // per, 0, 0)),
            row(d_br), row(d_br), row(d_br), row(d_br), row(d_br), row(d_br), row(N_BRANCH * d),
            _resident((1, d_br)),
            _resident(ws.shape),
            _resident(bs.shape),
            _resident(w_br.shape),
            _resident(w_o.shape),
        ],
        out_specs=row(d),
        out_shape=jax.ShapeDtypeStruct((rows, d), F32),
        compiler_params=_params("parallel"),
        name="merge",
    )(x2, mod, a, of, ob, p["hg"], p["mu"], p["mvn"], p["gate"], hg_g, ws, bs, w_br, w_o)


def _ffn_kernel(x_ref, mod_ref, g_ref, wg_ref, wu_ref, wd_ref, fg_ref, o_ref, act_ref, *, final, bf):
    x = x_ref[...]
    h = _rms_mod(x, g_ref[...], mod_ref[0, 4:5, :], mod_ref[0, 3:4, :]).astype(BF16)
    for c in range(wg_ref.shape[1] // bf):
        sl = slice(c * bf, (c + 1) * bf)
        act_ref[:, sl] = (_silu(_dot(h, wg_ref[:, sl])) * _dot(h, wu_ref[:, sl])).astype(BF16)
    y = x + mod_ref[0, 5:6, :] * _dot(act_ref[...], wd_ref[...])
    if final:
        y = y * lax.rsqrt(jnp.mean(y * y, axis=-1, keepdims=True) + EPS) * fg_ref[...]
    o_ref[...] = y


def _ffn(x2, mod, norm_g, w_gate, w_up, w_down, final_g, *, seq, tm, final):
    rows, d = x2.shape
    d_ff = w_gate.shape[1]
    per = seq // tm
    row = pl.BlockSpec((tm, d), lambda i: (i, 0))
    return pl.pallas_call(
        functools.partial(_ffn_kernel, final=final, bf=256),
        grid=(rows // tm,),
        in_specs=[
            row,
            pl.BlockSpec((1, 6, d), lambda i: (i // per, 0, 0)),
            _resident((1, d)),
            _resident(w_gate.shape),
            _resident(w_up.shape),
            _resident(w_down.shape),
            _resident((1, d)),
        ],
        out_specs=row,
        out_shape=jax.ShapeDtypeStruct((rows, d), F32),
        scratch_shapes=[pltpu.VMEM((tm, d_ff), BF16)],
        compiler_params=_params("parallel"),
        name="ffn_final" if final else "ffn",
    )(x2, mod, norm_g, w_gate, w_up, w_down, final_g)


def _rope_tables(seq):
    pos = jnp.arange(seq)
    nf = HEAD_DIM // 4
    inv = ROPE_BASE ** (-jnp.arange(nf, dtype=F32) / nf)
    ang_r = (pos // GRID_W).astype(F32)[:, None] * inv[None, :]
    ang_c = (pos % GRID_W).astype(F32)[:, None] * inv[None, :]
    zero = jnp.zeros_like(ang_r)
    cos = jnp.concatenate([jnp.cos(ang_r)] * 2 + [jnp.cos(ang_c)] * 2, axis=1)
    sa = jnp.concatenate([zero, jnp.sin(ang_r), zero, jnp.sin(ang_c)], axis=1)
    sb = jnp.concatenate([-jnp.sin(ang_r), zero, -jnp.sin(ang_c), zero], axis=1)
    reps = LANES // HEAD_DIM
    return tuple(jnp.tile(t, (1, reps)) for t in (cos, sa, sb))


def kernel(x, c, ctx, c_ctx, w_ada, b_ada, norm1_g, norm2_g, w_in, attn_sink, hg_lb_logits, hg_norm_g, mlp_v_norm_g, mlp_v_norm_b, mlp_ws, mlp_bs, w_branch, w_out, w_ffn_gate, w_ffn_up, w_ffn_down, final_norm_g):
    batch, seq, d = x.shape
    ctx_len = ctx.shape[1]
    depth = w_ada.shape[0]
    d_br = d // 2
    assert batch + 1 <= MOD_ROWS and seq % 512 == 0 and ctx_len % 256 == 0

    lb = jnp.cumsum(jax.nn.softmax(hg_lb_logits.astype(F32), axis=0), axis=0)
    lb = lb - lb[0]
    cs = jnp.concatenate([c, c_ctx[None, :], jnp.zeros((MOD_ROWS - batch - 1, d), F32)], axis=0)
    mod = _ada(cs, w_ada, b_ada).reshape(depth, MOD_ROWS, 6, d)
    tables = _rope_tables(seq)
    q_cols = jnp.concatenate([jnp.arange(h * HEAD_DIM, (h + 1) * HEAD_DIM) for h in Q_HEAD_ORDER])
    zero_state = jnp.zeros((batch, 2, HG_HEADS, HG_DK, LANES), F32)

    xs = x.reshape(batch * seq, d)
    cx = ctx.reshape(batch * ctx_len, d)
    for l in range(depth):
        need_ctx = l < depth - 1
        mod_x = mod[l, :batch]
        mod_c = jnp.broadcast_to(mod[l, batch:batch + 1], (batch, 6, d))
        w_in_l = jnp.concatenate([w_in[l][:, q_cols], w_in[l][:, d_br:]], axis=1).astype(BF16)
        w_br_l = jnp.concatenate([w_branch[l, :1][:, q_cols], w_branch[l, 1:]], axis=0).astype(BF16)
        ws_l = mlp_ws[l].astype(BF16)
        bs_l = jnp.broadcast_to(mlp_bs[l][:, :, None], (MLP_GROUPS, MLP_CHUNK, LANES))
        n1 = norm1_g[l][None, :]
        n2 = norm2_g[l][None, :]
        vg = mlp_v_norm_g[l][None, :]
        vb = mlp_v_norm_b[l][None, :]
        hgn = hg_norm_g[l][None, :]
        w_o_l = w_out[l].astype(BF16)
        wg_l, wu_l, wd_l = (w[l].astype(BF16) for w in (w_ffn_gate, w_ffn_up, w_ffn_down))
        fg = final_norm_g[None, :]

        pc = _in_proj(cx, mod_c, n1, w_in_l, lb[l], vg, vb, tables, seq=ctx_len, tm=256, rope=False)
        px = _in_proj(xs, mod_x, n1, w_in_l, lb[l], vg, vb, tables, seq=seq, tm=512, rope=True)
        a = _attention(px["qa"], px["ka"], px["va"], pc["ka"], pc["va"], attn_sink[l],
                       batch=batch, seq=seq, ctx_len=ctx_len, band=True)
        ocf, ocb, s_ctx = _hgrn(pc, zero_state, batch=batch, seq=ctx_len)
        of, ob, _ = _hgrn(px, s_ctx, batch=batch, seq=seq)
        xs = _merge(xs, mod_x, a, of, ob, px, hgn, ws_l, bs_l, w_br_l, w_o_l, seq=seq, tm=256)
        xs = _ffn(xs, mod_x, n2, wg_l, wu_l, wd_l, fg, seq=seq, tm=512, final=not need_ctx)
        if need_ctx:
            ac = _attention(pc["qa"], None, None, pc["ka"], pc["va"], attn_sink[l],
                            batch=batch, seq=ctx_len, ctx_len=ctx_len, band=False)
            cx = _merge(cx, mod_c, ac, ocf, ocb, pc, hgn, ws_l, bs_l, w_br_l, w_o_l,
                        seq=ctx_len, tm=256)
            cx = _ffn(cx, mod_c, n2, wg_l, wu_l, wd_l, fg, seq=ctx_len, tm=256, final=False)
    return xs.reshape(batch, seq, d)
```

```python
import functools

import jax
import jax.numpy as jnp
import numpy as np
from jax import lax
from jax.experimental import pallas as pl
from jax.experimental.pallas import tpu as pltpu

F32 = jnp.float32
BF16 = jnp.bfloat16

EPS = 1e-6
GRID_W = 64
HEAD_DIM = 64
ATT_HEADS = 8
ATT_KV_HEADS = 2
WINDOW = 128
ATT_BLOCK = 128
ROPE_BASE = 10000.0
HG_DK = 128
HG_HEADS = 4
HG_CHUNK = 128
MLP_CHUNK = 128
MLP_GROUPS = 4
N_BRANCH = 3

LANES = 128
MXU_COLS = 256
MOD_ROWS = 24
VMEM_LIMIT = 56 * 1024 * 1024
NEG = -1e30
LOG2E = 1.4426950408889634

Q_HEAD_ORDER = (0, 4, 1, 5, 2, 6, 3, 7)


def _sigmoid(x):
    return 0.5 * jnp.tanh(0.5 * x) + 0.5


def _silu(x):
    return x * _sigmoid(x)


def _rms_mod(x, g, sc, sh):
    y = x * lax.rsqrt(jnp.mean(x * x, axis=-1, keepdims=True) + EPS)
    return (y * g) * (1.0 + sc) + sh


def _dot(a, b):
    return jnp.dot(a, b, preferred_element_type=F32)


def _dot_nt(a, b):
    return lax.dot_general(a, b, (((1,), (1,)), ((), ())), preferred_element_type=F32)


def _dot_tn(a, b):
    return lax.dot_general(a, b, (((0,), (0,)), ((), ())), preferred_element_type=F32)


def _resident(shape, layer=None):
    if layer is None:
        return pl.BlockSpec(shape, lambda *_: (0,) * len(shape), pipeline_mode=pl.Buffered(1))
    rest = tuple(shape[1:])
    return pl.BlockSpec((None,) + rest, lambda *_: (layer,) + (0,) * len(rest),
                        pipeline_mode=pl.Buffered(1))


def _params(*sem):
    return pltpu.CompilerParams(dimension_semantics=sem, vmem_limit_bytes=VMEM_LIMIT)


def _ada_kernel(c_ref, w_ref, b_ref, o_ref):
    s = _silu(c_ref[...])
    o_ref[...] = jnp.dot(s, w_ref[...], precision=lax.Precision.HIGHEST,
                         preferred_element_type=F32) + b_ref[...]


def _ada(cs, w_ada, b_ada):
    depth, d, d6 = w_ada.shape
    bn = 1024
    return pl.pallas_call(
        _ada_kernel,
        grid=(depth, d6 // bn),
        in_specs=[
            pl.BlockSpec((MOD_ROWS, d), lambda l, j: (0, 0)),
            pl.BlockSpec((None, d, bn), lambda l, j: (l, 0, j)),
            pl.BlockSpec((None, 1, bn), lambda l, j: (l, 0, j)),
        ],
        out_specs=pl.BlockSpec((None, MOD_ROWS, bn), lambda l, j: (l, 0, j)),
        out_shape=jax.ShapeDtypeStruct((depth, MOD_ROWS, d6), F32),
        compiler_params=_params("parallel", "parallel"),
        name="ada",
    )(cs, w_ada, b_ada.reshape(depth, 1, d6))


def _rope(x, cos, sa, sb):
    return x * cos + pltpu.roll(x, 16, 1) * sa + pltpu.roll(x, LANES - 16, 1) * sb


def _forget_terms(z, lb):
    u = jnp.exp(-jnp.abs(z))
    one_u = 1.0 + u
    log_sig = jnp.minimum(z, 0.0) - jnp.log(one_u)
    la = jnp.maximum(jnp.log(lb), NEG)
    b = jnp.log1p(-lb) + log_sig
    logf = jnp.maximum(la, b) + jnp.log(1.0 + jnp.exp(-jnp.abs(la - b)))
    r = 1.0 / one_u
    k = (1.0 - lb) * jnp.where(z >= 0.0, u * r, r)
    return logf, k


def _in_proj_kernel(x_ref, mod_ref, g_ref, w_ref, lb_ref, vg_ref, vb_ref, cos_ref, sa_ref, sb_ref,
                    qa_ref, ka_ref, va_ref, hq_ref, hkf_ref, hkb_ref, hlf_ref, hlb_ref, hv_ref,
                    hg_ref, mu_ref, mvn_ref, *, rope, d_br):
    h = _rms_mod(x_ref[...], g_ref[...], mod_ref[0, 1:2, :], mod_ref[0, 0:1, :]).astype(BF16)
    cw = MXU_COLS

    def proj(lo):
        return _dot(h, w_ref[:, lo:lo + cw])

    def maybe_rope(t):
        if not rope:
            return t
        return _rope(t, cos_ref[...], sa_ref[...], sb_ref[...])

    def each(o_ref, lo, fn):
        width = o_ref.shape[1]
        for c in range(0, width, cw):
            o_ref[:, c:c + cw] = fn(proj(lo + c)).astype(o_ref.dtype)
        return lo + width

    def rope_lanes(acc):
        parts = [maybe_rope(acc[:, s:s + LANES]) for s in range(0, cw, LANES)]
        return jnp.concatenate(parts, axis=1) * (HEAD_DIM ** -0.5 * LOG2E)

    off = each(qa_ref, 0, rope_lanes)
    acc = proj(off)
    ka_ref[...] = maybe_rope(acc[:, :LANES]).astype(BF16)
    va_ref[...] = acc[:, LANES:].astype(BF16)
    off += cw
    off = each(hq_ref, off, lambda a: _silu(a) * HG_DK ** -0.5)
    off_f, off_b, off_v, off_g, off_u = (off + n * d_br for n in range(5))
    plain = iter([(r, o + c, c) for r, o in ((hv_ref, off_v), (mu_ref, off_u)) for c in range(0, d_br, cw)])
    for which, (l_ref, k_ref, lo) in enumerate(((hlf_ref, hkf_ref, off_f), (hlb_ref, hkb_ref, off_b))):
        for c in range(0, d_br, cw):
            logf, k = _forget_terms(proj(lo + c), lb_ref[which:which + 1, c:c + cw])
            l_ref[:, c:c + cw] = logf * LOG2E
            k_ref[:, c:c + cw] = k.astype(BF16)
            p_ref, p_lo, p_c = next(plain)
            p_ref[:, p_c:p_c + cw] = proj(p_lo).astype(BF16)
    each(hg_ref, off_g, lambda a: _silu(a))
    off = off_u + d_br
    v = jnp.concatenate([proj(off + c) for c in range(0, d_br, cw)], axis=1)
    mean = jnp.mean(v, axis=-1, keepdims=True)
    var = jnp.mean(jnp.square(v - mean), axis=-1, keepdims=True)
    mvn_ref[...] = ((v - mean) * lax.rsqrt(var + EPS) * vg_ref[...] + vb_ref[...]).astype(BF16)


def _in_proj(x2, mod, norm_g, w_in, lb, v_g, v_b, tables, *, layer, seq, tm, rope):
    rows, d = x2.shape
    d_br = d // 2
    per = seq // tm
    row = lambda w: pl.BlockSpec((tm, w), lambda i: (i, 0))
    tab = pl.BlockSpec((tm, LANES), lambda i: (i % per, 0))
    widths = dict(qa=d_br, ka=LANES, va=LANES, hq=d_br, hkf=d_br, hkb=d_br, hlf=d_br, hlb=d_br,
                  hv=d_br, hg=d_br, mu=d_br, mvn=d_br)
    assert w_in.shape[2] == sum(widths.values()) - 2 * d_br
    dtypes = {k: (F32 if k in ("hlf", "hlb") else BF16) for k in widths}
    outs = pl.pallas_call(
        functools.partial(_in_proj_kernel, rope=rope, d_br=d_br),
        grid=(rows // tm,),
        in_specs=[
            row(d),
            pl.BlockSpec((1, 6, d), lambda i: (i // per, 0, 0)),
            _resident(norm_g.shape, layer),
            _resident(w_in.shape, layer),
            _resident(lb.shape, layer),
            _resident(v_g.shape, layer),
            _resident(v_b.shape, layer),
            tab, tab, tab,
        ],
        out_specs=[row(w) for w in widths.values()],
        out_shape=[jax.ShapeDtypeStruct((rows, w), dtypes[k]) for k, w in widths.items()],
        compiler_params=_params("parallel"),
        name="in_proj_rope" if rope else "in_proj_ctx",
    )(x2, mod, norm_g, w_in, lb, v_g, v_b, *tables)
    return dict(zip(widths, outs))


def _attn_kernel(sink_ref, q_ref, *refs, band):
    if band:
        kl_ref, km_ref, kr_ref, vl_ref, vm_ref, vr_ref, kc_ref, vc_ref, o_ref = refs
        j = pl.program_id(1)
        nq = pl.num_programs(1)
        keys = jnp.concatenate([kl_ref[...], km_ref[...], kr_ref[...], kc_ref[...]], axis=0)
        vals = jnp.concatenate([vl_ref[...], vm_ref[...], vr_ref[...], vc_ref[...]], axis=0)
        qi = lax.broadcasted_iota(jnp.int32, (ATT_BLOCK, ATT_BLOCK), 0)
        kj = lax.broadcasted_iota(jnp.int32, (ATT_BLOCK, ATT_BLOCK), 1)
        left = jnp.where(jnp.logical_and(kj >= qi, j > 0), 0.0, NEG)
        right = jnp.where(jnp.logical_and(kj <= qi, j < nq - 1), 0.0, NEG)
    else:
        kc_ref, vc_ref, o_ref = refs
        keys = kc_ref[...]
        vals = vc_ref[...]
    tq = q_ref.shape[0]
    groups = q_ref.shape[1] // LANES
    rep = ATT_HEADS // ATT_KV_HEADS
    left_half = lax.broadcasted_iota(jnp.int32, (tq, LANES), 1) < HEAD_DIM
    top = lax.broadcasted_iota(jnp.int32, (2 * tq, 1), 0) < tq
    scores = []
    for g in range(groups):
        qc = q_ref[:, g * LANES:(g + 1) * LANES]
        zero = jnp.zeros_like(qc)
        q2 = jnp.concatenate([jnp.where(left_half, qc, zero), jnp.where(left_half, zero, qc)], axis=0)
        scores.append(_dot_nt(q2, keys))
    for g, s in enumerate(scores):
        if band:
            two = lambda t: jnp.concatenate([t, t], axis=0)
            s = jnp.concatenate([s[:, :ATT_BLOCK] + two(left), s[:, ATT_BLOCK:2 * ATT_BLOCK],
                                 s[:, 2 * ATT_BLOCK:3 * ATT_BLOCK] + two(right), s[:, 3 * ATT_BLOCK:]],
                                axis=1)
        sink = jnp.where(top, sink_ref[g], sink_ref[g + rep]) * LOG2E
        m = jnp.maximum(jnp.max(s, axis=-1, keepdims=True), sink)
        e = jnp.exp2(s - m)
        den = jnp.sum(e, axis=-1, keepdims=True) + jnp.exp2(sink - m)
        o = _dot(e.astype(BF16), vals) * (1.0 / den)
        o_ref[:, g * LANES:(g + 1) * LANES] = jnp.where(left_half, o[:tq], o[tq:]).astype(BF16)


def _attention(q, k, v, kc, vc, sink, *, batch, seq, ctx_len, band):
    d_q = q.shape[1]
    tq = ATT_BLOCK
    nq = seq // tq
    smem = pl.BlockSpec(memory_space=pltpu.SMEM)
    qspec = pl.BlockSpec((tq, d_q), lambda b, j: (b * nq + j, 0))
    cspec = pl.BlockSpec((ctx_len, LANES), lambda b, j: (b, 0))
    if band:
        kvspecs = [
            pl.BlockSpec((tq, LANES), lambda b, j: (b * nq + jnp.maximum(j - 1, 0), 0)),
            pl.BlockSpec((tq, LANES), lambda b, j: (b * nq + j, 0)),
            pl.BlockSpec((tq, LANES), lambda b, j: (b * nq + jnp.minimum(j + 1, nq - 1), 0)),
        ]
        in_specs = [smem, qspec] + kvspecs + kvspecs + [cspec, cspec]
        args = (sink, q, k, k, k, v, v, v, kc, vc)
    else:
        in_specs = [smem, qspec, cspec, cspec]
        args = (sink, q, kc, vc)
    return pl.pallas_call(
        functools.partial(_attn_kernel, band=band),
        grid=(batch, nq),
        in_specs=in_specs,
        out_specs=qspec,
        out_shape=jax.ShapeDtypeStruct(q.shape, BF16),
        compiler_params=_params("parallel", "parallel"),
        name="attn_band" if band else "attn_ctx",
    )(*args)


HG_LAG = 2 * HG_HEADS
HG_LEVELS = tuple(HG_CHUNK >> i for i in range(1, HG_CHUNK.bit_length()))


def _hgrn_tables():
    c = HG_CHUNK
    t = np.arange(c)[:, None]
    s = np.arange(c)[None, :]
    mats, lvls = [], []
    for rev in (False, True):
        lvl = np.full((c, c), -1, np.int32)
        lvl[np.arange(c), np.arange(c)] = 0
        for idx, b in enumerate(HG_LEVELS):
            same = (t // (2 * b)) == (s // (2 * b))
            t_hi, s_hi = (t % (2 * b)) >= b, (s % (2 * b)) >= b
            lvl[same & ((~t_hi & s_hi) if rev else (t_hi & ~s_hi))] = idx + 1
        mat = ((s >= t) if rev else (s <= t)).astype(np.float32)
        mats.append(np.concatenate([mat, mat], axis=1))
        lvls.append(lvl)
    return np.stack(mats), np.stack(lvls)


def _boundary_exponent(g, g_ref, lg, lanes, b, rev):
    c = g.shape[0]
    row = lax.broadcasted_iota(jnp.int32, (c, 1), 0)
    bound = lambda g0: g0 + b - (0 if rev else 1)
    if b == 1:
        return jnp.where(((row & 1) != 0) != rev, lg, 0.0)
    if b >= 8:
        pieces = []
        for g0 in range(0, c, 2 * b):
            gr = g_ref[bound(g0):bound(g0) + 1, lanes]
            first, second = g[g0:g0 + b], g[g0 + b:g0 + 2 * b]
            pieces += [first - gr, gr - second] if rev else [gr - first, second - gr]
        return jnp.concatenate(pieces, axis=0)
    sub = 8
    pieces = []
    for r0 in range(0, c, sub):
        refs = [jnp.broadcast_to(g_ref[bound(g0):bound(g0) + 1, lanes], (sub, LANES))
                for g0 in range(r0, r0 + sub, 2 * b)]
        gr = refs[0]
        for n in range(1, len(refs)):
            gr = jnp.where((row[:sub] & (sub - 1)) >= n * 2 * b, refs[n], gr)
        pieces.append(gr)
    return -jnp.abs(g - jnp.concatenate(pieces, axis=0))


def _mixed_rows(q, k, b, rev):
    c = q.shape[0]
    if b >= 8:
        pieces = []
        for g0 in range(0, c, 2 * b):
            lo, hi = slice(g0, g0 + b), slice(g0 + b, g0 + 2 * b)
            pieces += [q[lo], k[hi]] if rev else [k[lo], q[hi]]
        return jnp.concatenate(pieces, axis=0)
    later = (lax.broadcasted_iota(jnp.int32, (c, 1), 0) & b) != 0
    return jnp.where(later, k, q) if rev else jnp.where(later, q, k)


def _hgrn_scores(q, k, lg, g_ref, lanes, lvl, rev):
    c = HG_CHUNK
    g = g_ref[:, lanes]
    qf, kf = q.astype(F32), k.astype(F32)
    blocks = [(q, k)]
    for b in HG_LEVELS:
        e = jnp.exp2(_boundary_exponent(g, g_ref, lg, lanes, b, rev))
        y = (_mixed_rows(qf, kf, b, rev) * e).astype(BF16)
        blocks.append((y, y))
    zero = jnp.zeros_like(k)
    att = jnp.zeros((c, c), F32)
    for i in range(0, len(blocks), 2):
        (la, ra), (lb, rb) = blocks[i], blocks[i + 1]
        rhs = jnp.concatenate([jnp.concatenate([ra, zero], axis=1),
                               jnp.concatenate([zero, rb], axis=1)], axis=0)
        s = _dot_nt(jnp.concatenate([la, lb], axis=1), rhs)
        att = jnp.where(lvl == i, s[:, :c], jnp.where(lvl == i + 1, s[:, c:], att))
    return att.astype(BF16)


def _hgrn_apply(q, k, v, att, g_ref, lanes, state, rev):
    g = g_ref[:, lanes]
    end = 0 if rev else HG_CHUNK - 1
    g_end = g_ref[end:end + 1, lanes]
    lhs = jnp.concatenate([(q.astype(F32) * jnp.exp2(g)).astype(BF16), att], axis=1)
    rhs = jnp.concatenate([state.astype(BF16), v], axis=0)
    out = _dot(lhs, rhs)
    dec_t = jnp.broadcast_to(jnp.exp2(g_end), (HG_DK, LANES)).T
    new_state = dec_t * state + _dot_tn((k.astype(F32) * jnp.exp2(g_end - g)).astype(BF16), v)
    return out, new_state


def _hgrn_kernel(qf_ref, kf_ref, lf_ref, vf_ref, qb_ref, kb_ref, lb_ref, vb_ref, s0_ref, m_ref, lvl_ref,
                 of_ref, ob_ref, sfin_ref, state_ref, g_ref):
    i = pl.program_id(1)

    @pl.when(i == 0)
    def _():
        state_ref[...] = s0_ref[...]

    dirs = ((qf_ref, kf_ref, lf_ref, vf_ref, of_ref), (qb_ref, kb_ref, lb_ref, vb_ref, ob_ref))
    work = []
    for d, (q_ref, k_ref, l_ref, v_ref, o_ref) in enumerate(dirs):
        lg = l_ref[...]
        hi = lg.astype(BF16)
        lo = (lg - hi.astype(F32)).astype(BF16)
        g_ref[d] = _dot(m_ref[d], jnp.concatenate([hi, lo], axis=0))
        for h in range(HG_HEADS):
            lanes = slice(h * LANES, (h + 1) * LANES)
            work.append((d, h, lanes, o_ref, q_ref[:, lanes], k_ref[:, lanes], v_ref[:, lanes],
                         lg[:, lanes], state_ref[d, h]))
    atts, done = [], []
    for n in range(len(work) + HG_LAG):
        if n < len(work):
            d, h, lanes, o_ref, q, k, v, lg, state = work[n]
            atts.append(_hgrn_scores(q, k, lg, g_ref.at[d], lanes, lvl_ref[d], bool(d)))
        if n >= HG_LAG:
            d, h, lanes, o_ref, q, k, v, lg, state = work[n - HG_LAG]
            out, new_state = _hgrn_apply(q, k, v, atts[n - HG_LAG], g_ref.at[d], lanes, state, bool(d))
            done.append((d, h, lanes, o_ref, out, new_state))
    for d, h, lanes, o_ref, out, new_state in done:
        state_ref[d, h] = new_state
        o_ref[:, lanes] = out.astype(o_ref.dtype)

    @pl.when(i == pl.num_programs(1) - 1)
    def _():
        sfin_ref[...] = state_ref[...]


def _hgrn(p, s0, *, batch, seq):
    c = HG_CHUNK
    n = seq // c
    width = p["hq"].shape[1]
    mats, lvls = _hgrn_tables()
    fwd = pl.BlockSpec((c, width), lambda b, i: (b * n + i, 0))
    bwd = pl.BlockSpec((c, width), lambda b, i: (b * n + (n - 1 - i), 0))
    st = pl.BlockSpec((None, 2, HG_HEADS, HG_DK, LANES), lambda b, i: (b, 0, 0, 0, 0))
    rows = batch * seq
    return pl.pallas_call(
        _hgrn_kernel,
        grid=(batch, n),
        in_specs=[fwd, fwd, fwd, fwd, bwd, bwd, bwd, bwd, st, _resident(mats.shape), _resident(lvls.shape)],
        out_specs=[fwd, bwd, st],
        out_shape=[jax.ShapeDtypeStruct((rows, width), BF16),
                   jax.ShapeDtypeStruct((rows, width), BF16),
                   jax.ShapeDtypeStruct(s0.shape, F32)],
        scratch_shapes=[pltpu.VMEM((2, HG_HEADS, HG_DK, LANES), F32),
                        pltpu.VMEM((2, mats.shape[1], width), F32)],
        compiler_params=_params("parallel", "arbitrary"),
        name="hgrn_scan",
    )(p["hq"], p["hkf"], p["hlf"], p["hv"], p["hq"], p["hkb"], p["hlb"], p["hv"], s0,
      jnp.asarray(mats, BF16), jnp.asarray(lvls))


def _merge_kernel(x_ref, mod_ref, a_ref, of_ref, ob_ref, hg_ref, mu_ref, mvn_ref,
                  g_ref, wgt_ref, hgn_ref, ws_ref, bs_ref, wbr_ref, wo_ref, o_ref):
    tm, d = x_ref.shape
    cw = MXU_COLS
    h = _rms_mod(x_ref[...], g_ref[...], mod_ref[0, 1:2, :], mod_ref[0, 0:1, :]).astype(BF16)
    o = of_ref[...].astype(F32) + ob_ref[...].astype(F32)
    parts = []
    for head in range(HG_HEADS):
        oh = o[:, head * LANES:(head + 1) * LANES]
        parts.append(oh * lax.rsqrt(jnp.mean(oh * oh, axis=-1, keepdims=True) + EPS))
    b_br = (jnp.concatenate(parts, axis=1) * hgn_ref[...] * hg_ref[...].astype(F32)).astype(BF16)
    rows = []
    for r in range(tm // MLP_CHUNK):
        rs = slice(r * MLP_CHUNK, (r + 1) * MLP_CHUNK)
        cols = []
        for grp in range(MLP_GROUPS):
            cs = slice(grp * LANES, (grp + 1) * LANES)
            cols.append(_dot(ws_ref[grp], mvn_ref[rs, cs]) + bs_ref[grp])
        rows.append(jnp.concatenate(cols, axis=1))
    c_br = (mu_ref[...].astype(F32) * jnp.concatenate(rows, axis=0)).astype(BF16)
    branches = (a_ref[...], b_br, c_br)
    merged = []
    for c in range(0, d, cw):
        m = jnp.zeros((tm, cw), F32)
        for n, br in enumerate(branches):
            gate = _sigmoid(_dot(h, wgt_ref[:, n * d + c:n * d + c + cw]))
            m = m + gate * _dot(br, wbr_ref[n, :, c:c + cw])
        merged.append(m.astype(BF16))
    merged = jnp.concatenate(merged, axis=1)
    for c in range(0, d, cw):
        y = _dot(merged, wo_ref[:, c:c + cw])
        o_ref[:, c:c + cw] = x_ref[:, c:c + cw] + mod_ref[0, 2:3, c:c + cw] * y


def _merge(x2, mod, a, of, ob, p, norm_g, w_gates, hg_g, ws, bs, w_br, w_o, *, layer, seq, tm):
    rows, d = x2.shape
    d_br = d // 2
    per = seq // tm
    row = lambda w: pl.BlockSpec((tm, w), lambda i: (i, 0))
    weights = (norm_g, w_gates, hg_g, ws, bs, w_br, w_o)
    return pl.pallas_call(
        _merge_kernel,
        grid=(rows // tm,),
        in_specs=[
            row(d),
            pl.BlockSpec((1, 6, d), lambda i: (i // per, 0, 0)),
            row(d_br), row(d_br), row(d_br), row(d_br), row(d_br), row(d_br),
        ] + [_resident(w.shape, layer) for w in weights],
        out_specs=row(d),
        out_shape=jax.ShapeDtypeStruct((rows, d), F32),
        compiler_params=_params("parallel"),
        name="merge",
    )(x2, mod, a, of, ob, p["hg"], p["mu"], p["mvn"], *weights)


def _ffn_kernel(x_ref, mod_ref, g_ref, wg_ref, wu_ref, wd_ref, fg_ref, o_ref, act_ref, *, final, bf):
    x = x_ref[...]
    h = _rms_mod(x, g_ref[...], mod_ref[0, 4:5, :], mod_ref[0, 3:4, :]).astype(BF16)
    for c in range(wg_ref.shape[1] // bf):
        sl = slice(c * bf, (c + 1) * bf)
        act_ref[:, sl] = (_silu(_dot(h, wg_ref[:, sl])) * _dot(h, wu_ref[:, sl])).astype(BF16)
    y = x + mod_ref[0, 5:6, :] * _dot(act_ref[...], wd_ref[...])
    if final:
        y = y * lax.rsqrt(jnp.mean(y * y, axis=-1, keepdims=True) + EPS) * fg_ref[...]
    o_ref[...] = y


def _ffn(x2, mod, norm_g, w_gate, w_up, w_down, final_g, *, layer, seq, tm, final):
    rows, d = x2.shape
    d_ff = w_gate.shape[2]
    per = seq // tm
    row = pl.BlockSpec((tm, d), lambda i: (i, 0))
    return pl.pallas_call(
        functools.partial(_ffn_kernel, final=final, bf=MXU_COLS),
        grid=(rows // tm,),
        in_specs=[
            row,
            pl.BlockSpec((1, 6, d), lambda i: (i // per, 0, 0)),
            _resident(norm_g.shape, layer),
            _resident(w_gate.shape, layer),
            _resident(w_up.shape, layer),
            _resident(w_down.shape, layer),
            _resident((1, d)),
        ],
        out_specs=row,
        out_shape=jax.ShapeDtypeStruct((rows, d), F32),
        scratch_shapes=[pltpu.VMEM((tm, d_ff), BF16)],
        compiler_params=_params("parallel"),
        name="ffn_final" if final else "ffn",
    )(x2, mod, norm_g, w_gate, w_up, w_down, final_g)


def _rope_tables(seq):
    pos = jnp.arange(seq)
    nf = HEAD_DIM // 4
    inv = ROPE_BASE ** (-jnp.arange(nf, dtype=F32) / nf)
    ang_r = (pos // GRID_W).astype(F32)[:, None] * inv[None, :]
    ang_c = (pos % GRID_W).astype(F32)[:, None] * inv[None, :]
    zero = jnp.zeros_like(ang_r)
    cos = jnp.concatenate([jnp.cos(ang_r)] * 2 + [jnp.cos(ang_c)] * 2, axis=1)
    sa = jnp.concatenate([zero, jnp.sin(ang_r), zero, jnp.sin(ang_c)], axis=1)
    sb = jnp.concatenate([-jnp.sin(ang_r), zero, -jnp.sin(ang_c), zero], axis=1)
    reps = LANES // HEAD_DIM
    return tuple(jnp.tile(t, (1, reps)) for t in (cos, sa, sb))


def kernel(x, c, ctx, c_ctx, w_ada, b_ada, norm1_g, norm2_g, w_in, attn_sink, hg_lb_logits, hg_norm_g, mlp_v_norm_g, mlp_v_norm_b, mlp_ws, mlp_bs, w_branch, w_out, w_ffn_gate, w_ffn_up, w_ffn_down, final_norm_g):
    batch, seq, d = x.shape
    ctx_len = ctx.shape[1]
    depth = w_ada.shape[0]
    d_br = d // 2
    assert batch + 1 <= MOD_ROWS and seq % 512 == 0 and ctx_len % 256 == 0

    lb = jnp.cumsum(jax.nn.softmax(hg_lb_logits.astype(F32), axis=0), axis=0)
    lb = lb - lb[0]
    cs = jnp.concatenate([c, c_ctx[None, :], jnp.zeros((MOD_ROWS - batch - 1, d), F32)], axis=0)
    mod = _ada(cs, w_ada, b_ada).reshape(depth, MOD_ROWS, 6, d)
    tables = _rope_tables(seq)
    q_cols = jnp.concatenate([jnp.arange(h * HEAD_DIM, (h + 1) * HEAD_DIM) for h in Q_HEAD_ORDER])
    zero_state = jnp.zeros((batch, 2, HG_HEADS, HG_DK, LANES), F32)

    d_gate = N_BRANCH * d
    w_proj = jnp.concatenate([w_in[:, :, :d_br][:, :, q_cols], w_in[:, :, d_br:-d_gate]],
                             axis=2).astype(BF16)
    w_gates = w_in[:, :, -d_gate:].astype(BF16)
    w_br = jnp.concatenate([w_branch[:, :1][:, :, q_cols], w_branch[:, 1:]], axis=1).astype(BF16)
    ws = mlp_ws.astype(BF16)
    bs = jnp.broadcast_to(mlp_bs[:, :, :, None], mlp_bs.shape + (LANES,))
    w_o, wg, wu, wd = (w.astype(BF16) for w in (w_out, w_ffn_gate, w_ffn_up, w_ffn_down))
    n1, n2, vg, vb, hgn = (t[:, None, :] for t in (norm1_g, norm2_g, mlp_v_norm_g, mlp_v_norm_b, hg_norm_g))
    fg = final_norm_g[None, :]

    xs = x.reshape(batch * seq, d)
    cx = ctx.reshape(batch * ctx_len, d)
    for l in range(depth):
        need_ctx = l < depth - 1
        mod_x = mod[l, :batch]
        mod_c = jnp.broadcast_to(mod[l, batch:batch + 1], (batch, 6, d))
        pc = _in_proj(cx, mod_c, n1, w_proj, lb, vg, vb, tables, layer=l, seq=ctx_len, tm=256, rope=False)
        px = _in_proj(xs, mod_x, n1, w_proj, lb, vg, vb, tables, layer=l, seq=seq, tm=512, rope=True)
        a = _attention(px["qa"], px["ka"], px["va"], pc["ka"], pc["va"], attn_sink[l],
                       batch=batch, seq=seq, ctx_len=ctx_len, band=True)
        ocf, ocb, s_ctx = _hgrn(pc, zero_state, batch=batch, seq=ctx_len)
        of, ob, _ = _hgrn(px, s_ctx, batch=batch, seq=seq)
        xs = _merge(xs, mod_x, a, of, ob, px, n1, w_gates, hgn, ws, bs, w_br, w_o,
                    layer=l, seq=seq, tm=256)
        xs = _ffn(xs, mod_x, n2, wg, wu, wd, fg, layer=l, seq=seq, tm=512, final=not need_ctx)
        if need_ctx:
            ac = _attention(pc["qa"], None, None, pc["ka"], pc["va"], attn_sink[l],
                            batch=batch, seq=ctx_len, ctx_len=ctx_len, band=False)
            cx = _merge(cx, mod_c, ac, ocf, ocb, pc, n1, w_gates, hgn, ws, bs, w_br, w_o,
                        layer=l, seq=ctx_len, tm=256)
            cx = _ffn(cx, mod_c, n2, wg, wu, wd, fg, layer=l, seq=ctx_len, tm=256, final=False)
    return xs.reshape(batch, seq, d)
```

```python
import functools

import jax
import jax.numpy as jnp
import numpy as np
from jax import lax
from jax.experimental import pallas as pl
from jax.experimental.pallas import tpu as pltpu

F32 = jnp.float32
BF16 = jnp.bfloat16

EPS = 1e-6
GRID_W = 64
HEAD_DIM = 64
ATT_HEADS = 8
ATT_KV_HEADS = 2
WINDOW = 128
ATT_BLOCK = 128
ROPE_BASE = 10000.0
HG_DK = 128
HG_HEADS = 4
HG_CHUNK = 128
MLP_CHUNK = 128
MLP_GROUPS = 4
N_BRANCH = 3

LANES = 128
MXU_COLS = 256
IN_PROJ_ROWS = 128
FFN_ROWS = 128
MERGE_ROWS = 256
MOD_ROWS = 24
VMEM_LIMIT = 56 * 1024 * 1024
NEG = -1e30
LOG2E = 1.4426950408889634

Q_HEAD_ORDER = (0, 4, 1, 5, 2, 6, 3, 7)


def _sigmoid(x):
    return 0.5 * jnp.tanh(0.5 * x) + 0.5


def _silu(x):
    return x * _sigmoid(x)


def _rms_mod(x, g, sc, sh):
    y = x * lax.rsqrt(jnp.mean(x * x, axis=-1, keepdims=True) + EPS)
    return (y * g) * (1.0 + sc) + sh


def _dot(a, b):
    return jnp.dot(a, b, preferred_element_type=F32)


def _dot_nt(a, b):
    return lax.dot_general(a, b, (((1,), (1,)), ((), ())), preferred_element_type=F32)


def _dot_tn(a, b):
    return lax.dot_general(a, b, (((0,), (0,)), ((), ())), preferred_element_type=F32)


def _resident(shape, layer=None):
    if layer is None:
        return pl.BlockSpec(shape, lambda *_: (0,) * len(shape), pipeline_mode=pl.Buffered(1))
    rest = tuple(shape[1:])
    return pl.BlockSpec((None,) + rest, lambda *_: (layer,) + (0,) * len(rest),
                        pipeline_mode=pl.Buffered(1))


def _params(*sem):
    return pltpu.CompilerParams(dimension_semantics=sem, vmem_limit_bytes=VMEM_LIMIT)


def _ada_kernel(c_ref, w_ref, b_ref, o_ref):
    s = _silu(c_ref[...])
    o_ref[...] = jnp.dot(s, w_ref[...], precision=lax.Precision.HIGHEST,
                         preferred_element_type=F32) + b_ref[...]


def _ada(cs, w_ada, b_ada):
    depth, d, d6 = w_ada.shape
    bn = 1024
    return pl.pallas_call(
        _ada_kernel,
        grid=(depth, d6 // bn),
        in_specs=[
            pl.BlockSpec((MOD_ROWS, d), lambda l, j: (0, 0)),
            pl.BlockSpec((None, d, bn), lambda l, j: (l, 0, j)),
            pl.BlockSpec((None, 1, bn), lambda l, j: (l, 0, j)),
        ],
        out_specs=pl.BlockSpec((None, MOD_ROWS, bn), lambda l, j: (l, 0, j)),
        out_shape=jax.ShapeDtypeStruct((depth, MOD_ROWS, d6), F32),
        compiler_params=_params("parallel", "parallel"),
        name="ada",
    )(cs, w_ada, b_ada.reshape(depth, 1, d6))


def _rope(x, cos, sa, sb):
    return x * cos + pltpu.roll(x, 16, 1) * sa + pltpu.roll(x, LANES - 16, 1) * sb


def _forget_terms(z, lb):
    z2 = z * LOG2E
    u = jnp.exp2(-jnp.abs(z2))
    one_u = 1.0 + u
    log_sig = jnp.minimum(z2, 0.0) - jnp.log2(one_u)
    la = jnp.maximum(jnp.log2(lb), NEG)
    b = jnp.log1p(-lb) * LOG2E + log_sig
    logf2 = jnp.maximum(la, b) + jnp.log2(1.0 + jnp.exp2(-jnp.abs(la - b)))
    r = 1.0 / one_u
    k = (1.0 - lb) * jnp.where(z >= 0.0, u * r, r)
    return logf2, k


def _in_proj_kernel(x_ref, mod_ref, g_ref, w_ref, lb_ref, vg_ref, vb_ref, cos_ref, sa_ref, sb_ref,
                    qa_ref, ka_ref, va_ref, hq_ref, hkf_ref, hkb_ref, hlf_ref, hlb_ref, hv_ref,
                    hg_ref, mu_ref, mvn_ref, *, rope, d_br):
    cw = MXU_COLS
    for rows in (pl.ds(r, IN_PROJ_ROWS) for r in range(0, x_ref.shape[0], IN_PROJ_ROWS)):
        h = _rms_mod(x_ref[rows, :], g_ref[...], mod_ref[0, 1:2, :], mod_ref[0, 0:1, :]).astype(BF16)

        def proj(lo):
            return _dot(h, w_ref[:, lo:lo + cw])

        def maybe_rope(t):
            if not rope:
                return t
            return _rope(t, cos_ref[rows, :], sa_ref[rows, :], sb_ref[rows, :])

        def each(o_ref, lo, fn):
            width = o_ref.shape[1]
            for c in range(0, width, cw):
                o_ref[rows, c:c + cw] = fn(proj(lo + c)).astype(o_ref.dtype)
            return lo + width

        def rope_lanes(acc):
            parts = [maybe_rope(acc[:, s:s + LANES]) for s in range(0, cw, LANES)]
            return jnp.concatenate(parts, axis=1) * (HEAD_DIM ** -0.5 * LOG2E)

        off = each(qa_ref, 0, rope_lanes)
        acc = proj(off)
        ka_ref[rows, :] = maybe_rope(acc[:, :LANES]).astype(BF16)
        va_ref[rows, :] = acc[:, LANES:].astype(BF16)
        off += cw
        off = each(hq_ref, off, lambda a: _silu(a) * HG_DK ** -0.5)
        off_f, off_b, off_v, off_g, off_u = (off + n * d_br for n in range(5))
        plain = iter([(r, o + c, c) for r, o in ((hv_ref, off_v), (mu_ref, off_u))
                      for c in range(0, d_br, cw)])
        for which, (l_ref, k_ref, lo) in enumerate(((hlf_ref, hkf_ref, off_f), (hlb_ref, hkb_ref, off_b))):
            for c in range(0, d_br, cw):
                logf2, k = _forget_terms(proj(lo + c), lb_ref[which:which + 1, c:c + cw])
                l_ref[rows, c:c + cw] = logf2
                k_ref[rows, c:c + cw] = k.astype(BF16)
                p_ref, p_lo, p_c = next(plain)
                p_ref[rows, p_c:p_c + cw] = proj(p_lo).astype(BF16)
        each(hg_ref, off_g, lambda a: _silu(a))
        off = off_u + d_br
        v = jnp.concatenate([proj(off + c) for c in range(0, d_br, cw)], axis=1)
        mean = jnp.mean(v, axis=-1, keepdims=True)
        var = jnp.mean(jnp.square(v - mean), axis=-1, keepdims=True)
        mvn_ref[rows, :] = ((v - mean) * lax.rsqrt(var + EPS) * vg_ref[...] + vb_ref[...]).astype(BF16)


def _in_proj(x2, mod, norm_g, w_in, lb, v_g, v_b, tables, *, layer, seq, tm, rope):
    rows, d = x2.shape
    d_br = d // 2
    per = seq // tm
    row = lambda w: pl.BlockSpec((tm, w), lambda i: (i, 0))
    tab = pl.BlockSpec((tm, LANES), lambda i: (i % per, 0))
    widths = dict(qa=d_br, ka=LANES, va=LANES, hq=d_br, hkf=d_br, hkb=d_br, hlf=d_br, hlb=d_br,
                  hv=d_br, hg=d_br, mu=d_br, mvn=d_br)
    assert w_in.shape[2] == sum(widths.values()) - 2 * d_br
    dtypes = {k: (F32 if k in ("hlf", "hlb") else BF16) for k in widths}
    outs = pl.pallas_call(
        functools.partial(_in_proj_kernel, rope=rope, d_br=d_br),
        grid=(rows // tm,),
        in_specs=[
            row(d),
            pl.BlockSpec((1, 6, d), lambda i: (i // per, 0, 0)),
            _resident(norm_g.shape, layer),
            _resident(w_in.shape, layer),
            _resident(lb.shape, layer),
            _resident(v_g.shape, layer),
            _resident(v_b.shape, layer),
            tab, tab, tab,
        ],
        out_specs=[row(w) for w in widths.values()],
        out_shape=[jax.ShapeDtypeStruct((rows, w), dtypes[k]) for k, w in widths.items()],
        compiler_params=_params("parallel"),
        name="in_proj_rope" if rope else "in_proj_ctx",
    )(x2, mod, norm_g, w_in, lb, v_g, v_b, *tables)
    return dict(zip(widths, outs))


def _attn_kernel(sink_ref, q_ref, *refs, band):
    if band:
        kl_ref, km_ref, kr_ref, vl_ref, vm_ref, vr_ref, kc_ref, vc_ref, o_ref = refs
        j = pl.program_id(1)
        nq = pl.num_programs(1)
        keys = jnp.concatenate([kl_ref[...], km_ref[...], kr_ref[...], kc_ref[...]], axis=0)
        vals = jnp.concatenate([vl_ref[...], vm_ref[...], vr_ref[...], vc_ref[...]], axis=0)
        qi = lax.broadcasted_iota(jnp.int32, (ATT_BLOCK, ATT_BLOCK), 0)
        kj = lax.broadcasted_iota(jnp.int32, (ATT_BLOCK, ATT_BLOCK), 1)
        left = jnp.where(jnp.logical_and(kj >= qi, j > 0), 0.0, NEG)
        right = jnp.where(jnp.logical_and(kj <= qi, j < nq - 1), 0.0, NEG)
    else:
        kc_ref, vc_ref, o_ref = refs
        keys = kc_ref[...]
        vals = vc_ref[...]
    tq = q_ref.shape[0]
    groups = q_ref.shape[1] // LANES
    rep = ATT_HEADS // ATT_KV_HEADS
    left_half = lax.broadcasted_iota(jnp.int32, (tq, LANES), 1) < HEAD_DIM
    top = lax.broadcasted_iota(jnp.int32, (2 * tq, 1), 0) < tq
    scores = []
    for g in range(groups):
        qc = q_ref[:, g * LANES:(g + 1) * LANES]
        zero = jnp.zeros_like(qc)
        q2 = jnp.concatenate([jnp.where(left_half, qc, zero), jnp.where(left_half, zero, qc)], axis=0)
        scores.append(_dot_nt(q2, keys))
    for g, s in enumerate(scores):
        if band:
            two = lambda t: jnp.concatenate([t, t], axis=0)
            s = jnp.concatenate([s[:, :ATT_BLOCK] + two(left), s[:, ATT_BLOCK:2 * ATT_BLOCK],
                                 s[:, 2 * ATT_BLOCK:3 * ATT_BLOCK] + two(right), s[:, 3 * ATT_BLOCK:]],
                                axis=1)
        sink = jnp.where(top, sink_ref[g], sink_ref[g + rep]) * LOG2E
        m = jnp.maximum(jnp.max(s, axis=-1, keepdims=True), sink)
        e = jnp.exp2(s - m)
        den = jnp.sum(e, axis=-1, keepdims=True) + jnp.exp2(sink - m)
        o = _dot(e.astype(BF16), vals) * (1.0 / den)
        o_ref[:, g * LANES:(g + 1) * LANES] = jnp.where(left_half, o[:tq], o[tq:]).astype(BF16)


def _attention(q, k, v, kc, vc, sink, *, batch, seq, ctx_len, band):
    d_q = q.shape[1]
    tq = ATT_BLOCK
    nq = seq // tq
    smem = pl.BlockSpec(memory_space=pltpu.SMEM)
    qspec = pl.BlockSpec((tq, d_q), lambda b, j: (b * nq + j, 0))
    cspec = pl.BlockSpec((ctx_len, LANES), lambda b, j: (b, 0))
    if band:
        kvspecs = [
            pl.BlockSpec((tq, LANES), lambda b, j: (b * nq + jnp.maximum(j - 1, 0), 0)),
            pl.BlockSpec((tq, LANES), lambda b, j: (b * nq + j, 0)),
            pl.BlockSpec((tq, LANES), lambda b, j: (b * nq + jnp.minimum(j + 1, nq - 1), 0)),
        ]
        in_specs = [smem, qspec] + kvspecs + kvspecs + [cspec, cspec]
        args = (sink, q, k, k, k, v, v, v, kc, vc)
    else:
        in_specs = [smem, qspec, cspec, cspec]
        args = (sink, q, kc, vc)
    return pl.pallas_call(
        functools.partial(_attn_kernel, band=band),
        grid=(batch, nq),
        in_specs=in_specs,
        out_specs=qspec,
        out_shape=jax.ShapeDtypeStruct(q.shape, BF16),
        compiler_params=_params("parallel", "parallel"),
        name="attn_band" if band else "attn_ctx",
    )(*args)


HG_LAG = 2 * HG_HEADS
HG_LEVELS = tuple(HG_CHUNK >> i for i in range(1, HG_CHUNK.bit_length()))


def _hgrn_tables():
    c = HG_CHUNK
    t = np.arange(c)[:, None]
    s = np.arange(c)[None, :]
    mats, lvls = [], []
    for rev in (False, True):
        lvl = np.full((c, c), -1, np.int32)
        lvl[np.arange(c), np.arange(c)] = 0
        for idx, b in enumerate(HG_LEVELS):
            same = (t // (2 * b)) == (s // (2 * b))
            t_hi, s_hi = (t % (2 * b)) >= b, (s % (2 * b)) >= b
            lvl[same & ((~t_hi & s_hi) if rev else (t_hi & ~s_hi))] = idx + 1
        mat = ((s >= t) if rev else (s <= t)).astype(np.float32)
        mats.append(np.concatenate([mat, mat], axis=1))
        lvls.append(lvl)
    return np.stack(mats), np.stack(lvls)


def _boundary_exponent(g, g_ref, lg, lanes, b, rev):
    c = g.shape[0]
    row = lax.broadcasted_iota(jnp.int32, (c, 1), 0)
    bound = lambda g0: g0 + b - (0 if rev else 1)
    if b == 1:
        return jnp.where(((row & 1) != 0) != rev, lg, 0.0)
    if b >= 8:
        pieces = []
        for g0 in range(0, c, 2 * b):
            gr = g_ref[bound(g0):bound(g0) + 1, lanes]
            first, second = g[g0:g0 + b], g[g0 + b:g0 + 2 * b]
            pieces += [first - gr, gr - second] if rev else [gr - first, second - gr]
        return jnp.concatenate(pieces, axis=0)
    sub = 8
    pieces = []
    for r0 in range(0, c, sub):
        refs = [jnp.broadcast_to(g_ref[bound(g0):bound(g0) + 1, lanes], (sub, LANES))
                for g0 in range(r0, r0 + sub, 2 * b)]
        gr = refs[0]
        for n in range(1, len(refs)):
            gr = jnp.where((row[:sub] & (sub - 1)) >= n * 2 * b, refs[n], gr)
        pieces.append(gr)
    return -jnp.abs(g - jnp.concatenate(pieces, axis=0))


def _mixed_rows(q, k, b, rev):
    c = q.shape[0]
    if b >= 8:
        pieces = []
        for g0 in range(0, c, 2 * b):
            lo, hi = slice(g0, g0 + b), slice(g0 + b, g0 + 2 * b)
            pieces += [q[lo], k[hi]] if rev else [k[lo], q[hi]]
        return jnp.concatenate(pieces, axis=0)
    later = (lax.broadcasted_iota(jnp.int32, (c, 1), 0) & b) != 0
    return jnp.where(later, k, q) if rev else jnp.where(later, q, k)


def _hgrn_scores(q, k, lg, g_ref, lanes, lvl, rev):
    c = HG_CHUNK
    g = g_ref[:, lanes]
    qf, kf = q.astype(F32), k.astype(F32)
    blocks = [(q, k)]
    for b in HG_LEVELS:
        e = jnp.exp2(_boundary_exponent(g, g_ref, lg, lanes, b, rev))
        y = (_mixed_rows(qf, kf, b, rev) * e).astype(BF16)
        blocks.append((y, y))
    zero = jnp.zeros_like(k)
    att = jnp.zeros((c, c), F32)
    for i in range(0, len(blocks), 2):
        (la, ra), (lb, rb) = blocks[i], blocks[i + 1]
        rhs = jnp.concatenate([jnp.concatenate([ra, zero], axis=1),
                               jnp.concatenate([zero, rb], axis=1)], axis=0)
        s = _dot_nt(jnp.concatenate([la, lb], axis=1), rhs)
        att = jnp.where(lvl == i, s[:, :c], jnp.where(lvl == i + 1, s[:, c:], att))
    return att.astype(BF16)


def _hgrn_apply(q, k, v, att, g_ref, lanes, state, rev):
    g = g_ref[:, lanes]
    end = 0 if rev else HG_CHUNK - 1
    g_end = g_ref[end:end + 1, lanes]
    lhs = jnp.concatenate([(q.astype(F32) * jnp.exp2(g)).astype(BF16), att], axis=1)
    rhs = jnp.concatenate([state.astype(BF16), v], axis=0)
    out = _dot(lhs, rhs)
    dec_t = jnp.broadcast_to(jnp.exp2(g_end), (HG_DK, LANES)).T
    new_state = dec_t * state + _dot_tn((k.astype(F32) * jnp.exp2(g_end - g)).astype(BF16), v)
    return out, new_state


def _hgrn_kernel(qf_ref, kf_ref, lf_ref, vf_ref, qb_ref, kb_ref, lb_ref, vb_ref, s0_ref, m_ref, lvl_ref,
                 of_ref, ob_ref, sfin_ref, state_ref, g_ref):
    i = pl.program_id(1)

    @pl.when(i == 0)
    def _():
        state_ref[...] = s0_ref[...]

    dirs = ((qf_ref, kf_ref, lf_ref, vf_ref, of_ref), (qb_ref, kb_ref, lb_ref, vb_ref, ob_ref))
    work = []
    for d, (q_ref, k_ref, l_ref, v_ref, o_ref) in enumerate(dirs):
        lg = l_ref[...]
        hi = lg.astype(BF16)
        lo = (lg - hi.astype(F32)).astype(BF16)
        g_ref[d] = _dot(m_ref[d], jnp.concatenate([hi, lo], axis=0))
        for h in range(HG_HEADS):
            lanes = slice(h * LANES, (h + 1) * LANES)
            work.append((d, h, lanes, o_ref, q_ref[:, lanes], k_ref[:, lanes], v_ref[:, lanes],
                         lg[:, lanes], state_ref[d, h]))
    atts, done = [], []
    for n in range(len(work) + HG_LAG):
        if n < len(work):
            d, h, lanes, o_ref, q, k, v, lg, state = work[n]
            atts.append(_hgrn_scores(q, k, lg, g_ref.at[d], lanes, lvl_ref[d], bool(d)))
        if n >= HG_LAG:
            d, h, lanes, o_ref, q, k, v, lg, state = work[n - HG_LAG]
            out, new_state = _hgrn_apply(q, k, v, atts[n - HG_LAG], g_ref.at[d], lanes, state, bool(d))
            done.append((d, h, lanes, o_ref, out, new_state))
    for d, h, lanes, o_ref, out, new_state in done:
        state_ref[d, h] = new_state
        o_ref[:, lanes] = out.astype(o_ref.dtype)

    @pl.when(i == pl.num_programs(1) - 1)
    def _():
        sfin_ref[...] = state_ref[...]


def _hgrn(p, s0, *, batch, seq):
    c = HG_CHUNK
    n = seq // c
    width = p["hq"].shape[1]
    mats, lvls = _hgrn_tables()
    fwd = pl.BlockSpec((c, width), lambda b, i: (b * n + i, 0))
    bwd = pl.BlockSpec((c, width), lambda b, i: (b * n + (n - 1 - i), 0))
    st = pl.BlockSpec((None, 2, HG_HEADS, HG_DK, LANES), lambda b, i: (b, 0, 0, 0, 0))
    rows = batch * seq
    return pl.pallas_call(
        _hgrn_kernel,
        grid=(batch, n),
        in_specs=[fwd, fwd, fwd, fwd, bwd, bwd, bwd, bwd, st, _resident(mats.shape), _resident(lvls.shape)],
        out_specs=[fwd, bwd, st],
        out_shape=[jax.ShapeDtypeStruct((rows, width), BF16),
                   jax.ShapeDtypeStruct((rows, width), BF16),
                   jax.ShapeDtypeStruct(s0.shape, F32)],
        scratch_shapes=[pltpu.VMEM((2, HG_HEADS, HG_DK, LANES), F32),
                        pltpu.VMEM((2, mats.shape[1], width), F32)],
        compiler_params=_params("parallel", "arbitrary"),
        name="hgrn_scan",
    )(p["hq"], p["hkf"], p["hlf"], p["hv"], p["hq"], p["hkb"], p["hlb"], p["hv"], s0,
      jnp.asarray(mats, BF16), jnp.asarray(lvls))


def _merge_kernel(x_ref, mod_ref, a_ref, of_ref, ob_ref, hg_ref, mu_ref, mvn_ref,
                  g_ref, wgt_ref, hgn_ref, ws_ref, bs_ref, wbr_ref, wo_ref, o_ref):
    tm, d = x_ref.shape
    cw = MXU_COLS
    for r0 in range(0, tm, MERGE_ROWS):
        rows = pl.ds(r0, MERGE_ROWS)
        h = _rms_mod(x_ref[rows, :], g_ref[...], mod_ref[0, 1:2, :], mod_ref[0, 0:1, :]).astype(BF16)
        o = of_ref[rows, :].astype(F32) + ob_ref[rows, :].astype(F32)
        parts = []
        for head in range(HG_HEADS):
            oh = o[:, head * LANES:(head + 1) * LANES]
            parts.append(oh * lax.rsqrt(jnp.mean(oh * oh, axis=-1, keepdims=True) + EPS))
        b_br = (jnp.concatenate(parts, axis=1) * hgn_ref[...] * hg_ref[rows, :].astype(F32)).astype(BF16)
        gated = []
        for r in range(r0, r0 + MERGE_ROWS, MLP_CHUNK):
            cols = []
            for grp in range(MLP_GROUPS):
                cs = slice(grp * LANES, (grp + 1) * LANES)
                cols.append(_dot(ws_ref[grp], mvn_ref[pl.ds(r, MLP_CHUNK), cs]) + bs_ref[grp])
            gated.append(jnp.concatenate(cols, axis=1))
        c_br = (mu_ref[rows, :].astype(F32) * jnp.concatenate(gated, axis=0)).astype(BF16)
        branches = (a_ref[rows, :], b_br, c_br)
        merged = []
        for c in range(0, d, cw):
            m = jnp.zeros((MERGE_ROWS, cw), F32)
            for n, br in enumerate(branches):
                gate = _sigmoid(_dot(h, wgt_ref[:, n * d + c:n * d + c + cw]))
                m = m + gate * _dot(br, wbr_ref[n, :, c:c + cw])
            merged.append(m.astype(BF16))
        merged = jnp.concatenate(merged, axis=1)
        for c in range(0, d, cw):
            y = _dot(merged, wo_ref[:, c:c + cw])
            o_ref[rows, c:c + cw] = x_ref[rows, c:c + cw] + mod_ref[0, 2:3, c:c + cw] * y


def _merge(x2, mod, a, of, ob, p, norm_g, w_gates, hg_g, ws, bs, w_br, w_o, *, layer, seq, tm):
    rows, d = x2.shape
    d_br = d // 2
    per = seq // tm
    row = lambda w: pl.BlockSpec((tm, w), lambda i: (i, 0))
    weights = (norm_g, w_gates, hg_g, ws, bs, w_br, w_o)
    return pl.pallas_call(
        _merge_kernel,
        grid=(rows // tm,),
        in_specs=[
            row(d),
            pl.BlockSpec((1, 6, d), lambda i: (i // per, 0, 0)),
            row(d_br), row(d_br), row(d_br), row(d_br), row(d_br), row(d_br),
        ] + [_resident(w.shape, layer) for w in weights],
        out_specs=row(d),
        out_shape=jax.ShapeDtypeStruct((rows, d), F32),
        compiler_params=_params("parallel"),
        name="merge",
    )(x2, mod, a, of, ob, p["hg"], p["mu"], p["mvn"], *weights)


def _ffn_kernel(x_ref, mod_ref, g_ref, wg_ref, wu_ref, wd_ref, fg_ref, o_ref, act_ref, *, final, bf):
    for rows in (pl.ds(r, FFN_ROWS) for r in range(0, x_ref.shape[0], FFN_ROWS)):
        x = x_ref[rows, :]
        h = _rms_mod(x, g_ref[...], mod_ref[0, 4:5, :], mod_ref[0, 3:4, :]).astype(BF16)
        for c in range(wg_ref.shape[1] // bf):
            sl = slice(c * bf, (c + 1) * bf)
            act_ref[rows, sl] = (_silu(_dot(h, wg_ref[:, sl])) * _dot(h, wu_ref[:, sl])).astype(BF16)
        y = x + mod_ref[0, 5:6, :] * _dot(act_ref[rows, :], wd_ref[...])
        if final:
            y = y * lax.rsqrt(jnp.mean(y * y, axis=-1, keepdims=True) + EPS) * fg_ref[...]
        o_ref[rows, :] = y


def _ffn(x2, mod, norm_g, w_gate, w_up, w_down, final_g, *, layer, seq, tm, final):
    rows, d = x2.shape
    d_ff = w_gate.shape[2]
    per = seq // tm
    row = pl.BlockSpec((tm, d), lambda i: (i, 0))
    return pl.pallas_call(
        functools.partial(_ffn_kernel, final=final, bf=MXU_COLS),
        grid=(rows // tm,),
        in_specs=[
            row,
            pl.BlockSpec((1, 6, d), lambda i: (i // per, 0, 0)),
            _resident(norm_g.shape, layer),
            _resident(w_gate.shape, layer),
            _resident(w_up.shape, layer),
            _resident(w_down.shape, layer),
            _resident((1, d)),
        ],
        out_specs=row,
        out_shape=jax.ShapeDtypeStruct((rows, d), F32),
        scratch_shapes=[pltpu.VMEM((tm, d_ff), BF16)],
        compiler_params=_params("parallel"),
        name="ffn_final" if final else "ffn",
    )(x2, mod, norm_g, w_gate, w_up, w_down, final_g)


def _rope_tables(seq):
    pos = jnp.arange(seq)
    nf = HEAD_DIM // 4
    inv = ROPE_BASE ** (-jnp.arange(nf, dtype=F32) / nf)
    ang_r = (pos // GRID_W).astype(F32)[:, None] * inv[None, :]
    ang_c = (pos % GRID_W).astype(F32)[:, None] * inv[None, :]
    zero = jnp.zeros_like(ang_r)
    cos = jnp.concatenate([jnp.cos(ang_r)] * 2 + [jnp.cos(ang_c)] * 2, axis=1)
    sa = jnp.concatenate([zero, jnp.sin(ang_r), zero, jnp.sin(ang_c)], axis=1)
    sb = jnp.concatenate([-jnp.sin(ang_r), zero, -jnp.sin(ang_c), zero], axis=1)
    reps = LANES // HEAD_DIM
    return tuple(jnp.tile(t, (1, reps)) for t in (cos, sa, sb))


def kernel(x, c, ctx, c_ctx, w_ada, b_ada, norm1_g, norm2_g, w_in, attn_sink, hg_lb_logits, hg_norm_g, mlp_v_norm_g, mlp_v_norm_b, mlp_ws, mlp_bs, w_branch, w_out, w_ffn_gate, w_ffn_up, w_ffn_down, final_norm_g):
    batch, seq, d = x.shape
    ctx_len = ctx.shape[1]
    depth = w_ada.shape[0]
    d_br = d // 2
    assert batch + 1 <= MOD_ROWS and seq % 512 == 0 and ctx_len % 256 == 0

    lb = jnp.cumsum(jax.nn.softmax(hg_lb_logits.astype(F32), axis=0), axis=0)
    lb = lb - lb[0]
    cs = jnp.concatenate([c, c_ctx[None, :], jnp.zeros((MOD_ROWS - batch - 1, d), F32)], axis=0)
    mod = _ada(cs, w_ada, b_ada).reshape(depth, MOD_ROWS, 6, d)
    tables = _rope_tables(seq)
    q_cols = jnp.concatenate([jnp.arange(h * HEAD_DIM, (h + 1) * HEAD_DIM) for h in Q_HEAD_ORDER])
    zero_state = jnp.zeros((batch, 2, HG_HEADS, HG_DK, LANES), F32)

    d_gate = N_BRANCH * d
    w_proj = jnp.concatenate([w_in[:, :, :d_br][:, :, q_cols], w_in[:, :, d_br:-d_gate]],
                             axis=2).astype(BF16)
    w_gates = w_in[:, :, -d_gate:].astype(BF16)
    w_br = jnp.concatenate([w_branch[:, :1][:, :, q_cols], w_branch[:, 1:]], axis=1).astype(BF16)
    ws = mlp_ws.astype(BF16)
    bs = jnp.broadcast_to(mlp_bs[:, :, :, None], mlp_bs.shape + (LANES,))
    w_o, wg, wu, wd = (w.astype(BF16) for w in (w_out, w_ffn_gate, w_ffn_up, w_ffn_down))
    n1, n2, vg, vb, hgn = (t[:, None, :] for t in (norm1_g, norm2_g, mlp_v_norm_g, mlp_v_norm_b, hg_norm_g))
    fg = final_norm_g[None, :]

    xs = x.reshape(batch * seq, d)
    cx = ctx.reshape(batch * ctx_len, d)
    for l in range(depth):
        need_ctx = l < depth - 1
        mod_x = mod[l, :batch]
        mod_c = jnp.broadcast_to(mod[l, batch:batch + 1], (batch, 6, d))
        pc = _in_proj(cx, mod_c, n1, w_proj, lb, vg, vb, tables, layer=l, seq=ctx_len, tm=256, rope=False)
        px = _in_proj(xs, mod_x, n1, w_proj, lb, vg, vb, tables, layer=l, seq=seq, tm=512, rope=True)
        a = _attention(px["qa"], px["ka"], px["va"], pc["ka"], pc["va"], attn_sink[l],
                       batch=batch, seq=seq, ctx_len=ctx_len, band=True)
        ocf, ocb, s_ctx = _hgrn(pc, zero_state, batch=batch, seq=ctx_len)
        of, ob, _ = _hgrn(px, s_ctx, batch=batch, seq=seq)
        xs = _merge(xs, mod_x, a, of, ob, px, n1, w_gates, hgn, ws, bs, w_br, w_o,
                    layer=l, seq=seq, tm=512)
        xs = _ffn(xs, mod_x, n2, wg, wu, wd, fg, layer=l, seq=seq, tm=512, final=not need_ctx)
        if need_ctx:
            ac = _attention(pc["qa"], None, None, pc["ka"], pc["va"], attn_sink[l],
                            batch=batch, seq=ctx_len, ctx_len=ctx_len, band=False)
            cx = _merge(cx, mod_c, ac, ocf, ocb, pc, n1, w_gates, hgn, ws, bs, w_br, w_o,
                        layer=l, seq=ctx_len, tm=256)
            cx = _ffn(cx, mod_c, n2, wg, wu, wd, fg, layer=l, seq=ctx_len, tm=256, final=False)
    return xs.reshape(batch, seq, d)
```

```python
import functools

import jax
import jax.numpy as jnp
import numpy as np
from jax import lax
from jax.experimental import pallas as pl
from jax.experimental.pallas import tpu as pltpu

F32 = jnp.float32
BF16 = jnp.bfloat16

EPS = 1e-6
GRID_W = 64
HEAD_DIM = 64
ATT_HEADS = 8
ATT_KV_HEADS = 2
WINDOW = 128
ATT_BLOCK = 128
ROPE_BASE = 10000.0
HG_DK = 128
HG_HEADS = 4
HG_CHUNK = 128
MLP_CHUNK = 128
MLP_GROUPS = 4
N_BRANCH = 3

LANES = 128
MXU_COLS = 256
IN_PROJ_ROWS = 128
FFN_ROWS = 128
MERGE_ROWS = 256
ROW_TILE = 1024
MOD_ROWS = 24
VMEM_LIMIT = 56 * 1024 * 1024
NEG = -1e30
LOG2E = 1.4426950408889634

Q_HEAD_ORDER = (0, 4, 1, 5, 2, 6, 3, 7)


def _sigmoid(x):
    return 0.5 * jnp.tanh(0.5 * x) + 0.5


def _silu(x):
    return x * _sigmoid(x)


def _rms_mod(x, g, sc, sh):
    y = x * lax.rsqrt(jnp.mean(x * x, axis=-1, keepdims=True) + EPS)
    return (y * g) * (1.0 + sc) + sh


def _dot(a, b):
    return jnp.dot(a, b, preferred_element_type=F32)


def _dot_nt(a, b):
    return lax.dot_general(a, b, (((1,), (1,)), ((), ())), preferred_element_type=F32)


def _dot_tn(a, b):
    return lax.dot_general(a, b, (((0,), (0,)), ((), ())), preferred_element_type=F32)


def _resident(shape, layer=None):
    if layer is None:
        return pl.BlockSpec(shape, lambda *_: (0,) * len(shape), pipeline_mode=pl.Buffered(1))
    rest = tuple(shape[1:])
    return pl.BlockSpec((None,) + rest, lambda *_: (layer,) + (0,) * len(rest),
                        pipeline_mode=pl.Buffered(1))


def _params(*sem):
    return pltpu.CompilerParams(dimension_semantics=sem, vmem_limit_bytes=VMEM_LIMIT)


def _ada_kernel(c_ref, w_ref, b_ref, o_ref):
    s = _silu(c_ref[...])
    o_ref[...] = jnp.dot(s, w_ref[...], precision=lax.Precision.HIGHEST,
                         preferred_element_type=F32) + b_ref[...]


def _ada(cs, w_ada, b_ada):
    depth, d, d6 = w_ada.shape
    bn = 1024
    return pl.pallas_call(
        _ada_kernel,
        grid=(depth, d6 // bn),
        in_specs=[
            pl.BlockSpec((MOD_ROWS, d), lambda l, j: (0, 0)),
            pl.BlockSpec((None, d, bn), lambda l, j: (l, 0, j)),
            pl.BlockSpec((None, 1, bn), lambda l, j: (l, 0, j)),
        ],
        out_specs=pl.BlockSpec((None, MOD_ROWS, bn), lambda l, j: (l, 0, j)),
        out_shape=jax.ShapeDtypeStruct((depth, MOD_ROWS, d6), F32),
        compiler_params=_params("parallel", "parallel"),
        name="ada",
    )(cs, w_ada, b_ada.reshape(depth, 1, d6))


def _rope(x, cos, sa, sb):
    return x * cos + pltpu.roll(x, 16, 1) * sa + pltpu.roll(x, LANES - 16, 1) * sb


def _forget_terms(z, lb):
    z2 = z * LOG2E
    u = jnp.exp2(-jnp.abs(z2))
    one_u = 1.0 + u
    log_sig = jnp.minimum(z2, 0.0) - jnp.log2(one_u)
    la = jnp.maximum(jnp.log2(lb), NEG)
    b = jnp.log1p(-lb) * LOG2E + log_sig
    logf2 = jnp.maximum(la, b) + jnp.log2(1.0 + jnp.exp2(-jnp.abs(la - b)))
    r = 1.0 / one_u
    k = (1.0 - lb) * jnp.where(z >= 0.0, u * r, r)
    return logf2, k


def _in_proj_kernel(x_ref, mod_ref, g_ref, w_ref, lb_ref, vg_ref, vb_ref, cos_ref, sa_ref, sb_ref,
                    *out_refs, rope, d_br, names):
    out = dict(zip(names, out_refs))
    cw = MXU_COLS
    off_q, off_kv, off_hq = 0, d_br, d_br + cw
    off_f, off_b, off_v, off_g, off_u, off_cv = (off_hq + n * d_br for n in range(1, 7))
    for rows in (pl.ds(r, IN_PROJ_ROWS) for r in range(0, x_ref.shape[0], IN_PROJ_ROWS)):
        h = _rms_mod(x_ref[rows, :], g_ref[...], mod_ref[0, 1:2, :], mod_ref[0, 0:1, :]).astype(BF16)

        def proj(lo):
            return _dot(h, w_ref[:, lo:lo + cw])

        def maybe_rope(t):
            if not rope:
                return t
            return _rope(t, cos_ref[rows, :], sa_ref[rows, :], sb_ref[rows, :])

        def each(name, lo, fn):
            if name in out:
                for c in range(0, out[name].shape[1], cw):
                    out[name][rows, c:c + cw] = fn(proj(lo + c)).astype(out[name].dtype)

        def rope_lanes(acc):
            parts = [maybe_rope(acc[:, s:s + LANES]) for s in range(0, cw, LANES)]
            return jnp.concatenate(parts, axis=1) * (HEAD_DIM ** -0.5 * LOG2E)

        each("qa", off_q, rope_lanes)
        acc = proj(off_kv)
        out["ka"][rows, :] = maybe_rope(acc[:, :LANES]).astype(BF16)
        out["va"][rows, :] = acc[:, LANES:].astype(BF16)
        each("hq", off_hq, lambda a: _silu(a) * HG_DK ** -0.5)
        plain = iter([(name, o + c, c) for name, o in (("hv", off_v), ("mu", off_u)) if name in out
                      for c in range(0, d_br, cw)])
        for which, (l_name, k_name, lo) in enumerate((("hlf", "hkf", off_f), ("hlb", "hkb", off_b))):
            for c in range(0, d_br, cw):
                logf2, k = _forget_terms(proj(lo + c), lb_ref[which:which + 1, c:c + cw])
                out[l_name][rows, c:c + cw] = logf2
                out[k_name][rows, c:c + cw] = k.astype(BF16)
                filler = next(plain, None)
                if filler is not None:
                    p_name, p_lo, p_c = filler
                    out[p_name][rows, p_c:p_c + cw] = proj(p_lo).astype(BF16)
        each("hg", off_g, lambda a: _silu(a))
        if "mvn" in out:
            v = jnp.concatenate([proj(off_cv + c) for c in range(0, d_br, cw)], axis=1)
            mean = jnp.mean(v, axis=-1, keepdims=True)
            var = jnp.mean(jnp.square(v - mean), axis=-1, keepdims=True)
            out["mvn"][rows, :] = ((v - mean) * lax.rsqrt(var + EPS) * vg_ref[...]
                                   + vb_ref[...]).astype(BF16)


STATE_ONLY_OUTPUTS = ("ka", "va", "hkf", "hkb", "hlf", "hlb", "hv")


def _in_proj(x2, mod, norm_g, w_in, lb, v_g, v_b, tables, *, layer, seq, tm, rope, only=None):
    rows, d = x2.shape
    d_br = d // 2
    per = seq // tm
    row = lambda w: pl.BlockSpec((tm, w), lambda i: (i, 0))
    tab = pl.BlockSpec((tm, LANES), lambda i: (i % per, 0))
    widths = dict(qa=d_br, ka=LANES, va=LANES, hq=d_br, hkf=d_br, hkb=d_br, hlf=d_br, hlb=d_br,
                  hv=d_br, hg=d_br, mu=d_br, mvn=d_br)
    assert w_in.shape[2] == sum(widths.values()) - 2 * d_br
    if only is not None:
        widths = {k: w for k, w in widths.items() if k in only}
    dtypes = {k: (F32 if k in ("hlf", "hlb") else BF16) for k in widths}
    outs = pl.pallas_call(
        functools.partial(_in_proj_kernel, rope=rope, d_br=d_br, names=tuple(widths)),
        grid=(rows // tm,),
        in_specs=[
            row(d),
            pl.BlockSpec((1, 6, d), lambda i: (i // per, 0, 0)),
            _resident(norm_g.shape, layer),
            _resident(w_in.shape, layer),
            _resident(lb.shape, layer),
            _resident(v_g.shape, layer),
            _resident(v_b.shape, layer),
            tab, tab, tab,
        ],
        out_specs=[row(w) for w in widths.values()],
        out_shape=[jax.ShapeDtypeStruct((rows, w), dtypes[k]) for k, w in widths.items()],
        compiler_params=_params("parallel"),
        name="in_proj_rope" if rope else "in_proj_ctx",
    )(x2, mod, norm_g, w_in, lb, v_g, v_b, *tables)
    return dict(zip(widths, outs))


def _attn_kernel(sink_ref, q_ref, *refs, band):
    if band:
        kl_ref, km_ref, kr_ref, vl_ref, vm_ref, vr_ref, kc_ref, vc_ref, o_ref = refs
        j = pl.program_id(1)
        nq = pl.num_programs(1)
        keys = jnp.concatenate([kl_ref[...], km_ref[...], kr_ref[...], kc_ref[...]], axis=0)
        vals = jnp.concatenate([vl_ref[...], vm_ref[...], vr_ref[...], vc_ref[...]], axis=0)
        qi = lax.broadcasted_iota(jnp.int32, (ATT_BLOCK, ATT_BLOCK), 0)
        kj = lax.broadcasted_iota(jnp.int32, (ATT_BLOCK, ATT_BLOCK), 1)
        left = jnp.where(jnp.logical_and(kj >= qi, j > 0), 0.0, NEG)
        right = jnp.where(jnp.logical_and(kj <= qi, j < nq - 1), 0.0, NEG)
    else:
        kc_ref, vc_ref, o_ref = refs
        keys = kc_ref[...]
        vals = vc_ref[...]
    tq = q_ref.shape[0]
    groups = q_ref.shape[1] // LANES
    rep = ATT_HEADS // ATT_KV_HEADS
    left_half = lax.broadcasted_iota(jnp.int32, (tq, LANES), 1) < HEAD_DIM
    top = lax.broadcasted_iota(jnp.int32, (2 * tq, 1), 0) < tq
    scores = []
    for g in range(groups):
        qc = q_ref[:, g * LANES:(g + 1) * LANES]
        zero = jnp.zeros_like(qc)
        q2 = jnp.concatenate([jnp.where(left_half, qc, zero), jnp.where(left_half, zero, qc)], axis=0)
        scores.append(_dot_nt(q2, keys))
    for g, s in enumerate(scores):
        if band:
            two = lambda t: jnp.concatenate([t, t], axis=0)
            s = jnp.concatenate([s[:, :ATT_BLOCK] + two(left), s[:, ATT_BLOCK:2 * ATT_BLOCK],
                                 s[:, 2 * ATT_BLOCK:3 * ATT_BLOCK] + two(right), s[:, 3 * ATT_BLOCK:]],
                                axis=1)
        sink = jnp.where(top, sink_ref[g], sink_ref[g + rep]) * LOG2E
        m = jnp.maximum(jnp.max(s, axis=-1, keepdims=True), sink)
        e = jnp.exp2(s - m)
        den = jnp.sum(e, axis=-1, keepdims=True) + jnp.exp2(sink - m)
        o = _dot(e.astype(BF16), vals) * (1.0 / den)
        o_ref[:, g * LANES:(g + 1) * LANES] = jnp.where(left_half, o[:tq], o[tq:]).astype(BF16)


def _attention(q, k, v, kc, vc, sink, *, batch, seq, ctx_len, band):
    d_q = q.shape[1]
    tq = ATT_BLOCK
    nq = seq // tq
    smem = pl.BlockSpec(memory_space=pltpu.SMEM)
    qspec = pl.BlockSpec((tq, d_q), lambda b, j: (b * nq + j, 0))
    cspec = pl.BlockSpec((ctx_len, LANES), lambda b, j: (b, 0))
    if band:
        kvspecs = [
            pl.BlockSpec((tq, LANES), lambda b, j: (b * nq + jnp.maximum(j - 1, 0), 0)),
            pl.BlockSpec((tq, LANES), lambda b, j: (b * nq + j, 0)),
            pl.BlockSpec((tq, LANES), lambda b, j: (b * nq + jnp.minimum(j + 1, nq - 1), 0)),
        ]
        in_specs = [smem, qspec] + kvspecs + kvspecs + [cspec, cspec]
        args = (sink, q, k, k, k, v, v, v, kc, vc)
    else:
        in_specs = [smem, qspec, cspec, cspec]
        args = (sink, q, kc, vc)
    return pl.pallas_call(
        functools.partial(_attn_kernel, band=band),
        grid=(batch, nq),
        in_specs=in_specs,
        out_specs=qspec,
        out_shape=jax.ShapeDtypeStruct(q.shape, BF16),
        compiler_params=_params("parallel", "parallel"),
        name="attn_band" if band else "attn_ctx",
    )(*args)


HG_LAG = 2 * HG_HEADS
HG_LEVELS = tuple(HG_CHUNK >> i for i in range(1, HG_CHUNK.bit_length()))


def _hgrn_tables():
    c = HG_CHUNK
    t = np.arange(c)[:, None]
    s = np.arange(c)[None, :]
    mats, lvls = [], []
    for rev in (False, True):
        lvl = np.full((c, c), -1, np.int32)
        lvl[np.arange(c), np.arange(c)] = 0
        for idx, b in enumerate(HG_LEVELS):
            same = (t // (2 * b)) == (s // (2 * b))
            t_hi, s_hi = (t % (2 * b)) >= b, (s % (2 * b)) >= b
            lvl[same & ((~t_hi & s_hi) if rev else (t_hi & ~s_hi))] = idx + 1
        mat = ((s >= t) if rev else (s <= t)).astype(np.float32)
        mats.append(np.concatenate([mat, mat], axis=1))
        lvls.append(lvl)
    return np.stack(mats), np.stack(lvls)


def _boundary_exponent(g, g_ref, lg, lanes, b, rev):
    c = g.shape[0]
    row = lax.broadcasted_iota(jnp.int32, (c, 1), 0)
    bound = lambda g0: g0 + b - (0 if rev else 1)
    if b == 1:
        return jnp.where(((row & 1) != 0) != rev, lg, 0.0)
    if b >= 8:
        pieces = []
        for g0 in range(0, c, 2 * b):
            gr = g_ref[bound(g0):bound(g0) + 1, lanes]
            first, second = g[g0:g0 + b], g[g0 + b:g0 + 2 * b]
            pieces += [first - gr, gr - second] if rev else [gr - first, second - gr]
        return jnp.concatenate(pieces, axis=0)
    sub = 8
    pieces = []
    for r0 in range(0, c, sub):
        refs = [jnp.broadcast_to(g_ref[bound(g0):bound(g0) + 1, lanes], (sub, LANES))
                for g0 in range(r0, r0 + sub, 2 * b)]
        gr = refs[0]
        for n in range(1, len(refs)):
            gr = jnp.where((row[:sub] & (sub - 1)) >= n * 2 * b, refs[n], gr)
        pieces.append(gr)
    return -jnp.abs(g - jnp.concatenate(pieces, axis=0))


def _mixed_rows(q, k, b, rev):
    c = q.shape[0]
    if b >= 8:
        pieces = []
        for g0 in range(0, c, 2 * b):
            lo, hi = slice(g0, g0 + b), slice(g0 + b, g0 + 2 * b)
            pieces += [q[lo], k[hi]] if rev else [k[lo], q[hi]]
        return jnp.concatenate(pieces, axis=0)
    later = (lax.broadcasted_iota(jnp.int32, (c, 1), 0) & b) != 0
    return jnp.where(later, k, q) if rev else jnp.where(later, q, k)


def _hgrn_scores(q, k, lg, g_ref, lanes, lvl, rev):
    c = HG_CHUNK
    g = g_ref[:, lanes]
    qf, kf = q.astype(F32), k.astype(F32)
    blocks = [(q, k)]
    for b in HG_LEVELS:
        e = jnp.exp2(_boundary_exponent(g, g_ref, lg, lanes, b, rev))
        y = (_mixed_rows(qf, kf, b, rev) * e).astype(BF16)
        blocks.append((y, y))
    zero = jnp.zeros_like(k)
    att = jnp.zeros((c, c), F32)
    for i in range(0, len(blocks), 2):
        (la, ra), (lb, rb) = blocks[i], blocks[i + 1]
        rhs = jnp.concatenate([jnp.concatenate([ra, zero], axis=1),
                               jnp.concatenate([zero, rb], axis=1)], axis=0)
        s = _dot_nt(jnp.concatenate([la, lb], axis=1), rhs)
        att = jnp.where(lvl == i, s[:, :c], jnp.where(lvl == i + 1, s[:, c:], att))
    return att.astype(BF16)


def _hgrn_apply(q, k, v, att, g_ref, lanes, state, rev):
    lhs = jnp.concatenate([(q.astype(F32) * jnp.exp2(g_ref[:, lanes])).astype(BF16), att], axis=1)
    rhs = jnp.concatenate([state.astype(BF16), v], axis=0)
    return _dot(lhs, rhs), _hgrn_next_state(k, v, g_ref, lanes, state, rev)


def _hgrn_next_state(k, v, g_ref, lanes, state, rev):
    g = g_ref[:, lanes]
    end = 0 if rev else HG_CHUNK - 1
    g_end = g_ref[end:end + 1, lanes]
    dec_t = jnp.broadcast_to(jnp.exp2(g_end), (HG_DK, LANES)).T
    return dec_t * state + _dot_tn((k.astype(F32) * jnp.exp2(g_end - g)).astype(BF16), v)


def _hgrn_cumsum(lg, m):
    hi = lg.astype(BF16)
    lo = (lg - hi.astype(F32)).astype(BF16)
    return _dot(m, jnp.concatenate([hi, lo], axis=0))


def _hgrn_state_kernel(kf_ref, lf_ref, vf_ref, kb_ref, lb_ref, vb_ref, s0_ref, m_ref,
                       sfin_ref, state_ref, g_ref):
    i = pl.program_id(1)

    @pl.when(i == 0)
    def _():
        state_ref[...] = s0_ref[...]

    for d, (k_ref, l_ref, v_ref) in enumerate(((kf_ref, lf_ref, vf_ref), (kb_ref, lb_ref, vb_ref))):
        g_ref[d] = _hgrn_cumsum(l_ref[...], m_ref[d])
        for h in range(HG_HEADS):
            lanes = slice(h * LANES, (h + 1) * LANES)
            state_ref[d, h] = _hgrn_next_state(k_ref[:, lanes], v_ref[:, lanes], g_ref.at[d], lanes,
                                               state_ref[d, h], bool(d))

    @pl.when(i == pl.num_programs(1) - 1)
    def _():
        sfin_ref[...] = state_ref[...]


def _hgrn_kernel(qf_ref, kf_ref, lf_ref, vf_ref, qb_ref, kb_ref, lb_ref, vb_ref, s0_ref, m_ref, lvl_ref,
                 of_ref, ob_ref, sfin_ref, state_ref, g_ref):
    i = pl.program_id(1)

    @pl.when(i == 0)
    def _():
        state_ref[...] = s0_ref[...]

    dirs = ((qf_ref, kf_ref, lf_ref, vf_ref, of_ref), (qb_ref, kb_ref, lb_ref, vb_ref, ob_ref))
    work = []
    for d, (q_ref, k_ref, l_ref, v_ref, o_ref) in enumerate(dirs):
        lg = l_ref[...]
        g_ref[d] = _hgrn_cumsum(lg, m_ref[d])
        for h in range(HG_HEADS):
            lanes = slice(h * LANES, (h + 1) * LANES)
            work.append((d, h, lanes, o_ref, q_ref[:, lanes], k_ref[:, lanes], v_ref[:, lanes],
                         lg[:, lanes], state_ref[d, h]))
    atts, done = [], []
    for n in range(len(work) + HG_LAG):
        if n < len(work):
            d, h, lanes, o_ref, q, k, v, lg, state = work[n]
            atts.append(_hgrn_scores(q, k, lg, g_ref.at[d], lanes, lvl_ref[d], bool(d)))
        if n >= HG_LAG:
            d, h, lanes, o_ref, q, k, v, lg, state = work[n - HG_LAG]
            out, new_state = _hgrn_apply(q, k, v, atts[n - HG_LAG], g_ref.at[d], lanes, state, bool(d))
            done.append((d, h, lanes, o_ref, out, new_state))
    for d, h, lanes, o_ref, out, new_state in done:
        state_ref[d, h] = new_state
        o_ref[:, lanes] = out.astype(o_ref.dtype)

    @pl.when(i == pl.num_programs(1) - 1)
    def _():
        sfin_ref[...] = state_ref[...]


def _hgrn(p, s0, *, batch, seq, state_only=False):
    c = HG_CHUNK
    n = seq // c
    width = p["hv"].shape[1]
    mats, lvls = _hgrn_tables()
    fwd = pl.BlockSpec((c, width), lambda b, i: (b * n + i, 0))
    bwd = pl.BlockSpec((c, width), lambda b, i: (b * n + (n - 1 - i), 0))
    st = pl.BlockSpec((None, 2, HG_HEADS, HG_DK, LANES), lambda b, i: (b, 0, 0, 0, 0))
    rows = batch * seq
    scratch = [pltpu.VMEM((2, HG_HEADS, HG_DK, LANES), F32), pltpu.VMEM((2, mats.shape[1], width), F32)]
    if state_only:
        return pl.pallas_call(
            _hgrn_state_kernel,
            grid=(batch, n),
            in_specs=[fwd, fwd, fwd, bwd, bwd, bwd, st, _resident(mats.shape)],
            out_specs=st,
            out_shape=jax.ShapeDtypeStruct(s0.shape, F32),
            scratch_shapes=scratch,
            compiler_params=_params("parallel", "arbitrary"),
            name="hgrn_state",
        )(p["hkf"], p["hlf"], p["hv"], p["hkb"], p["hlb"], p["hv"], s0, jnp.asarray(mats, BF16))
    return pl.pallas_call(
        _hgrn_kernel,
        grid=(batch, n),
        in_specs=[fwd, fwd, fwd, fwd, bwd, bwd, bwd, bwd, st, _resident(mats.shape), _resident(lvls.shape)],
        out_specs=[fwd, bwd, st],
        out_shape=[jax.ShapeDtypeStruct((rows, width), BF16),
                   jax.ShapeDtypeStruct((rows, width), BF16),
                   jax.ShapeDtypeStruct(s0.shape, F32)],
        scratch_shapes=scratch,
        compiler_params=_params("parallel", "arbitrary"),
        name="hgrn_scan",
    )(p["hq"], p["hkf"], p["hlf"], p["hv"], p["hq"], p["hkb"], p["hlb"], p["hv"], s0,
      jnp.asarray(mats, BF16), jnp.asarray(lvls))


def _merge_kernel(x_ref, mod_ref, a_ref, of_ref, ob_ref, hg_ref, mu_ref, mvn_ref,
                  g_ref, wgt_ref, hgn_ref, ws_ref, bs_ref, wbr_ref, wo_ref, o_ref):
    tm, d = x_ref.shape
    cw = MXU_COLS
    for r0 in range(0, tm, MERGE_ROWS):
        rows = pl.ds(r0, MERGE_ROWS)
        h = _rms_mod(x_ref[rows, :], g_ref[...], mod_ref[0, 1:2, :], mod_ref[0, 0:1, :]).astype(BF16)
        o = of_ref[rows, :].astype(F32) + ob_ref[rows, :].astype(F32)
        parts = []
        for head in range(HG_HEADS):
            oh = o[:, head * LANES:(head + 1) * LANES]
            parts.append(oh * lax.rsqrt(jnp.mean(oh * oh, axis=-1, keepdims=True) + EPS))
        b_br = (jnp.concatenate(parts, axis=1) * hgn_ref[...] * hg_ref[rows, :].astype(F32)).astype(BF16)
        gated = []
        for r in range(r0, r0 + MERGE_ROWS, MLP_CHUNK):
            cols = []
            for grp in range(MLP_GROUPS):
                cs = slice(grp * LANES, (grp + 1) * LANES)
                cols.append(_dot(ws_ref[grp], mvn_ref[pl.ds(r, MLP_CHUNK), cs]) + bs_ref[grp])
            gated.append(jnp.concatenate(cols, axis=1))
        c_br = (mu_ref[rows, :].astype(F32) * jnp.concatenate(gated, axis=0)).astype(BF16)
        branches = (a_ref[rows, :], b_br, c_br)
        merged = []
        for c in range(0, d, cw):
            m = jnp.zeros((MERGE_ROWS, cw), F32)
            for n, br in enumerate(branches):
                gate = _sigmoid(_dot(h, wgt_ref[:, n * d + c:n * d + c + cw]))
                m = m + gate * _dot(br, wbr_ref[n, :, c:c + cw])
            merged.append(m.astype(BF16))
        merged = jnp.concatenate(merged, axis=1)
        for c in range(0, d, cw):
            y = _dot(merged, wo_ref[:, c:c + cw])
            o_ref[rows, c:c + cw] = x_ref[rows, c:c + cw] + mod_ref[0, 2:3, c:c + cw] * y


def _merge(x2, mod, a, of, ob, p, norm_g, w_gates, hg_g, ws, bs, w_br, w_o, *, layer, seq, tm):
    rows, d = x2.shape
    d_br = d // 2
    per = seq // tm
    row = lambda w: pl.BlockSpec((tm, w), lambda i: (i, 0))
    weights = (norm_g, w_gates, hg_g, ws, bs, w_br, w_o)
    return pl.pallas_call(
        _merge_kernel,
        grid=(rows // tm,),
        in_specs=[
            row(d),
            pl.BlockSpec((1, 6, d), lambda i: (i // per, 0, 0)),
            row(d_br), row(d_br), row(d_br), row(d_br), row(d_br), row(d_br),
        ] + [_resident(w.shape, layer) for w in weights],
        out_specs=row(d),
        out_shape=jax.ShapeDtypeStruct((rows, d), F32),
        compiler_params=_params("parallel"),
        name="merge",
    )(x2, mod, a, of, ob, p["hg"], p["mu"], p["mvn"], *weights)


def _ffn_kernel(x_ref, mod_ref, g_ref, wg_ref, wu_ref, wd_ref, fg_ref, o_ref, act_ref, *, final, bf):
    for rows in (pl.ds(r, FFN_ROWS) for r in range(0, x_ref.shape[0], FFN_ROWS)):
        x = x_ref[rows, :]
        h = _rms_mod(x, g_ref[...], mod_ref[0, 4:5, :], mod_ref[0, 3:4, :]).astype(BF16)
        for c in range(wg_ref.shape[1] // bf):
            sl = slice(c * bf, (c + 1) * bf)
            act_ref[rows, sl] = (_silu(_dot(h, wg_ref[:, sl])) * _dot(h, wu_ref[:, sl])).astype(BF16)
        y = x + mod_ref[0, 5:6, :] * _dot(act_ref[rows, :], wd_ref[...])
        if final:
            y = y * lax.rsqrt(jnp.mean(y * y, axis=-1, keepdims=True) + EPS) * fg_ref[...]
        o_ref[rows, :] = y


def _ffn(x2, mod, norm_g, w_gate, w_up, w_down, final_g, *, layer, seq, tm, final):
    rows, d = x2.shape
    d_ff = w_gate.shape[2]
    per = seq // tm
    row = pl.BlockSpec((tm, d), lambda i: (i, 0))
    return pl.pallas_call(
        functools.partial(_ffn_kernel, final=final, bf=MXU_COLS),
        grid=(rows // tm,),
        in_specs=[
            row,
            pl.BlockSpec((1, 6, d), lambda i: (i // per, 0, 0)),
            _resident(norm_g.shape, layer),
            _resident(w_gate.shape, layer),
            _resident(w_up.shape, layer),
            _resident(w_down.shape, layer),
            _resident((1, d)),
        ],
        out_specs=row,
        out_shape=jax.ShapeDtypeStruct((rows, d), F32),
        scratch_shapes=[pltpu.VMEM((tm, d_ff), BF16)],
        compiler_params=_params("parallel"),
        name="ffn_final" if final else "ffn",
    )(x2, mod, norm_g, w_gate, w_up, w_down, final_g)


def _rope_tables(seq):
    pos = jnp.arange(seq)
    nf = HEAD_DIM // 4
    inv = ROPE_BASE ** (-jnp.arange(nf, dtype=F32) / nf)
    ang_r = (pos // GRID_W).astype(F32)[:, None] * inv[None, :]
    ang_c = (pos % GRID_W).astype(F32)[:, None] * inv[None, :]
    zero = jnp.zeros_like(ang_r)
    cos = jnp.concatenate([jnp.cos(ang_r)] * 2 + [jnp.cos(ang_c)] * 2, axis=1)
    sa = jnp.concatenate([zero, jnp.sin(ang_r), zero, jnp.sin(ang_c)], axis=1)
    sb = jnp.concatenate([-jnp.sin(ang_r), zero, -jnp.sin(ang_c), zero], axis=1)
    reps = LANES // HEAD_DIM
    return tuple(jnp.tile(t, (1, reps)) for t in (cos, sa, sb))


def kernel(x, c, ctx, c_ctx, w_ada, b_ada, norm1_g, norm2_g, w_in, attn_sink, hg_lb_logits, hg_norm_g, mlp_v_norm_g, mlp_v_norm_b, mlp_ws, mlp_bs, w_branch, w_out, w_ffn_gate, w_ffn_up, w_ffn_down, final_norm_g):
    batch, seq, d = x.shape
    ctx_len = ctx.shape[1]
    depth = w_ada.shape[0]
    d_br = d // 2
    tm_x, tm_c = min(ROW_TILE, seq), min(ROW_TILE // 4, ctx_len)
    assert batch + 1 <= MOD_ROWS and seq % tm_x == 0 and ctx_len % tm_c == 0 and tm_c % MERGE_ROWS == 0

    lb = jnp.cumsum(jax.nn.softmax(hg_lb_logits.astype(F32), axis=0), axis=0)
    lb = lb - lb[0]
    cs = jnp.concatenate([c, c_ctx[None, :], jnp.zeros((MOD_ROWS - batch - 1, d), F32)], axis=0)
    mod = _ada(cs, w_ada, b_ada).reshape(depth, MOD_ROWS, 6, d)
    tables = _rope_tables(seq)
    zero_state = jnp.zeros((batch, 2, HG_HEADS, HG_DK, LANES), F32)

    def reorder_heads(t, axis):
        rep = ATT_HEADS // ATT_KV_HEADS
        split = t.shape[:axis] + (ATT_KV_HEADS, rep, HEAD_DIM) + t.shape[axis + 1:]
        return jnp.swapaxes(t.reshape(split), axis, axis + 1).reshape(t.shape)

    d_gate = N_BRANCH * d
    w_proj = jnp.concatenate([reorder_heads(w_in[:, :, :d_br], 2), w_in[:, :, d_br:-d_gate]],
                             axis=2).astype(BF16)
    w_gates = w_in[:, :, -d_gate:].astype(BF16)
    w_br = jnp.concatenate([reorder_heads(w_branch[:, :1], 2), w_branch[:, 1:]], axis=1).astype(BF16)
    ws = mlp_ws.astype(BF16)
    bs = jnp.broadcast_to(mlp_bs[:, :, :, None], mlp_bs.shape + (LANES,))
    w_o, wg, wu, wd = (w.astype(BF16) for w in (w_out, w_ffn_gate, w_ffn_up, w_ffn_down))
    n1, n2, vg, vb, hgn = (t[:, None, :] for t in (norm1_g, norm2_g, mlp_v_norm_g, mlp_v_norm_b, hg_norm_g))
    fg = final_norm_g[None, :]

    xs = x.reshape(batch * seq, d)
    cx = ctx.reshape(batch * ctx_len, d)
    for l in range(depth):
        need_ctx = l < depth - 1
        mod_x = mod[l, :batch]
        mod_c = jnp.broadcast_to(mod[l, batch:batch + 1], (batch, 6, d))
        pc = _in_proj(cx, mod_c, n1, w_proj, lb, vg, vb, tables, layer=l, seq=ctx_len, tm=tm_c, rope=False,
                      only=None if need_ctx else STATE_ONLY_OUTPUTS)
        px = _in_proj(xs, mod_x, n1, w_proj, lb, vg, vb, tables, layer=l, seq=seq, tm=tm_x, rope=True)
        a = _attention(px["qa"], px["ka"], px["va"], pc["ka"], pc["va"], attn_sink[l],
                       batch=batch, seq=seq, ctx_len=ctx_len, band=True)
        if need_ctx:
            ocf, ocb, s_ctx = _hgrn(pc, zero_state, batch=batch, seq=ctx_len)
        else:
            s_ctx = _hgrn(pc, zero_state, batch=batch, seq=ctx_len, state_only=True)
        of, ob, _ = _hgrn(px, s_ctx, batch=batch, seq=seq)
        xs = _merge(xs, mod_x, a, of, ob, px, n1, w_gates, hgn, ws, bs, w_br, w_o,
                    layer=l, seq=seq, tm=tm_x)
        xs = _ffn(xs, mod_x, n2, wg, wu, wd, fg, layer=l, seq=seq, tm=tm_x, final=not need_ctx)
        if need_ctx:
            ac = _attention(pc["qa"], None, None, pc["ka"], pc["va"], attn_sink[l],
                            batch=batch, seq=ctx_len, ctx_len=ctx_len, band=False)
            cx = _merge(cx, mod_c, ac, ocf, ocb, pc, n1, w_gates, hgn, ws, bs, w_br, w_o,
                        layer=l, seq=ctx_len, tm=tm_c)
            cx = _ffn(cx, mod_c, n2, wg, wu, wd, fg, layer=l, seq=ctx_len, tm=tm_c, final=False)
    return xs.reshape(batch, seq, d)
```

```python
import functools
import math

import jax
import jax.numpy as jnp
import numpy as np
from jax import lax
from jax.experimental import pallas as pl
from jax.experimental.pallas import tpu as pltpu

F32 = jnp.float32
BF16 = jnp.bfloat16

EPS = 1e-6
GRID_W = 64
HEAD_DIM = 64
ATT_HEADS = 8
ATT_KV_HEADS = 2
WINDOW = 128
ATT_BLOCK = 128
ROPE_BASE = 10000.0
HG_DK = 128
HG_HEADS = 4
HG_CHUNK = 128
MLP_CHUNK = 128
MLP_GROUPS = 4
N_BRANCH = 3

LANES = 128
MXU_COLS = 256
IN_PROJ_ROWS = 128
FFN_ROWS = 128
MERGE_ROWS = 256
ATT_SAMPLES = 4
ROW_TILE = 1024
MOD_ROWS = 24
VMEM_LIMIT = 56 * 1024 * 1024
NEG = -1e30
LOG2E = 1.4426950408889634

Q_HEAD_ORDER = (0, 4, 1, 5, 2, 6, 3, 7)


def _sigmoid(x):
    return 0.5 * jnp.tanh(0.5 * x) + 0.5


def _silu(x):
    return x * _sigmoid(x)


def _rms_mod(x, g, sc, sh):
    y = x * lax.rsqrt(jnp.mean(x * x, axis=-1, keepdims=True) + EPS)
    return (y * g) * (1.0 + sc) + sh


def _dot(a, b):
    return jnp.dot(a, b, preferred_element_type=F32)


def _dot_nt(a, b):
    return lax.dot_general(a, b, (((1,), (1,)), ((), ())), preferred_element_type=F32)


def _dot_tn(a, b):
    return lax.dot_general(a, b, (((0,), (0,)), ((), ())), preferred_element_type=F32)


def _resident(shape, layer=None):
    if layer is None:
        return pl.BlockSpec(shape, lambda *_: (0,) * len(shape), pipeline_mode=pl.Buffered(1))
    rest = tuple(shape[1:])
    return pl.BlockSpec((None,) + rest, lambda *_: (layer,) + (0,) * len(rest),
                        pipeline_mode=pl.Buffered(1))


def _params(*sem):
    return pltpu.CompilerParams(dimension_semantics=sem, vmem_limit_bytes=VMEM_LIMIT)


def _ada_kernel(c_ref, w_ref, b_ref, o_ref):
    s = _silu(c_ref[...])
    o_ref[...] = jnp.dot(s, w_ref[...], precision=lax.Precision.HIGHEST,
                         preferred_element_type=F32) + b_ref[...]


def _ada(cs, w_ada, b_ada):
    depth, d, d6 = w_ada.shape
    bn = 1024
    return pl.pallas_call(
        _ada_kernel,
        grid=(depth, d6 // bn),
        in_specs=[
            pl.BlockSpec((MOD_ROWS, d), lambda l, j: (0, 0)),
            pl.BlockSpec((None, d, bn), lambda l, j: (l, 0, j)),
            pl.BlockSpec((None, 1, bn), lambda l, j: (l, 0, j)),
        ],
        out_specs=pl.BlockSpec((None, MOD_ROWS, bn), lambda l, j: (l, 0, j)),
        out_shape=jax.ShapeDtypeStruct((depth, MOD_ROWS, d6), F32),
        compiler_params=_params("parallel", "parallel"),
        name="ada",
    )(cs, w_ada, b_ada.reshape(depth, 1, d6))


def _rope(x, cos, sa, sb):
    return x * cos + pltpu.roll(x, 16, 1) * sa + pltpu.roll(x, LANES - 16, 1) * sb


def _forget_terms(z, lb):
    z2 = z * LOG2E
    u = jnp.exp2(-jnp.abs(z2))
    one_u = 1.0 + u
    log_sig = jnp.minimum(z2, 0.0) - jnp.log2(one_u)
    la = jnp.maximum(jnp.log2(lb), NEG)
    b = jnp.log1p(-lb) * LOG2E + log_sig
    logf2 = jnp.maximum(la, b) + jnp.log2(1.0 + jnp.exp2(-jnp.abs(la - b)))
    r = 1.0 / one_u
    k = (1.0 - lb) * jnp.where(z >= 0.0, u * r, r)
    return logf2, k


def _in_proj_kernel(x_ref, mod_ref, g_ref, w_ref, lb_ref, vg_ref, vb_ref, cos_ref, sa_ref, sb_ref,
                    *out_refs, rope, d_br, names):
    out = dict(zip(names, out_refs))
    cw = MXU_COLS
    off_q, off_kv, off_hq = 0, d_br, d_br + cw
    off_f, off_b, off_v, off_g, off_u, off_cv = (off_hq + n * d_br for n in range(1, 7))
    for rows in (pl.ds(r, IN_PROJ_ROWS) for r in range(0, x_ref.shape[0], IN_PROJ_ROWS)):
        h = _rms_mod(x_ref[rows, :], g_ref[...], mod_ref[0, 1:2, :], mod_ref[0, 0:1, :]).astype(BF16)

        def proj(lo):
            return _dot(h, w_ref[:, lo:lo + cw])

        def maybe_rope(t):
            if not rope:
                return t
            return _rope(t, cos_ref[rows, :], sa_ref[rows, :], sb_ref[rows, :])

        def each(name, lo, fn):
            if name in out:
                for c in range(0, out[name].shape[1], cw):
                    out[name][rows, c:c + cw] = fn(proj(lo + c)).astype(out[name].dtype)

        def rope_lanes(acc):
            parts = [maybe_rope(acc[:, s:s + LANES]) for s in range(0, cw, LANES)]
            return jnp.concatenate(parts, axis=1) * (HEAD_DIM ** -0.5 * LOG2E)

        each("qa", off_q, rope_lanes)
        acc = proj(off_kv)
        out["ka"][rows, :] = maybe_rope(acc[:, :LANES]).astype(BF16)
        out["va"][rows, :] = acc[:, LANES:].astype(BF16)
        each("hq", off_hq, lambda a: _silu(a) * HG_DK ** -0.5)
        plain = iter([(name, o + c, c) for name, o in (("hv", off_v), ("mu", off_u)) if name in out
                      for c in range(0, d_br, cw)])
        for which, (l_name, k_name, lo) in enumerate((("hlf", "hkf", off_f), ("hlb", "hkb", off_b))):
            for c in range(0, d_br, cw):
                logf2, k = _forget_terms(proj(lo + c), lb_ref[which:which + 1, c:c + cw])
                out[l_name][rows, c:c + cw] = logf2
                out[k_name][rows, c:c + cw] = k.astype(BF16)
                filler = next(plain, None)
                if filler is not None:
                    p_name, p_lo, p_c = filler
                    out[p_name][rows, p_c:p_c + cw] = proj(p_lo).astype(BF16)
        each("hg", off_g, lambda a: _silu(a))
        if "mvn" in out:
            v = jnp.concatenate([proj(off_cv + c) for c in range(0, d_br, cw)], axis=1)
            mean = jnp.mean(v, axis=-1, keepdims=True)
            var = jnp.mean(jnp.square(v - mean), axis=-1, keepdims=True)
            out["mvn"][rows, :] = ((v - mean) * lax.rsqrt(var + EPS) * vg_ref[...]
                                   + vb_ref[...]).astype(BF16)


STATE_ONLY_OUTPUTS = ("ka", "va", "hkf", "hkb", "hlf", "hlb", "hv")


def _in_proj(x2, mod, norm_g, w_in, lb, v_g, v_b, tables, *, layer, seq, tm, rope, only=None):
    rows, d = x2.shape
    d_br = d // 2
    per = seq // tm
    row = lambda w: pl.BlockSpec((tm, w), lambda i: (i, 0))
    tab = pl.BlockSpec((tm, LANES), lambda i: (i % per, 0))
    widths = dict(qa=d_br, ka=LANES, va=LANES, hq=d_br, hkf=d_br, hkb=d_br, hlf=d_br, hlb=d_br,
                  hv=d_br, hg=d_br, mu=d_br, mvn=d_br)
    assert w_in.shape[2] == sum(widths.values()) - 2 * d_br
    if only is not None:
        widths = {k: w for k, w in widths.items() if k in only}
    dtypes = {k: (F32 if k in ("hlf", "hlb") else BF16) for k in widths}
    outs = pl.pallas_call(
        functools.partial(_in_proj_kernel, rope=rope, d_br=d_br, names=tuple(widths)),
        grid=(rows // tm,),
        in_specs=[
            row(d),
            pl.BlockSpec((1, 6, d), lambda i: (i // per, 0, 0)),
            _resident(norm_g.shape, layer),
            _resident(w_in.shape, layer),
            _resident(lb.shape, layer),
            _resident(v_g.shape, layer),
            _resident(v_b.shape, layer),
            tab, tab, tab,
        ],
        out_specs=[row(w) for w in widths.values()],
        out_shape=[jax.ShapeDtypeStruct((rows, w), dtypes[k]) for k, w in widths.items()],
        compiler_params=_params("parallel"),
        name="in_proj_rope" if rope else "in_proj_ctx",
    )(x2, mod, norm_g, w_in, lb, v_g, v_b, *tables)
    return dict(zip(widths, outs))


def _attn_kernel(sink_ref, q_ref, *refs, band):
    if band:
        kl_ref, km_ref, kr_ref, vl_ref, vm_ref, vr_ref, kc_ref, vc_ref, o_ref = refs
        j = pl.program_id(1)
        nq = pl.num_programs(1)
        qi = lax.broadcasted_iota(jnp.int32, (ATT_BLOCK, ATT_BLOCK), 0)
        kj = lax.broadcasted_iota(jnp.int32, (ATT_BLOCK, ATT_BLOCK), 1)
        left = jnp.where(jnp.logical_and(kj >= qi, j > 0), 0.0, NEG)
        right = jnp.where(jnp.logical_and(kj <= qi, j < nq - 1), 0.0, NEG)
    else:
        kc_ref, vc_ref, o_ref = refs
    tq = q_ref.shape[1]
    groups = q_ref.shape[2] // LANES
    rep = ATT_HEADS // ATT_KV_HEADS
    left_half = lax.broadcasted_iota(jnp.int32, (tq, LANES), 1) < HEAD_DIM
    top = lax.broadcasted_iota(jnp.int32, (2 * tq, 1), 0) < tq
    work = []
    for s in range(q_ref.shape[0]):
        if band:
            keys = jnp.concatenate([kl_ref[s], km_ref[s], kr_ref[s], kc_ref[s]], axis=0)
            vals = jnp.concatenate([vl_ref[s], vm_ref[s], vr_ref[s], vc_ref[s]], axis=0)
        else:
            keys, vals = kc_ref[s], vc_ref[s]
        for g in range(groups):
            qc = q_ref[s, :, g * LANES:(g + 1) * LANES]
            zero = jnp.zeros_like(qc)
            q2 = jnp.concatenate([jnp.where(left_half, qc, zero), jnp.where(left_half, zero, qc)], axis=0)
            work.append((s, g, _dot_nt(q2, keys), vals))
    for s, g, sc, vals in work:
        if band:
            two = lambda t: jnp.concatenate([t, t], axis=0)
            sc = jnp.concatenate([sc[:, :ATT_BLOCK] + two(left), sc[:, ATT_BLOCK:2 * ATT_BLOCK],
                                  sc[:, 2 * ATT_BLOCK:3 * ATT_BLOCK] + two(right), sc[:, 3 * ATT_BLOCK:]],
                                 axis=1)
        sink = jnp.where(top, sink_ref[g], sink_ref[g + rep]) * LOG2E
        m = jnp.maximum(jnp.max(sc, axis=-1, keepdims=True), sink)
        e = jnp.exp2(sc - m)
        den = jnp.sum(e, axis=-1, keepdims=True) + jnp.exp2(sink - m)
        o = _dot(e.astype(BF16), vals) * (1.0 / den)
        o_ref[s, :, g * LANES:(g + 1) * LANES] = jnp.where(left_half, o[:tq], o[tq:]).astype(BF16)


def _attention(q, k, v, kc, vc, sink, *, batch, seq, ctx_len, band):
    d_q = q.shape[1]
    tq = ATT_BLOCK
    nq = seq // tq
    ns = math.gcd(batch, ATT_SAMPLES)
    smem = pl.BlockSpec(memory_space=pltpu.SMEM)
    qspec = pl.BlockSpec((ns, tq, d_q), lambda b, j: (b, j, 0))
    cspec = pl.BlockSpec((ns, ctx_len, LANES), lambda b, j: (b, 0, 0))
    per_sample = lambda t, n: t.reshape(batch, n, t.shape[1])
    kc, vc = per_sample(kc, ctx_len), per_sample(vc, ctx_len)
    if band:
        kvspecs = [
            pl.BlockSpec((ns, tq, LANES), lambda b, j: (b, jnp.maximum(j - 1, 0), 0)),
            pl.BlockSpec((ns, tq, LANES), lambda b, j: (b, j, 0)),
            pl.BlockSpec((ns, tq, LANES), lambda b, j: (b, jnp.minimum(j + 1, nq - 1), 0)),
        ]
        in_specs = [smem, qspec] + kvspecs + kvspecs + [cspec, cspec]
        k, v = per_sample(k, seq), per_sample(v, seq)
        args = (sink, per_sample(q, seq), k, k, k, v, v, v, kc, vc)
    else:
        in_specs = [smem, qspec, cspec, cspec]
        args = (sink, per_sample(q, seq), kc, vc)
    return pl.pallas_call(
        functools.partial(_attn_kernel, band=band),
        grid=(batch // ns, nq),
        in_specs=in_specs,
        out_specs=qspec,
        out_shape=jax.ShapeDtypeStruct((batch, seq, d_q), BF16),
        compiler_params=_params("parallel", "parallel"),
        name="attn_band" if band else "attn_ctx",
    )(*args).reshape(batch * seq, d_q)


HG_SAMPLES = 4
HG_LEVELS = tuple(HG_CHUNK >> i for i in range(1, HG_CHUNK.bit_length()))


def _hgrn_tables():
    c = HG_CHUNK
    t = np.arange(c)[:, None]
    s = np.arange(c)[None, :]
    mats, lvls = [], []
    for rev in (False, True):
        lvl = np.full((c, c), -1, np.int32)
        lvl[np.arange(c), np.arange(c)] = 0
        for idx, b in enumerate(HG_LEVELS):
            same = (t // (2 * b)) == (s // (2 * b))
            t_hi, s_hi = (t % (2 * b)) >= b, (s % (2 * b)) >= b
            lvl[same & ((~t_hi & s_hi) if rev else (t_hi & ~s_hi))] = idx + 1
        mat = ((s >= t) if rev else (s <= t)).astype(np.float32)
        mats.append(np.concatenate([mat, mat], axis=1))
        lvls.append(lvl)
    return np.stack(mats), np.stack(lvls)


def _boundary_exponent(g, g_ref, lg, lanes, b, rev):
    c = g.shape[0]
    row = lax.broadcasted_iota(jnp.int32, (c, 1), 0)
    bound = lambda g0: g0 + b - (0 if rev else 1)
    if b == 1:
        return jnp.where(((row & 1) != 0) != rev, lg, 0.0)
    if b >= 8:
        pieces = []
        for g0 in range(0, c, 2 * b):
            gr = g_ref[bound(g0):bound(g0) + 1, lanes]
            first, second = g[g0:g0 + b], g[g0 + b:g0 + 2 * b]
            pieces += [first - gr, gr - second] if rev else [gr - first, second - gr]
        return jnp.concatenate(pieces, axis=0)
    sub = 8
    pieces = []
    for r0 in range(0, c, sub):
        refs = [jnp.broadcast_to(g_ref[bound(g0):bound(g0) + 1, lanes], (sub, LANES))
                for g0 in range(r0, r0 + sub, 2 * b)]
        gr = refs[0]
        for n in range(1, len(refs)):
            gr = jnp.where((row[:sub] & (sub - 1)) >= n * 2 * b, refs[n], gr)
        pieces.append(gr)
    return -jnp.abs(g - jnp.concatenate(pieces, axis=0))


def _mixed_rows(q, k, b, rev):
    c = q.shape[0]
    if b >= 8:
        pieces = []
        for g0 in range(0, c, 2 * b):
            lo, hi = slice(g0, g0 + b), slice(g0 + b, g0 + 2 * b)
            pieces += [q[lo], k[hi]] if rev else [k[lo], q[hi]]
        return jnp.concatenate(pieces, axis=0)
    later = (lax.broadcasted_iota(jnp.int32, (c, 1), 0) & b) != 0
    return jnp.where(later, k, q) if rev else jnp.where(later, q, k)


def _hgrn_scores(q, k, lg, g_ref, lanes, lvl, rev):
    c = HG_CHUNK
    g = g_ref[:, lanes]
    qf, kf = q.astype(F32), k.astype(F32)
    blocks = [(q, k)]
    for b in HG_LEVELS:
        e = jnp.exp2(_boundary_exponent(g, g_ref, lg, lanes, b, rev))
        y = (_mixed_rows(qf, kf, b, rev) * e).astype(BF16)
        blocks.append((y, y))
    zero = jnp.zeros_like(k)
    att = jnp.zeros((c, c), F32)
    for i in range(0, len(blocks), 2):
        (la, ra), (lb, rb) = blocks[i], blocks[i + 1]
        rhs = jnp.concatenate([jnp.concatenate([ra, zero], axis=1),
                               jnp.concatenate([zero, rb], axis=1)], axis=0)
        s = _dot_nt(jnp.concatenate([la, lb], axis=1), rhs)
        att = jnp.where(lvl == i, s[:, :c], jnp.where(lvl == i + 1, s[:, c:], att))
    return att.astype(BF16)


def _hgrn_apply(q, k, v, att, g_ref, lanes, state, rev):
    lhs = jnp.concatenate([(q.astype(F32) * jnp.exp2(g_ref[:, lanes])).astype(BF16), att], axis=1)
    rhs = jnp.concatenate([state.astype(BF16), v], axis=0)
    return _dot(lhs, rhs), _hgrn_next_state(k, v, g_ref, lanes, state, rev)


def _hgrn_next_state(k, v, g_ref, lanes, state, rev):
    g = g_ref[:, lanes]
    end = 0 if rev else HG_CHUNK - 1
    g_end = g_ref[end:end + 1, lanes]
    dec_t = jnp.broadcast_to(jnp.exp2(g_end), (HG_DK, LANES)).T
    return dec_t * state + _dot_tn((k.astype(F32) * jnp.exp2(g_end - g)).astype(BF16), v)


def _hgrn_cumsum(lg, m):
    hi = lg.astype(BF16)
    lo = (lg - hi.astype(F32)).astype(BF16)
    return _dot(m, jnp.concatenate([hi, lo], axis=0))


def _hgrn_state_kernel(kf_ref, lf_ref, vf_ref, kb_ref, lb_ref, vb_ref, s0_ref, m_ref,
                       sfin_ref, state_ref, g_ref):
    i = pl.program_id(1)

    @pl.when(i == 0)
    def _():
        state_ref[...] = s0_ref[...]

    for s in range(state_ref.shape[0]):
        for d, (k_ref, l_ref, v_ref) in enumerate(((kf_ref, lf_ref, vf_ref), (kb_ref, lb_ref, vb_ref))):
            g_ref[s, d] = _hgrn_cumsum(l_ref[s], m_ref[d])
            for h in range(HG_HEADS):
                lanes = slice(h * LANES, (h + 1) * LANES)
                state_ref[s, d, h] = _hgrn_next_state(k_ref[s, :, lanes], v_ref[s, :, lanes],
                                                      g_ref.at[s, d], lanes, state_ref[s, d, h], bool(d))

    @pl.when(i == pl.num_programs(1) - 1)
    def _():
        sfin_ref[...] = state_ref[...]


def _hgrn_kernel(qf_ref, kf_ref, lf_ref, vf_ref, qb_ref, kb_ref, lb_ref, vb_ref, s0_ref, m_ref, lvl_ref,
                 of_ref, ob_ref, sfin_ref, state_ref, g_ref):
    i = pl.program_id(1)

    @pl.when(i == 0)
    def _():
        state_ref[...] = s0_ref[...]

    dirs = ((qf_ref, kf_ref, lf_ref, vf_ref, of_ref), (qb_ref, kb_ref, lb_ref, vb_ref, ob_ref))
    work = []
    for s in range(state_ref.shape[0]):
        for d, (q_ref, k_ref, l_ref, v_ref, o_ref) in enumerate(dirs):
            lg = l_ref[s]
            g_ref[s, d] = _hgrn_cumsum(lg, m_ref[d])
            for h in range(HG_HEADS):
                lanes = slice(h * LANES, (h + 1) * LANES)
                work.append((s, d, h, lanes, o_ref, q_ref[s, :, lanes], k_ref[s, :, lanes],
                             v_ref[s, :, lanes], lg[:, lanes], state_ref[s, d, h]))
    atts = [_hgrn_scores(q, k, lg, g_ref.at[s, d], lanes, lvl_ref[d], bool(d))
            for s, d, h, lanes, o_ref, q, k, v, lg, state in work]
    done = []
    for att, (s, d, h, lanes, o_ref, q, k, v, lg, state) in zip(atts, work):
        out, new_state = _hgrn_apply(q, k, v, att, g_ref.at[s, d], lanes, state, bool(d))
        done.append((s, d, h, lanes, o_ref, out, new_state))
    for s, d, h, lanes, o_ref, out, new_state in done:
        state_ref[s, d, h] = new_state
        o_ref[s, :, lanes] = out.astype(o_ref.dtype)

    @pl.when(i == pl.num_programs(1) - 1)
    def _():
        sfin_ref[...] = state_ref[...]


def _hgrn(p, s0, *, batch, seq, state_only=False):
    c = HG_CHUNK
    n = seq // c
    ns = math.gcd(batch, HG_SAMPLES)
    width = p["hv"].shape[1]
    mats, lvls = _hgrn_tables()
    fwd = pl.BlockSpec((ns, c, width), lambda b, i: (b, i, 0))
    bwd = pl.BlockSpec((ns, c, width), lambda b, i: (b, n - 1 - i, 0))
    st = pl.BlockSpec((ns, 2, HG_HEADS, HG_DK, LANES), lambda b, i: (b, 0, 0, 0, 0))
    scratch = [pltpu.VMEM((ns, 2, HG_HEADS, HG_DK, LANES), F32),
               pltpu.VMEM((ns, 2, mats.shape[1], width), F32)]
    arg = lambda name: p[name].reshape(batch, seq, width)
    if state_only:
        return pl.pallas_call(
            _hgrn_state_kernel,
            grid=(batch // ns, n),
            in_specs=[fwd, fwd, fwd, bwd, bwd, bwd, st, _resident(mats.shape)],
            out_specs=st,
            out_shape=jax.ShapeDtypeStruct(s0.shape, F32),
            scratch_shapes=scratch,
            compiler_params=_params("parallel", "arbitrary"),
            name="hgrn_state",
        )(arg("hkf"), arg("hlf"), arg("hv"), arg("hkb"), arg("hlb"), arg("hv"), s0, jnp.asarray(mats, BF16))
    of, ob, s_fin = pl.pallas_call(
        _hgrn_kernel,
        grid=(batch // ns, n),
        in_specs=[fwd, fwd, fwd, fwd, bwd, bwd, bwd, bwd, st, _resident(mats.shape), _resident(lvls.shape)],
        out_specs=[fwd, bwd, st],
        out_shape=[jax.ShapeDtypeStruct((batch, seq, width), BF16),
                   jax.ShapeDtypeStruct((batch, seq, width), BF16),
                   jax.ShapeDtypeStruct(s0.shape, F32)],
        scratch_shapes=scratch,
        compiler_params=_params("parallel", "arbitrary"),
        name="hgrn_scan",
    )(arg("hq"), arg("hkf"), arg("hlf"), arg("hv"), arg("hq"), arg("hkb"), arg("hlb"), arg("hv"), s0,
      jnp.asarray(mats, BF16), jnp.asarray(lvls))
    return of.reshape(batch * seq, width), ob.reshape(batch * seq, width), s_fin


def _merge_kernel(x_ref, mod_ref, a_ref, of_ref, ob_ref, hg_ref, mu_ref, mvn_ref,
                  g_ref, wgt_ref, hgn_ref, ws_ref, bs_ref, wbr_ref, wo_ref, o_ref):
    tm, d = x_ref.shape
    cw = MXU_COLS
    for r0 in range(0, tm, MERGE_ROWS):
        rows = pl.ds(r0, MERGE_ROWS)
        h = _rms_mod(x_ref[rows, :], g_ref[...], mod_ref[0, 1:2, :], mod_ref[0, 0:1, :]).astype(BF16)
        o = of_ref[rows, :].astype(F32) + ob_ref[rows, :].astype(F32)
        parts = []
        for head in range(HG_HEADS):
            oh = o[:, head * LANES:(head + 1) * LANES]
            parts.append(oh * lax.rsqrt(jnp.mean(oh * oh, axis=-1, keepdims=True) + EPS))
        b_br = (jnp.concatenate(parts, axis=1) * hgn_ref[...] * hg_ref[rows, :].astype(F32)).astype(BF16)
        gated = []
        for r in range(r0, r0 + MERGE_ROWS, MLP_CHUNK):
            cols = []
            for grp in range(MLP_GROUPS):
                cs = slice(grp * LANES, (grp + 1) * LANES)
                cols.append(_dot(ws_ref[grp], mvn_ref[pl.ds(r, MLP_CHUNK), cs]) + bs_ref[grp])
            gated.append(jnp.concatenate(cols, axis=1))
        c_br = (mu_ref[rows, :].astype(F32) * jnp.concatenate(gated, axis=0)).astype(BF16)
        branches = (a_ref[rows, :], b_br, c_br)
        merged = []
        for c in range(0, d, cw):
            m = jnp.zeros((MERGE_ROWS, cw), F32)
            for n, br in enumerate(branches):
                gate = _sigmoid(_dot(h, wgt_ref[:, n * d + c:n * d + c + cw]))
                m = m + gate * _dot(br, wbr_ref[n, :, c:c + cw])
            merged.append(m.astype(BF16))
        merged = jnp.concatenate(merged, axis=1)
        for c in range(0, d, cw):
            y = _dot(merged, wo_ref[:, c:c + cw])
            o_ref[rows, c:c + cw] = x_ref[rows, c:c + cw] + mod_ref[0, 2:3, c:c + cw] * y


def _merge(x2, mod, a, of, ob, p, norm_g, w_gates, hg_g, ws, bs, w_br, w_o, *, layer, seq, tm):
    rows, d = x2.shape
    d_br = d // 2
    per = seq // tm
    row = lambda w: pl.BlockSpec((tm, w), lambda i: (i, 0))
    weights = (norm_g, w_gates, hg_g, ws, bs, w_br, w_o)
    return pl.pallas_call(
        _merge_kernel,
        grid=(rows // tm,),
        in_specs=[
            row(d),
            pl.BlockSpec((1, 6, d), lambda i: (i // per, 0, 0)),
            row(d_br), row(d_br), row(d_br), row(d_br), row(d_br), row(d_br),
        ] + [_resident(w.shape, layer) for w in weights],
        out_specs=row(d),
        out_shape=jax.ShapeDtypeStruct((rows, d), F32),
        compiler_params=_params("parallel"),
        name="merge",
    )(x2, mod, a, of, ob, p["hg"], p["mu"], p["mvn"], *weights)


def _ffn_kernel(x_ref, mod_ref, g_ref, wg_ref, wu_ref, wd_ref, fg_ref, o_ref, act_ref, *, final, bf):
    for rows in (pl.ds(r, FFN_ROWS) for r in range(0, x_ref.shape[0], FFN_ROWS)):
        x = x_ref[rows, :]
        h = _rms_mod(x, g_ref[...], mod_ref[0, 4:5, :], mod_ref[0, 3:4, :]).astype(BF16)
        for c in range(wg_ref.shape[1] // bf):
            sl = slice(c * bf, (c + 1) * bf)
            act_ref[rows, sl] = (_silu(_dot(h, wg_ref[:, sl])) * _dot(h, wu_ref[:, sl])).astype(BF16)
        y = x + mod_ref[0, 5:6, :] * _dot(act_ref[rows, :], wd_ref[...])
        if final:
            y = y * lax.rsqrt(jnp.mean(y * y, axis=-1, keepdims=True) + EPS) * fg_ref[...]
        o_ref[rows, :] = y


def _ffn(x2, mod, norm_g, w_gate, w_up, w_down, final_g, *, layer, seq, tm, final):
    rows, d = x2.shape
    d_ff = w_gate.shape[2]
    per = seq // tm
    row = pl.BlockSpec((tm, d), lambda i: (i, 0))
    return pl.pallas_call(
        functools.partial(_ffn_kernel, final=final, bf=MXU_COLS),
        grid=(rows // tm,),
        in_specs=[
            row,
            pl.BlockSpec((1, 6, d), lambda i: (i // per, 0, 0)),
            _resident(norm_g.shape, layer),
            _resident(w_gate.shape, layer),
            _resident(w_up.shape, layer),
            _resident(w_down.shape, layer),
            _resident((1, d)),
        ],
        out_specs=row,
        out_shape=jax.ShapeDtypeStruct((rows, d), F32),
        scratch_shapes=[pltpu.VMEM((tm, d_ff), BF16)],
        compiler_params=_params("parallel"),
        name="ffn_final" if final else "ffn",
    )(x2, mod, norm_g, w_gate, w_up, w_down, final_g)


def _rope_tables(seq):
    pos = jnp.arange(seq)
    nf = HEAD_DIM // 4
    inv = ROPE_BASE ** (-jnp.arange(nf, dtype=F32) / nf)
    ang_r = (pos // GRID_W).astype(F32)[:, None] * inv[None, :]
    ang_c = (pos % GRID_W).astype(F32)[:, None] * inv[None, :]
    zero = jnp.zeros_like(ang_r)
    cos = jnp.concatenate([jnp.cos(ang_r)] * 2 + [jnp.cos(ang_c)] * 2, axis=1)
    sa = jnp.concatenate([zero, jnp.sin(ang_r), zero, jnp.sin(ang_c)], axis=1)
    sb = jnp.concatenate([-jnp.sin(ang_r), zero, -jnp.sin(ang_c), zero], axis=1)
    reps = LANES // HEAD_DIM
    return tuple(jnp.tile(t, (1, reps)) for t in (cos, sa, sb))


def kernel(x, c, ctx, c_ctx, w_ada, b_ada, norm1_g, norm2_g, w_in, attn_sink, hg_lb_logits, hg_norm_g, mlp_v_norm_g, mlp_v_norm_b, mlp_ws, mlp_bs, w_branch, w_out, w_ffn_gate, w_ffn_up, w_ffn_down, final_norm_g):
    batch, seq, d = x.shape
    ctx_len = ctx.shape[1]
    depth = w_ada.shape[0]
    d_br = d // 2
    tm_x, tm_c = min(ROW_TILE, seq), min(ROW_TILE // 4, ctx_len)
    assert batch + 1 <= MOD_ROWS and seq % tm_x == 0 and ctx_len % tm_c == 0 and tm_c % MERGE_ROWS == 0

    lb = jnp.cumsum(jax.nn.softmax(hg_lb_logits.astype(F32), axis=0), axis=0)
    lb = lb - lb[0]
    cs = jnp.concatenate([c, c_ctx[None, :], jnp.zeros((MOD_ROWS - batch - 1, d), F32)], axis=0)
    mod = _ada(cs, w_ada, b_ada).reshape(depth, MOD_ROWS, 6, d)
    tables = _rope_tables(seq)
    zero_state = jnp.zeros((batch, 2, HG_HEADS, HG_DK, LANES), F32)

    def reorder_heads(t, axis):
        rep = ATT_HEADS // ATT_KV_HEADS
        split = t.shape[:axis] + (ATT_KV_HEADS, rep, HEAD_DIM) + t.shape[axis + 1:]
        return jnp.swapaxes(t.reshape(split), axis, axis + 1).reshape(t.shape)

    d_gate = N_BRANCH * d
    w_proj = jnp.concatenate([reorder_heads(w_in[:, :, :d_br], 2), w_in[:, :, d_br:-d_gate]],
                             axis=2).astype(BF16)
    w_gates = w_in[:, :, -d_gate:].astype(BF16)
    w_br = jnp.concatenate([reorder_heads(w_branch[:, :1], 2), w_branch[:, 1:]], axis=1).astype(BF16)
    ws = mlp_ws.astype(BF16)
    bs = jnp.broadcast_to(mlp_bs[:, :, :, None], mlp_bs.shape + (LANES,))
    w_o, wg, wu, wd = (w.astype(BF16) for w in (w_out, w_ffn_gate, w_ffn_up, w_ffn_down))
    n1, n2, vg, vb, hgn = (t[:, None, :] for t in (norm1_g, norm2_g, mlp_v_norm_g, mlp_v_norm_b, hg_norm_g))
    fg = final_norm_g[None, :]

    xs = x.reshape(batch * seq, d)
    cx = ctx.reshape(batch * ctx_len, d)
    for l in range(depth):
        need_ctx = l < depth - 1
        mod_x = mod[l, :batch]
        mod_c = jnp.broadcast_to(mod[l, batch:batch + 1], (batch, 6, d))
        pc = _in_proj(cx, mod_c, n1, w_proj, lb, vg, vb, tables, layer=l, seq=ctx_len, tm=tm_c, rope=False,
                      only=None if need_ctx else STATE_ONLY_OUTPUTS)
        px = _in_proj(xs, mod_x, n1, w_proj, lb, vg, vb, tables, layer=l, seq=seq, tm=tm_x // 2, rope=True)
        a = _attention(px["qa"], px["ka"], px["va"], pc["ka"], pc["va"], attn_sink[l],
                       batch=batch, seq=seq, ctx_len=ctx_len, band=True)
        if need_ctx:
            ocf, ocb, s_ctx = _hgrn(pc, zero_state, batch=batch, seq=ctx_len)
        else:
            s_ctx = _hgrn(pc, zero_state, batch=batch, seq=ctx_len, state_only=True)
        of, ob, _ = _hgrn(px, s_ctx, batch=batch, seq=seq)
        xs = _merge(xs, mod_x, a, of, ob, px, n1, w_gates, hgn, ws, bs, w_br, w_o,
                    layer=l, seq=seq, tm=tm_x)
        xs = _ffn(xs, mod_x, n2, wg, wu, wd, fg, layer=l, seq=seq, tm=tm_x, final=not need_ctx)
        if need_ctx:
            ac = _attention(pc["qa"], None, None, pc["ka"], pc["va"], attn_sink[l],
                            batch=batch, seq=ctx_len, ctx_len=ctx_len, band=False)
            cx = _merge(cx, mod_c, ac, ocf, ocb, pc, n1, w_gates, hgn, ws, bs, w_br, w_o,
                        layer=l, seq=ctx_len, tm=tm_c)
            cx = _ffn(cx, mod_c, n2, wg, wu, wd, fg, layer=l, seq=ctx_len, tm=tm_c, final=False)
    return xs.reshape(batch, seq, d)
```

```python
import functools
import math

import jax
import jax.numpy as jnp
import numpy as np
from jax import lax
from jax.experimental import pallas as pl
from jax.experimental.pallas import tpu as pltpu

F32 = jnp.float32
BF16 = jnp.bfloat16

EPS = 1e-6
GRID_W = 64
HEAD_DIM = 64
ATT_HEADS = 8
ATT_KV_HEADS = 2
WINDOW = 128
ATT_BLOCK = 128
ROPE_BASE = 10000.0
HG_DK = 128
HG_HEADS = 4
HG_CHUNK = 128
MLP_CHUNK = 128
MLP_GROUPS = 4
N_BRANCH = 3

LANES = 128
MXU_COLS = 256
IN_PROJ_ROWS = 128
FFN_ROWS = 128
MERGE_ROWS = 256
ATT_SAMPLES = 4
ROW_TILE = 1024
MOD_ROWS = 24
VMEM_LIMIT = 56 * 1024 * 1024
NEG = -1e30
LOG2E = 1.4426950408889634

Q_HEAD_ORDER = (0, 4, 1, 5, 2, 6, 3, 7)


def _sigmoid(x):
    return 0.5 * jnp.tanh(0.5 * x) + 0.5


def _silu(x):
    return x * _sigmoid(x)


def _rms_mod(x, g, sc, sh):
    y = x * lax.rsqrt(jnp.mean(x * x, axis=-1, keepdims=True) + EPS)
    return (y * g) * (1.0 + sc) + sh


def _dot(a, b):
    return jnp.dot(a, b, preferred_element_type=F32)


def _dot_nt(a, b):
    return lax.dot_general(a, b, (((1,), (1,)), ((), ())), preferred_element_type=F32)


def _dot_tn(a, b):
    return lax.dot_general(a, b, (((0,), (0,)), ((), ())), preferred_element_type=F32)


def _resident(shape, layer=None):
    if layer is None:
        return pl.BlockSpec(shape, lambda *_: (0,) * len(shape), pipeline_mode=pl.Buffered(1))
    rest = tuple(shape[1:])
    return pl.BlockSpec((None,) + rest, lambda *_: (layer,) + (0,) * len(rest),
                        pipeline_mode=pl.Buffered(1))


def _params(*sem):
    return pltpu.CompilerParams(dimension_semantics=sem, vmem_limit_bytes=VMEM_LIMIT)


def _ada_kernel(c_ref, w_ref, b_ref, o_ref):
    s = _silu(c_ref[...])
    o_ref[...] = jnp.dot(s, w_ref[...], precision=lax.Precision.HIGHEST,
                         preferred_element_type=F32) + b_ref[...]


def _ada(cs, w_ada, b_ada):
    depth, d, d6 = w_ada.shape
    bn = 1024
    return pl.pallas_call(
        _ada_kernel,
        grid=(depth, d6 // bn),
        in_specs=[
            pl.BlockSpec((MOD_ROWS, d), lambda l, j: (0, 0)),
            pl.BlockSpec((None, d, bn), lambda l, j: (l, 0, j)),
            pl.BlockSpec((None, 1, bn), lambda l, j: (l, 0, j)),
        ],
        out_specs=pl.BlockSpec((None, MOD_ROWS, bn), lambda l, j: (l, 0, j)),
        out_shape=jax.ShapeDtypeStruct((depth, MOD_ROWS, d6), F32),
        compiler_params=_params("parallel", "parallel"),
        name="ada",
    )(cs, w_ada, b_ada.reshape(depth, 1, d6))


def _rope(x, cos, sa, sb):
    return x * cos + pltpu.roll(x, 16, 1) * sa + pltpu.roll(x, LANES - 16, 1) * sb


def _forget_terms(z, lb):
    z2 = z * LOG2E
    u = jnp.exp2(-jnp.abs(z2))
    one_u = 1.0 + u
    log_sig = jnp.minimum(z2, 0.0) - jnp.log2(one_u)
    la = jnp.maximum(jnp.log2(lb), NEG)
    b = jnp.log1p(-lb) * LOG2E + log_sig
    logf2 = jnp.maximum(la, b) + jnp.log2(1.0 + jnp.exp2(-jnp.abs(la - b)))
    r = 1.0 / one_u
    k = (1.0 - lb) * jnp.where(z >= 0.0, u * r, r)
    return logf2, k


def _in_proj_kernel(x_ref, mod_ref, g_ref, w_ref, lb_ref, vg_ref, vb_ref, cos_ref, sa_ref, sb_ref,
                    *out_refs, rope, d_br, names):
    out = dict(zip(names, out_refs))
    cw = MXU_COLS
    off_q, off_kv, off_hq = 0, d_br, d_br + cw
    off_f, off_b, off_v, off_g, off_u, off_cv = (off_hq + n * d_br for n in range(1, 7))
    for rows in (pl.ds(r, IN_PROJ_ROWS) for r in range(0, x_ref.shape[0], IN_PROJ_ROWS)):
        h = _rms_mod(x_ref[rows, :], g_ref[...], mod_ref[0, 1:2, :], mod_ref[0, 0:1, :]).astype(BF16)

        def proj(lo):
            return _dot(h, w_ref[:, lo:lo + cw])

        def maybe_rope(t):
            if not rope:
                return t
            return _rope(t, cos_ref[rows, :], sa_ref[rows, :], sb_ref[rows, :])

        def each(name, lo, fn):
            if name in out:
                for c in range(0, out[name].shape[1], cw):
                    out[name][rows, c:c + cw] = fn(proj(lo + c)).astype(out[name].dtype)

        def rope_lanes(acc):
            parts = [maybe_rope(acc[:, s:s + LANES]) for s in range(0, cw, LANES)]
            return jnp.concatenate(parts, axis=1) * (HEAD_DIM ** -0.5 * LOG2E)

        each("qa", off_q, rope_lanes)
        acc = proj(off_kv)
        out["ka"][rows, :] = maybe_rope(acc[:, :LANES]).astype(BF16)
        out["va"][rows, :] = acc[:, LANES:].astype(BF16)
        each("hq", off_hq, lambda a: _silu(a) * HG_DK ** -0.5)
        plain = iter([(name, o + c, c) for name, o in (("hv", off_v), ("mu", off_u)) if name in out
                      for c in range(0, d_br, cw)])
        for which, (l_name, k_name, lo) in enumerate((("hlf", "hkf", off_f), ("hlb", "hkb", off_b))):
            for c in range(0, d_br, cw):
                logf2, k = _forget_terms(proj(lo + c), lb_ref[which:which + 1, c:c + cw])
                out[l_name][rows, c:c + cw] = logf2
                out[k_name][rows, c:c + cw] = k.astype(BF16)
                filler = next(plain, None)
                if filler is not None:
                    p_name, p_lo, p_c = filler
                    out[p_name][rows, p_c:p_c + cw] = proj(p_lo).astype(BF16)
        each("hg", off_g, lambda a: _silu(a))
        if "mvn" in out:
            v = jnp.concatenate([proj(off_cv + c) for c in range(0, d_br, cw)], axis=1)
            mean = jnp.mean(v, axis=-1, keepdims=True)
            var = jnp.mean(jnp.square(v - mean), axis=-1, keepdims=True)
            out["mvn"][rows, :] = ((v - mean) * lax.rsqrt(var + EPS) * vg_ref[...]
                                   + vb_ref[...]).astype(BF16)


STATE_ONLY_OUTPUTS = ("ka", "va", "hkf", "hkb", "hlf", "hlb", "hv")


def _in_proj(x2, mod, norm_g, w_in, lb, v_g, v_b, tables, *, layer, seq, tm, rope, only=None):
    rows, d = x2.shape
    d_br = d // 2
    per = seq // tm
    row = lambda w: pl.BlockSpec((tm, w), lambda i: (i, 0))
    tab = pl.BlockSpec((tm, LANES), lambda i: (i % per, 0))
    widths = dict(qa=d_br, ka=LANES, va=LANES, hq=d_br, hkf=d_br, hkb=d_br, hlf=d_br, hlb=d_br,
                  hv=d_br, hg=d_br, mu=d_br, mvn=d_br)
    assert w_in.shape[2] == sum(widths.values()) - 2 * d_br
    if only is not None:
        widths = {k: w for k, w in widths.items() if k in only}
    dtypes = {k: (F32 if k in ("hlf", "hlb") else BF16) for k in widths}
    outs = pl.pallas_call(
        functools.partial(_in_proj_kernel, rope=rope, d_br=d_br, names=tuple(widths)),
        grid=(rows // tm,),
        in_specs=[
            row(d),
            pl.BlockSpec((1, 6, d), lambda i: (i // per, 0, 0)),
            _resident(norm_g.shape, layer),
            _resident(w_in.shape, layer),
            _resident(lb.shape, layer),
            _resident(v_g.shape, layer),
            _resident(v_b.shape, layer),
            tab, tab, tab,
        ],
        out_specs=[row(w) for w in widths.values()],
        out_shape=[jax.ShapeDtypeStruct((rows, w), dtypes[k]) for k, w in widths.items()],
        compiler_params=_params("parallel"),
        name="in_proj_rope" if rope else "in_proj_ctx",
    )(x2, mod, norm_g, w_in, lb, v_g, v_b, *tables)
    return dict(zip(widths, outs))


def _attn_kernel(sink_ref, q_ref, *refs, band):
    if band:
        kl_ref, km_ref, kr_ref, vl_ref, vm_ref, vr_ref, kc_ref, vc_ref, o_ref = refs
        j = pl.program_id(1)
        nq = pl.num_programs(1)
        qi = lax.broadcasted_iota(jnp.int32, (ATT_BLOCK, ATT_BLOCK), 0)
        kj = lax.broadcasted_iota(jnp.int32, (ATT_BLOCK, ATT_BLOCK), 1)
        left = jnp.where(jnp.logical_and(kj >= qi, j > 0), 0.0, NEG)
        right = jnp.where(jnp.logical_and(kj <= qi, j < nq - 1), 0.0, NEG)
    else:
        kc_ref, vc_ref, o_ref = refs
    tq = q_ref.shape[1]
    groups = q_ref.shape[2] // LANES
    rep = ATT_HEADS // ATT_KV_HEADS
    left_half = lax.broadcasted_iota(jnp.int32, (tq, LANES), 1) < HEAD_DIM
    top = lax.broadcasted_iota(jnp.int32, (2 * tq, 1), 0) < tq
    work = []
    for s in range(q_ref.shape[0]):
        if band:
            keys = jnp.concatenate([kl_ref[s], km_ref[s], kr_ref[s], kc_ref[s]], axis=0)
            vals = jnp.concatenate([vl_ref[s], vm_ref[s], vr_ref[s], vc_ref[s]], axis=0)
        else:
            keys, vals = kc_ref[s], vc_ref[s]
        keys_t = keys.astype(F32).T.astype(BF16)
        for g in range(groups):
            qc = q_ref[s, :, g * LANES:(g + 1) * LANES]
            zero = jnp.zeros_like(qc)
            q2 = jnp.concatenate([jnp.where(left_half, qc, zero), jnp.where(left_half, zero, qc)], axis=0)
            work.append((s, g, _dot(q2, keys_t), vals))
    for s, g, sc, vals in work:
        if band:
            two = lambda t: jnp.concatenate([t, t], axis=0)
            sc = jnp.concatenate([sc[:, :ATT_BLOCK] + two(left), sc[:, ATT_BLOCK:2 * ATT_BLOCK],
                                  sc[:, 2 * ATT_BLOCK:3 * ATT_BLOCK] + two(right), sc[:, 3 * ATT_BLOCK:]],
                                 axis=1)
        sink = jnp.where(top, sink_ref[g], sink_ref[g + rep]) * LOG2E
        m = jnp.maximum(jnp.max(sc, axis=-1, keepdims=True), sink)
        e = jnp.exp2(sc - m)
        den = jnp.sum(e, axis=-1, keepdims=True) + jnp.exp2(sink - m)
        o = _dot(e.astype(BF16), vals) * (1.0 / den)
        o_ref[s, :, g * LANES:(g + 1) * LANES] = jnp.where(left_half, o[:tq], o[tq:]).astype(BF16)


def _attention(q, k, v, kc, vc, sink, *, batch, seq, ctx_len, band):
    d_q = q.shape[1]
    tq = ATT_BLOCK
    nq = seq // tq
    ns = math.gcd(batch, ATT_SAMPLES)
    smem = pl.BlockSpec(memory_space=pltpu.SMEM)
    qspec = pl.BlockSpec((ns, tq, d_q), lambda b, j: (b, j, 0))
    cspec = pl.BlockSpec((ns, ctx_len, LANES), lambda b, j: (b, 0, 0))
    per_sample = lambda t, n: t.reshape(batch, n, t.shape[1])
    kc, vc = per_sample(kc, ctx_len), per_sample(vc, ctx_len)
    if band:
        kvspecs = [
            pl.BlockSpec((ns, tq, LANES), lambda b, j: (b, jnp.maximum(j - 1, 0), 0)),
            pl.BlockSpec((ns, tq, LANES), lambda b, j: (b, j, 0)),
            pl.BlockSpec((ns, tq, LANES), lambda b, j: (b, jnp.minimum(j + 1, nq - 1), 0)),
        ]
        in_specs = [smem, qspec] + kvspecs + kvspecs + [cspec, cspec]
        k, v = per_sample(k, seq), per_sample(v, seq)
        args = (sink, per_sample(q, seq), k, k, k, v, v, v, kc, vc)
    else:
        in_specs = [smem, qspec, cspec, cspec]
        args = (sink, per_sample(q, seq), kc, vc)
    return pl.pallas_call(
        functools.partial(_attn_kernel, band=band),
        grid=(batch // ns, nq),
        in_specs=in_specs,
        out_specs=qspec,
        out_shape=jax.ShapeDtypeStruct((batch, seq, d_q), BF16),
        compiler_params=_params("parallel", "parallel"),
        name="attn_band" if band else "attn_ctx",
    )(*args).reshape(batch * seq, d_q)


HG_SAMPLES = 4
HG_LEVELS = tuple(HG_CHUNK >> i for i in range(1, HG_CHUNK.bit_length()))


def _hgrn_tables():
    c = HG_CHUNK
    t = np.arange(c)[:, None]
    s = np.arange(c)[None, :]
    mats, lvls = [], []
    for rev in (False, True):
        lvl = np.full((c, c), -1, np.int32)
        lvl[np.arange(c), np.arange(c)] = 0
        for idx, b in enumerate(HG_LEVELS):
            same = (t // (2 * b)) == (s // (2 * b))
            t_hi, s_hi = (t % (2 * b)) >= b, (s % (2 * b)) >= b
            lvl[same & ((~t_hi & s_hi) if rev else (t_hi & ~s_hi))] = idx + 1
        mat = ((s >= t) if rev else (s <= t)).astype(np.float32)
        mats.append(np.concatenate([mat, mat], axis=1))
        lvls.append(lvl)
    return np.stack(mats), np.stack(lvls)


def _boundary_exponent(g, g_ref, lg, lanes, b, rev):
    c = g.shape[0]
    row = lax.broadcasted_iota(jnp.int32, (c, 1), 0)
    bound = lambda g0: g0 + b - (0 if rev else 1)
    if b == 1:
        return jnp.where(((row & 1) != 0) != rev, lg, 0.0)
    if b >= 8:
        pieces = []
        for g0 in range(0, c, 2 * b):
            gr = g_ref[bound(g0):bound(g0) + 1, lanes]
            first, second = g[g0:g0 + b], g[g0 + b:g0 + 2 * b]
            pieces += [first - gr, gr - second] if rev else [gr - first, second - gr]
        return jnp.concatenate(pieces, axis=0)
    sub = 8
    pieces = []
    for r0 in range(0, c, sub):
        refs = [jnp.broadcast_to(g_ref[bound(g0):bound(g0) + 1, lanes], (sub, LANES))
                for g0 in range(r0, r0 + sub, 2 * b)]
        gr = refs[0]
        for n in range(1, len(refs)):
            gr = jnp.where((row[:sub] & (sub - 1)) >= n * 2 * b, refs[n], gr)
        pieces.append(gr)
    return -jnp.abs(g - jnp.concatenate(pieces, axis=0))


def _mixed_rows(q, k, b, rev):
    c = q.shape[0]
    if b >= 8:
        pieces = []
        for g0 in range(0, c, 2 * b):
            lo, hi = slice(g0, g0 + b), slice(g0 + b, g0 + 2 * b)
            pieces += [q[lo], k[hi]] if rev else [k[lo], q[hi]]
        return jnp.concatenate(pieces, axis=0)
    later = (lax.broadcasted_iota(jnp.int32, (c, 1), 0) & b) != 0
    return jnp.where(later, k, q) if rev else jnp.where(later, q, k)


def _hgrn_scores(q, k, lg, g_ref, lanes, lvl, rev):
    c = HG_CHUNK
    g = g_ref[:, lanes]
    qf, kf = q.astype(F32), k.astype(F32)
    blocks = [(q, kf.T.astype(BF16))]
    for b in HG_LEVELS:
        e = jnp.exp2(_boundary_exponent(g, g_ref, lg, lanes, b, rev))
        y = _mixed_rows(qf, kf, b, rev) * e
        blocks.append((y.astype(BF16), y.T.astype(BF16)))
    zero = jnp.zeros((LANES, c), BF16)
    att = jnp.zeros((c, c), F32)
    for i in range(0, len(blocks), 2):
        (la, ra), (lb, rb) = blocks[i], blocks[i + 1]
        rhs = jnp.concatenate([jnp.concatenate([ra, zero], axis=1),
                               jnp.concatenate([zero, rb], axis=1)], axis=0)
        s = _dot(jnp.concatenate([la, lb], axis=1), rhs)
        att = jnp.where(lvl == i, s[:, :c], jnp.where(lvl == i + 1, s[:, c:], att))
    return att.astype(BF16)


def _hgrn_apply(q, k, v, att, g_ref, lanes, state, rev):
    lhs = jnp.concatenate([(q.astype(F32) * jnp.exp2(g_ref[:, lanes])).astype(BF16), att], axis=1)
    rhs = jnp.concatenate([state.astype(BF16), v], axis=0)
    return _dot(lhs, rhs), _hgrn_next_state(k, v, g_ref, lanes, state, rev)


def _hgrn_next_state(k, v, g_ref, lanes, state, rev):
    g = g_ref[:, lanes]
    end = 0 if rev else HG_CHUNK - 1
    g_end = g_ref[end:end + 1, lanes]
    dec_t = jnp.broadcast_to(jnp.exp2(g_end), (HG_DK, LANES)).T
    return dec_t * state + _dot_tn((k.astype(F32) * jnp.exp2(g_end - g)).astype(BF16), v)


def _hgrn_cumsum(lg, m):
    hi = lg.astype(BF16)
    lo = (lg - hi.astype(F32)).astype(BF16)
    return _dot(m, jnp.concatenate([hi, lo], axis=0))


def _hgrn_state_kernel(kf_ref, lf_ref, vf_ref, kb_ref, lb_ref, vb_ref, s0_ref, m_ref,
                       sfin_ref, state_ref, g_ref):
    i = pl.program_id(1)

    @pl.when(i == 0)
    def _():
        state_ref[...] = s0_ref[...]

    for s in range(state_ref.shape[0]):
        for d, (k_ref, l_ref, v_ref) in enumerate(((kf_ref, lf_ref, vf_ref), (kb_ref, lb_ref, vb_ref))):
            g_ref[s, d] = _hgrn_cumsum(l_ref[s], m_ref[d])
            for h in range(HG_HEADS):
                lanes = slice(h * LANES, (h + 1) * LANES)
                state_ref[s, d, h] = _hgrn_next_state(k_ref[s, :, lanes], v_ref[s, :, lanes],
                                                      g_ref.at[s, d], lanes, state_ref[s, d, h], bool(d))

    @pl.when(i == pl.num_programs(1) - 1)
    def _():
        sfin_ref[...] = state_ref[...]


def _hgrn_kernel(qf_ref, kf_ref, lf_ref, vf_ref, qb_ref, kb_ref, lb_ref, vb_ref, s0_ref, m_ref, lvl_ref,
                 of_ref, ob_ref, sfin_ref, state_ref, g_ref):
    i = pl.program_id(1)

    @pl.when(i == 0)
    def _():
        state_ref[...] = s0_ref[...]

    dirs = ((qf_ref, kf_ref, lf_ref, vf_ref, of_ref), (qb_ref, kb_ref, lb_ref, vb_ref, ob_ref))
    work = []
    for s in range(state_ref.shape[0]):
        for d, (q_ref, k_ref, l_ref, v_ref, o_ref) in enumerate(dirs):
            lg = l_ref[s]
            g_ref[s, d] = _hgrn_cumsum(lg, m_ref[d])
            for h in range(HG_HEADS):
                lanes = slice(h * LANES, (h + 1) * LANES)
                work.append((s, d, h, lanes, o_ref, q_ref[s, :, lanes], k_ref[s, :, lanes],
                             v_ref[s, :, lanes], lg[:, lanes], state_ref[s, d, h]))
    atts = [_hgrn_scores(q, k, lg, g_ref.at[s, d], lanes, lvl_ref[d], bool(d))
            for s, d, h, lanes, o_ref, q, k, v, lg, state in work]
    done = []
    for att, (s, d, h, lanes, o_ref, q, k, v, lg, state) in zip(atts, work):
        out, new_state = _hgrn_apply(q, k, v, att, g_ref.at[s, d], lanes, state, bool(d))
        done.append((s, d, h, lanes, o_ref, out, new_state))
    for s, d, h, lanes, o_ref, out, new_state in done:
        state_ref[s, d, h] = new_state
        o_ref[s, :, lanes] = out.astype(o_ref.dtype)

    @pl.when(i == pl.num_programs(1) - 1)
    def _():
        sfin_ref[...] = state_ref[...]


def _hgrn(p, s0, *, batch, seq, state_only=False):
    c = HG_CHUNK
    n = seq // c
    ns = math.gcd(batch, HG_SAMPLES)
    width = p["hv"].shape[1]
    mats, lvls = _hgrn_tables()
    fwd = pl.BlockSpec((ns, c, width), lambda b, i: (b, i, 0))
    bwd = pl.BlockSpec((ns, c, width), lambda b, i: (b, n - 1 - i, 0))
    st = pl.BlockSpec((ns, 2, HG_HEADS, HG_DK, LANES), lambda b, i: (b, 0, 0, 0, 0))
    scratch = [pltpu.VMEM((ns, 2, HG_HEADS, HG_DK, LANES), F32),
               pltpu.VMEM((ns, 2, mats.shape[1], width), F32)]
    arg = lambda name: p[name].reshape(batch, seq, width)
    if state_only:
        return pl.pallas_call(
            _hgrn_state_kernel,
            grid=(batch // ns, n),
            in_specs=[fwd, fwd, fwd, bwd, bwd, bwd, st, _resident(mats.shape)],
            out_specs=st,
            out_shape=jax.ShapeDtypeStruct(s0.shape, F32),
            scratch_shapes=scratch,
            compiler_params=_params("parallel", "arbitrary"),
            name="hgrn_state",
        )(arg("hkf"), arg("hlf"), arg("hv"), arg("hkb"), arg("hlb"), arg("hv"), s0, jnp.asarray(mats, BF16))
    of, ob, s_fin = pl.pallas_call(
        _hgrn_kernel,
        grid=(batch // ns, n),
        in_specs=[fwd, fwd, fwd, fwd, bwd, bwd, bwd, bwd, st, _resident(mats.shape), _resident(lvls.shape)],
        out_specs=[fwd, bwd, st],
        out_shape=[jax.ShapeDtypeStruct((batch, seq, width), BF16),
                   jax.ShapeDtypeStruct((batch, seq, width), BF16),
                   jax.ShapeDtypeStruct(s0.shape, F32)],
        scratch_shapes=scratch,
        compiler_params=_params("parallel", "arbitrary"),
        name="hgrn_scan",
    )(arg("hq"), arg("hkf"), arg("hlf"), arg("hv"), arg("hq"), arg("hkb"), arg("hlb"), arg("hv"), s0,
      jnp.asarray(mats, BF16), jnp.asarray(lvls))
    return of.reshape(batch * seq, width), ob.reshape(batch * seq, width), s_fin


def _merge_kernel(x_ref, mod_ref, a_ref, of_ref, ob_ref, hg_ref, mu_ref, mvn_ref,
                  g_ref, wgt_ref, hgn_ref, ws_ref, bs_ref, wbr_ref, wo_ref, o_ref):
    tm, d = x_ref.shape
    cw = MXU_COLS
    for r0 in range(0, tm, MERGE_ROWS):
        rows = pl.ds(r0, MERGE_ROWS)
        h = _rms_mod(x_ref[rows, :], g_ref[...], mod_ref[0, 1:2, :], mod_ref[0, 0:1, :]).astype(BF16)
        o = of_ref[rows, :].astype(F32) + ob_ref[rows, :].astype(F32)
        parts = []
        for head in range(HG_HEADS):
            oh = o[:, head * LANES:(head + 1) * LANES]
            parts.append(oh * lax.rsqrt(jnp.mean(oh * oh, axis=-1, keepdims=True) + EPS))
        b_br = (jnp.concatenate(parts, axis=1) * hgn_ref[...] * hg_ref[rows, :].astype(F32)).astype(BF16)
        gated = []
        for r in range(r0, r0 + MERGE_ROWS, MLP_CHUNK):
            cols = []
            for grp in range(MLP_GROUPS):
                cs = slice(grp * LANES, (grp + 1) * LANES)
                cols.append(_dot(ws_ref[grp], mvn_ref[pl.ds(r, MLP_CHUNK), cs]) + bs_ref[grp])
            gated.append(jnp.concatenate(cols, axis=1))
        c_br = (mu_ref[rows, :].astype(F32) * jnp.concatenate(gated, axis=0)).astype(BF16)
        branches = (a_ref[rows, :], b_br, c_br)
        merged = []
        for c in range(0, d, cw):
            m = jnp.zeros((MERGE_ROWS, cw), F32)
            for n, br in enumerate(branches):
                gate = _sigmoid(_dot(h, wgt_ref[:, n * d + c:n * d + c + cw]))
                m = m + gate * _dot(br, wbr_ref[n, :, c:c + cw])
            merged.append(m.astype(BF16))
        merged = jnp.concatenate(merged, axis=1)
        for c in range(0, d, cw):
            y = _dot(merged, wo_ref[:, c:c + cw])
            o_ref[rows, c:c + cw] = x_ref[rows, c:c + cw] + mod_ref[0, 2:3, c:c + cw] * y


def _merge(x2, mod, a, of, ob, p, norm_g, w_gates, hg_g, ws, bs, w_br, w_o, *, layer, seq, tm):
    rows, d = x2.shape
    d_br = d // 2
    per = seq // tm
    row = lambda w: pl.BlockSpec((tm, w), lambda i: (i, 0))
    weights = (norm_g, w_gates, hg_g, ws, bs, w_br, w_o)
    return pl.pallas_call(
        _merge_kernel,
        grid=(rows // tm,),
        in_specs=[
            row(d),
            pl.BlockSpec((1, 6, d), lambda i: (i // per, 0, 0)),
            row(d_br), row(d_br), row(d_br), row(d_br), row(d_br), row(d_br),
        ] + [_resident(w.shape, layer) for w in weights],
        out_specs=row(d),
        out_shape=jax.ShapeDtypeStruct((rows, d), F32),
        compiler_params=_params("parallel"),
        name="merge",
    )(x2, mod, a, of, ob, p["hg"], p["mu"], p["mvn"], *weights)


def _ffn_kernel(x_ref, mod_ref, g_ref, wg_ref, wu_ref, wd_ref, fg_ref, o_ref, act_ref, *, final, bf):
    for rows in (pl.ds(r, FFN_ROWS) for r in range(0, x_ref.shape[0], FFN_ROWS)):
        x = x_ref[rows, :]
        h = _rms_mod(x, g_ref[...], mod_ref[0, 4:5, :], mod_ref[0, 3:4, :]).astype(BF16)
        for c in range(wg_ref.shape[1] // bf):
            sl = slice(c * bf, (c + 1) * bf)
            act_ref[rows, sl] = (_silu(_dot(h, wg_ref[:, sl])) * _dot(h, wu_ref[:, sl])).astype(BF16)
        y = x + mod_ref[0, 5:6, :] * _dot(act_ref[rows, :], wd_ref[...])
        if final:
            y = y * lax.rsqrt(jnp.mean(y * y, axis=-1, keepdims=True) + EPS) * fg_ref[...]
        o_ref[rows, :] = y


def _ffn(x2, mod, norm_g, w_gate, w_up, w_down, final_g, *, layer, seq, tm, final):
    rows, d = x2.shape
    d_ff = w_gate.shape[2]
    per = seq // tm
    row = pl.BlockSpec((tm, d), lambda i: (i, 0))
    return pl.pallas_call(
        functools.partial(_ffn_kernel, final=final, bf=MXU_COLS),
        grid=(rows // tm,),
        in_specs=[
            row,
            pl.BlockSpec((1, 6, d), lambda i: (i // per, 0, 0)),
            _resident(norm_g.shape, layer),
            _resident(w_gate.shape, layer),
            _resident(w_up.shape, layer),
            _resident(w_down.shape, layer),
            _resident((1, d)),
        ],
        out_specs=row,
        out_shape=jax.ShapeDtypeStruct((rows, d), F32),
        scratch_shapes=[pltpu.VMEM((tm, d_ff), BF16)],
        compiler_params=_params("parallel"),
        name="ffn_final" if final else "ffn",
    )(x2, mod, norm_g, w_gate, w_up, w_down, final_g)


def _rope_tables(seq):
    pos = jnp.arange(seq)
    nf = HEAD_DIM // 4
    inv = ROPE_BASE ** (-jnp.arange(nf, dtype=F32) / nf)
    ang_r = (pos // GRID_W).astype(F32)[:, None] * inv[None, :]
    ang_c = (pos % GRID_W).astype(F32)[:, None] * inv[None, :]
    zero = jnp.zeros_like(ang_r)
    cos = jnp.concatenate([jnp.cos(ang_r)] * 2 + [jnp.cos(ang_c)] * 2, axis=1)
    sa = jnp.concatenate([zero, jnp.sin(ang_r), zero, jnp.sin(ang_c)], axis=1)
    sb = jnp.concatenate([-jnp.sin(ang_r), zero, -jnp.sin(ang_c), zero], axis=1)
    reps = LANES // HEAD_DIM
    return tuple(jnp.tile(t, (1, reps)) for t in (cos, sa, sb))


def kernel(x, c, ctx, c_ctx, w_ada, b_ada, norm1_g, norm2_g, w_in, attn_sink, hg_lb_logits, hg_norm_g, mlp_v_norm_g, mlp_v_norm_b, mlp_ws, mlp_bs, w_branch, w_out, w_ffn_gate, w_ffn_up, w_ffn_down, final_norm_g):
    batch, seq, d = x.shape
    ctx_len = ctx.shape[1]
    depth = w_ada.shape[0]
    d_br = d // 2
    tm_x, tm_c = min(ROW_TILE, seq), min(ROW_TILE // 4, ctx_len)
    assert batch + 1 <= MOD_ROWS and seq % tm_x == 0 and ctx_len % tm_c == 0 and tm_c % MERGE_ROWS == 0

    lb = jnp.cumsum(jax.nn.softmax(hg_lb_logits.astype(F32), axis=0), axis=0)
    lb = lb - lb[0]
    cs = jnp.concatenate([c, c_ctx[None, :], jnp.zeros((MOD_ROWS - batch - 1, d), F32)], axis=0)
    mod = _ada(cs, w_ada, b_ada).reshape(depth, MOD_ROWS, 6, d)
    tables = _rope_tables(seq)
    zero_state = jnp.zeros((batch, 2, HG_HEADS, HG_DK, LANES), F32)

    def reorder_heads(t, axis):
        rep = ATT_HEADS // ATT_KV_HEADS
        split = t.shape[:axis] + (ATT_KV_HEADS, rep, HEAD_DIM) + t.shape[axis + 1:]
        return jnp.swapaxes(t.reshape(split), axis, axis + 1).reshape(t.shape)

    d_gate = N_BRANCH * d
    w_proj = jnp.concatenate([reorder_heads(w_in[:, :, :d_br], 2), w_in[:, :, d_br:-d_gate]],
                             axis=2).astype(BF16)
    w_gates = w_in[:, :, -d_gate:].astype(BF16)
    w_br = jnp.concatenate([reorder_heads(w_branch[:, :1], 2), w_branch[:, 1:]], axis=1).astype(BF16)
    ws = mlp_ws.astype(BF16)
    bs = jnp.broadcast_to(mlp_bs[:, :, :, None], mlp_bs.shape + (LANES,))
    w_o, wg, wu, wd = (w.astype(BF16) for w in (w_out, w_ffn_gate, w_ffn_up, w_ffn_down))
    n1, n2, vg, vb, hgn = (t[:, None, :] for t in (norm1_g, norm2_g, mlp_v_norm_g, mlp_v_norm_b, hg_norm_g))
    fg = final_norm_g[None, :]

    xs = x.reshape(batch * seq, d)
    cx = ctx.reshape(batch * ctx_len, d)
    for l in range(depth):
        need_ctx = l < depth - 1
        mod_x = mod[l, :batch]
        mod_c = jnp.broadcast_to(mod[l, batch:batch + 1], (batch, 6, d))
        pc = _in_proj(cx, mod_c, n1, w_proj, lb, vg, vb, tables, layer=l, seq=ctx_len, tm=tm_c, rope=False,
                      only=None if need_ctx else STATE_ONLY_OUTPUTS)
        px = _in_proj(xs, mod_x, n1, w_proj, lb, vg, vb, tables, layer=l, seq=seq, tm=tm_x // 2, rope=True)
        a = _attention(px["qa"], px["ka"], px["va"], pc["ka"], pc["va"], attn_sink[l],
                       batch=batch, seq=seq, ctx_len=ctx_len, band=True)
        if need_ctx:
            ocf, ocb, s_ctx = _hgrn(pc, zero_state, batch=batch, seq=ctx_len)
        else:
            s_ctx = _hgrn(pc, zero_state, batch=batch, seq=ctx_len, state_only=True)
        of, ob, _ = _hgrn(px, s_ctx, batch=batch, seq=seq)
        xs = _merge(xs, mod_x, a, of, ob, px, n1, w_gates, hgn, ws, bs, w_br, w_o,
                    layer=l, seq=seq, tm=tm_x)
        xs = _ffn(xs, mod_x, n2, wg, wu, wd, fg, layer=l, seq=seq, tm=tm_x, final=not need_ctx)
        if need_ctx:
            ac = _attention(pc["qa"], None, None, pc["ka"], pc["va"], attn_sink[l],
                            batch=batch, seq=ctx_len, ctx_len=ctx_len, band=False)
            cx = _merge(cx, mod_c, ac, ocf, ocb, pc, n1, w_gates, hgn, ws, bs, w_br, w_o,
                        layer=l, seq=ctx_len, tm=tm_c)
            cx = _ffn(cx, mod_c, n2, wg, wu, wd, fg, layer=l, seq=ctx_len, tm=tm_c, final=False)
    return xs.reshape(batch, seq, d)
```

```python
import functools
import math

import jax
import jax.numpy as jnp
import numpy as np
from jax import lax
from jax.experimental import pallas as pl
from jax.experimental.pallas import tpu as pltpu

F32 = jnp.float32
BF16 = jnp.bfloat16

EPS = 1e-6
GRID_W = 64
HEAD_DIM = 64
ATT_HEADS = 8
ATT_KV_HEADS = 2
WINDOW = 128
ATT_BLOCK = 128
ROPE_BASE = 10000.0
HG_DK = 128
HG_HEADS = 4
HG_CHUNK = 128
MLP_CHUNK = 128
MLP_GROUPS = 4
N_BRANCH = 3

LANES = 128
MXU_COLS = 256
IN_PROJ_ROWS = 128
FFN_ROWS = 128
MERGE_ROWS = 256
ATT_SAMPLES = 4
ROW_TILE = 1024
MOD_ROWS = 24
VMEM_LIMIT = 56 * 1024 * 1024
NEG = -1e30
LOG2E = 1.4426950408889634

assert WINDOW == ATT_BLOCK


def _sigmoid(x):
    return 0.5 * jnp.tanh(0.5 * x) + 0.5


def _silu(x):
    return x * _sigmoid(x)


def _rms_mod(x, g, sc, sh):
    y = x * lax.rsqrt(jnp.mean(x * x, axis=-1, keepdims=True) + EPS)
    return (y * g) * (1.0 + sc) + sh


def _dot(a, b):
    return jnp.dot(a, b, preferred_element_type=F32)


def _dot_nt(a, b):
    return lax.dot_general(a, b, (((1,), (1,)), ((), ())), preferred_element_type=F32)


def _dot_tn(a, b):
    return lax.dot_general(a, b, (((0,), (0,)), ((), ())), preferred_element_type=F32)


def _resident(shape, layer=None):
    if layer is None:
        return pl.BlockSpec(shape, lambda *_: (0,) * len(shape), pipeline_mode=pl.Buffered(1))
    rest = tuple(shape[1:])
    return pl.BlockSpec((None,) + rest, lambda *_: (layer,) + (0,) * len(rest),
                        pipeline_mode=pl.Buffered(1))


def _params(*sem):
    return pltpu.CompilerParams(dimension_semantics=sem, vmem_limit_bytes=VMEM_LIMIT)


def _ada_kernel(c_ref, w_ref, b_ref, o_ref):
    s = _silu(c_ref[...])
    o_ref[...] = jnp.dot(s, w_ref[...], precision=lax.Precision.HIGHEST,
                         preferred_element_type=F32) + b_ref[...]


def _ada(cs, w_ada, b_ada):
    depth, d, d6 = w_ada.shape
    bn = d
    return pl.pallas_call(
        _ada_kernel,
        grid=(depth, d6 // bn),
        in_specs=[
            pl.BlockSpec((MOD_ROWS, d), lambda l, j: (0, 0)),
            pl.BlockSpec((None, d, bn), lambda l, j: (l, 0, j)),
            pl.BlockSpec((None, 1, bn), lambda l, j: (l, 0, j)),
        ],
        out_specs=pl.BlockSpec((None, MOD_ROWS, bn), lambda l, j: (l, 0, j)),
        out_shape=jax.ShapeDtypeStruct((depth, MOD_ROWS, d6), F32),
        compiler_params=_params("parallel", "parallel"),
        name="ada",
    )(cs, w_ada, b_ada.reshape(depth, 1, d6))


def _rope(x, cos, sa, sb):
    return x * cos + pltpu.roll(x, 16, 1) * sa + pltpu.roll(x, LANES - 16, 1) * sb


def _forget_terms(z, lb):
    z2 = z * LOG2E
    u = jnp.exp2(-jnp.abs(z2))
    one_u = 1.0 + u
    log_sig = jnp.minimum(z2, 0.0) - jnp.log2(one_u)
    la = jnp.maximum(jnp.log2(lb), NEG)
    b = jnp.log1p(-lb) * LOG2E + log_sig
    logf2 = jnp.maximum(la, b) + jnp.log2(1.0 + jnp.exp2(-jnp.abs(la - b)))
    r = 1.0 / one_u
    k = (1.0 - lb) * jnp.where(z >= 0.0, u * r, r)
    return logf2, k


def _in_proj_kernel(x_ref, mod_ref, g_ref, w_ref, lb_ref, vg_ref, vb_ref, cos_ref, sa_ref, sb_ref,
                    *out_refs, rope, d_br, names):
    out = dict(zip(names, out_refs))
    cw = MXU_COLS
    off_q, off_kv, off_hq = 0, d_br, d_br + cw
    off_f, off_b, off_v, off_g, off_u, off_cv = (off_hq + n * d_br for n in range(1, 7))
    for rows in (pl.ds(r, IN_PROJ_ROWS) for r in range(0, x_ref.shape[0], IN_PROJ_ROWS)):
        h = _rms_mod(x_ref[rows, :], g_ref[...], mod_ref[0, 1:2, :], mod_ref[0, 0:1, :]).astype(BF16)

        def proj(lo):
            return _dot(h, w_ref[:, lo:lo + cw])

        def maybe_rope(t):
            if not rope:
                return t
            return _rope(t, cos_ref[rows, :], sa_ref[rows, :], sb_ref[rows, :])

        def each(name, lo, fn):
            if name in out:
                for c in range(0, out[name].shape[1], cw):
                    out[name][rows, c:c + cw] = fn(proj(lo + c)).astype(out[name].dtype)

        def rope_lanes(acc):
            parts = [maybe_rope(acc[:, s:s + LANES]) for s in range(0, cw, LANES)]
            return jnp.concatenate(parts, axis=1) * (HEAD_DIM ** -0.5 * LOG2E)

        each("qa", off_q, rope_lanes)
        acc = proj(off_kv)
        out["ka"][rows, :] = maybe_rope(acc[:, :LANES]).astype(BF16)
        out["va"][rows, :] = acc[:, LANES:].astype(BF16)
        each("hq", off_hq, lambda a: _silu(a) * HG_DK ** -0.5)
        plain = iter([(name, o + c, c) for name, o in (("hv", off_v), ("mu", off_u)) if name in out
                      for c in range(0, d_br, cw)])
        for which, (l_name, k_name, lo) in enumerate((("hlf", "hkf", off_f), ("hlb", "hkb", off_b))):
            for c in range(0, d_br, cw):
                logf2, k = _forget_terms(proj(lo + c), lb_ref[which:which + 1, c:c + cw])
                out[l_name][rows, c:c + cw] = logf2
                out[k_name][rows, c:c + cw] = k.astype(BF16)
                filler = next(plain, None)
                if filler is not None:
                    p_name, p_lo, p_c = filler
                    out[p_name][rows, p_c:p_c + cw] = proj(p_lo).astype(BF16)
        each("hg", off_g, lambda a: _silu(a))
        if "mvn" in out:
            v = jnp.concatenate([proj(off_cv + c) for c in range(0, d_br, cw)], axis=1)
            mean = jnp.mean(v, axis=-1, keepdims=True)
            var = jnp.mean(jnp.square(v - mean), axis=-1, keepdims=True)
            out["mvn"][rows, :] = ((v - mean) * lax.rsqrt(var + EPS) * vg_ref[...]
                                   + vb_ref[...]).astype(BF16)


STATE_ONLY_OUTPUTS = ("ka", "va", "hkf", "hkb", "hlf", "hlb", "hv")


def _in_proj(x2, mod, norm_g, w_in, lb, v_g, v_b, tables, *, layer, seq, tm, rope, only=None):
    rows, d = x2.shape
    d_br = d // 2
    per = seq // tm
    row = lambda w: pl.BlockSpec((tm, w), lambda i: (i, 0))
    tab = pl.BlockSpec((tm, LANES), lambda i: (i % per, 0))
    widths = dict(qa=d_br, ka=LANES, va=LANES, hq=d_br, hkf=d_br, hkb=d_br, hlf=d_br, hlb=d_br,
                  hv=d_br, hg=d_br, mu=d_br, mvn=d_br)
    assert w_in.shape[2] == sum(widths.values()) - 2 * d_br
    if only is not None:
        widths = {k: w for k, w in widths.items() if k in only}
    dtypes = {k: (F32 if k in ("hlf", "hlb") else BF16) for k in widths}
    outs = pl.pallas_call(
        functools.partial(_in_proj_kernel, rope=rope, d_br=d_br, names=tuple(widths)),
        grid=(rows // tm,),
        in_specs=[
            row(d),
            pl.BlockSpec((1, 6, d), lambda i: (i // per, 0, 0)),
            _resident(norm_g.shape, layer),
            _resident(w_in.shape, layer),
            _resident(lb.shape, layer),
            _resident(v_g.shape, layer),
            _resident(v_b.shape, layer),
            tab, tab, tab,
        ],
        out_specs=[row(w) for w in widths.values()],
        out_shape=[jax.ShapeDtypeStruct((rows, w), dtypes[k]) for k, w in widths.items()],
        compiler_params=_params("parallel"),
        name="in_proj_rope" if rope else "in_proj_ctx",
    )(x2, mod, norm_g, w_in, lb, v_g, v_b, *tables)
    return dict(zip(widths, outs))


def _attn_kernel(sink_ref, q_ref, *refs, band):
    if band:
        kl_ref, km_ref, kr_ref, vl_ref, vm_ref, vr_ref, kc_ref, vc_ref, o_ref = refs
        j = pl.program_id(1)
        nq = pl.num_programs(1)
        qi = lax.broadcasted_iota(jnp.int32, (ATT_BLOCK, ATT_BLOCK), 0)
        kj = lax.broadcasted_iota(jnp.int32, (ATT_BLOCK, ATT_BLOCK), 1)
        left = jnp.where(jnp.logical_and(kj >= qi, j > 0), 0.0, NEG)
        right = jnp.where(jnp.logical_and(kj <= qi, j < nq - 1), 0.0, NEG)
    else:
        kc_ref, vc_ref, o_ref = refs
    tq = q_ref.shape[1]
    groups = q_ref.shape[2] // LANES
    rep = ATT_HEADS // ATT_KV_HEADS
    left_half = lax.broadcasted_iota(jnp.int32, (tq, LANES), 1) < HEAD_DIM
    top = lax.broadcasted_iota(jnp.int32, (2 * tq, 1), 0) < tq
    work = []
    for s in range(q_ref.shape[0]):
        if band:
            keys = jnp.concatenate([kl_ref[s], km_ref[s], kr_ref[s], kc_ref[s]], axis=0)
            vals = jnp.concatenate([vl_ref[s], vm_ref[s], vr_ref[s], vc_ref[s]], axis=0)
        else:
            keys, vals = kc_ref[s], vc_ref[s]
        for g in range(groups):
            qc = q_ref[s, :, g * LANES:(g + 1) * LANES]
            zero = jnp.zeros_like(qc)
            q2 = jnp.concatenate([jnp.where(left_half, qc, zero), jnp.where(left_half, zero, qc)], axis=0)
            work.append((s, g, _dot_nt(q2, keys), vals))
    for s, g, sc, vals in work:
        if band:
            two = lambda t: jnp.concatenate([t, t], axis=0)
            sc = jnp.concatenate([sc[:, :ATT_BLOCK] + two(left), sc[:, ATT_BLOCK:2 * ATT_BLOCK],
                                  sc[:, 2 * ATT_BLOCK:3 * ATT_BLOCK] + two(right), sc[:, 3 * ATT_BLOCK:]],
                                 axis=1)
        sink = jnp.where(top, sink_ref[g], sink_ref[g + rep]) * LOG2E
        m = jnp.maximum(jnp.max(sc, axis=-1, keepdims=True), sink)
        e = jnp.exp2(sc - m)
        den = jnp.sum(e, axis=-1, keepdims=True) + jnp.exp2(sink - m)
        o = _dot(e.astype(BF16), vals) * (1.0 / den)
        o_ref[s, :, g * LANES:(g + 1) * LANES] = jnp.where(left_half, o[:tq], o[tq:]).astype(BF16)


def _attention(q, k, v, kc, vc, sink, *, batch, seq, ctx_len, band):
    d_q = q.shape[1]
    tq = ATT_BLOCK
    nq = seq // tq
    ns = math.gcd(batch, ATT_SAMPLES)
    smem = pl.BlockSpec(memory_space=pltpu.SMEM)
    qspec = pl.BlockSpec((ns, tq, d_q), lambda b, j: (b, j, 0))
    cspec = pl.BlockSpec((ns, ctx_len, LANES), lambda b, j: (b, 0, 0))
    per_sample = lambda t, n: t.reshape(batch, n, t.shape[1])
    kc, vc = per_sample(kc, ctx_len), per_sample(vc, ctx_len)
    if band:
        kvspecs = [
            pl.BlockSpec((ns, tq, LANES), lambda b, j: (b, jnp.maximum(j - 1, 0), 0)),
            pl.BlockSpec((ns, tq, LANES), lambda b, j: (b, j, 0)),
            pl.BlockSpec((ns, tq, LANES), lambda b, j: (b, jnp.minimum(j + 1, nq - 1), 0)),
        ]
        in_specs = [smem, qspec] + kvspecs + kvspecs + [cspec, cspec]
        k, v = per_sample(k, seq), per_sample(v, seq)
        args = (sink, per_sample(q, seq), k, k, k, v, v, v, kc, vc)
    else:
        in_specs = [smem, qspec, cspec, cspec]
        args = (sink, per_sample(q, seq), kc, vc)
    return pl.pallas_call(
        functools.partial(_attn_kernel, band=band),
        grid=(batch // ns, nq),
        in_specs=in_specs,
        out_specs=qspec,
        out_shape=jax.ShapeDtypeStruct((batch, seq, d_q), BF16),
        compiler_params=_params("parallel", "parallel"),
        name="attn_band" if band else "attn_ctx",
    )(*args).reshape(batch * seq, d_q)


HG_SAMPLES = 4
HG_LEVELS = tuple(HG_CHUNK >> i for i in range(1, HG_CHUNK.bit_length()))


def _hgrn_tables():
    c = HG_CHUNK
    t = np.arange(c)[:, None]
    s = np.arange(c)[None, :]
    mats, lvls = [], []
    for rev in (False, True):
        lvl = np.full((c, c), -1, np.int32)
        lvl[np.arange(c), np.arange(c)] = 0
        for idx, b in enumerate(HG_LEVELS):
            same = (t // (2 * b)) == (s // (2 * b))
            t_hi, s_hi = (t % (2 * b)) >= b, (s % (2 * b)) >= b
            lvl[same & ((~t_hi & s_hi) if rev else (t_hi & ~s_hi))] = idx + 1
        mat = ((s >= t) if rev else (s <= t)).astype(np.float32)
        mats.append(np.concatenate([mat, mat], axis=1))
        n_bits = len(HG_LEVELS).bit_length()
        planes = [(np.maximum(lvl, 0) >> bit) & 1 for bit in range(n_bits)] + [lvl >= 0]
        lvls.append(np.stack(planes).astype(np.int32))
    return np.stack(mats), np.stack(lvls)


def _boundary_exponent(g, g_ref, lg, lanes, b, rev):
    c = g.shape[0]
    row = lax.broadcasted_iota(jnp.int32, (c, 1), 0)
    bound = lambda g0: g0 + b - (0 if rev else 1)
    if b == 1:
        return jnp.where(((row & 1) != 0) != rev, lg, 0.0)
    if b >= 8:
        pieces = []
        for g0 in range(0, c, 2 * b):
            gr = g_ref[bound(g0):bound(g0) + 1, lanes]
            first, second = g[g0:g0 + b], g[g0 + b:g0 + 2 * b]
            pieces += [first - gr, gr - second] if rev else [gr - first, second - gr]
        return jnp.concatenate(pieces, axis=0)
    sub = 8
    pieces = []
    for r0 in range(0, c, sub):
        refs = [jnp.broadcast_to(g_ref[bound(g0):bound(g0) + 1, lanes], (sub, LANES))
                for g0 in range(r0, r0 + sub, 2 * b)]
        gr = refs[0]
        for n in range(1, len(refs)):
            gr = jnp.where((row[:sub] & (sub - 1)) >= n * 2 * b, refs[n], gr)
        pieces.append(gr)
    return -jnp.abs(g - jnp.concatenate(pieces, axis=0))


def _mixed_rows(q, k, b, rev):
    c = q.shape[0]
    if b >= 8:
        pieces = []
        for g0 in range(0, c, 2 * b):
            lo, hi = slice(g0, g0 + b), slice(g0 + b, g0 + 2 * b)
            pieces += [q[lo], k[hi]] if rev else [k[lo], q[hi]]
        return jnp.concatenate(pieces, axis=0)
    later = (lax.broadcasted_iota(jnp.int32, (c, 1), 0) & b) != 0
    return jnp.where(later, k, q) if rev else jnp.where(later, q, k)


def _hgrn_scores(q, k, lg, g_ref, lanes, lvl, rev):
    c = HG_CHUNK
    g = g_ref[:, lanes]
    qf, kf = q.astype(F32), k.astype(F32)
    blocks = [(q, kf.T.astype(BF16))]
    for b in HG_LEVELS:
        e = jnp.exp2(_boundary_exponent(g, g_ref, lg, lanes, b, rev))
        y = _mixed_rows(qf, kf, b, rev) * e
        blocks.append((y.astype(BF16), y.T.astype(BF16)))
    zero = jnp.zeros((LANES, c), BF16)
    bits = [lvl[n] != 0 for n in range(lvl.shape[0] - 1)]
    picked = []
    for i in range(0, len(blocks), 2):
        (la, ra), (lb, rb) = blocks[i], blocks[i + 1]
        rhs = jnp.concatenate([jnp.concatenate([ra, zero], axis=1),
                               jnp.concatenate([zero, rb], axis=1)], axis=0)
        s = _dot(jnp.concatenate([la, lb], axis=1), rhs)
        picked.append(jnp.where(bits[0], s[:, c:], s[:, :c]))
    for bit in bits[1:]:
        picked = [jnp.where(bit, hi, lo) for lo, hi in zip(picked[0::2], picked[1::2])]
    return jnp.where(lvl[-1] != 0, picked[0], 0.0).astype(BF16)


def _hgrn_apply(q, k, v, att, g_ref, lanes, state, rev):
    lhs = jnp.concatenate([(q.astype(F32) * jnp.exp2(g_ref[:, lanes])).astype(BF16), att], axis=1)
    rhs = jnp.concatenate([state.astype(BF16), v], axis=0)
    return _dot(lhs, rhs), _hgrn_next_state(k, v, g_ref, lanes, state, rev)


def _hgrn_next_state(k, v, g_ref, lanes, state, rev):
    g = g_ref[:, lanes]
    end = 0 if rev else HG_CHUNK - 1
    g_end = g_ref[end:end + 1, lanes]
    dec_t = jnp.broadcast_to(jnp.exp2(g_end), (HG_DK, LANES)).T
    return dec_t * state + _dot_tn((k.astype(F32) * jnp.exp2(g_end - g)).astype(BF16), v)


def _hgrn_cumsum(lg, m):
    hi = lg.astype(BF16)
    lo = (lg - hi.astype(F32)).astype(BF16)
    return _dot(m, jnp.concatenate([hi, lo], axis=0))


def _hgrn_state_kernel(kf_ref, lf_ref, vf_ref, kb_ref, lb_ref, vb_ref, s0_ref, m_ref,
                       sfin_ref, state_ref, g_ref):
    i = pl.program_id(1)

    @pl.when(i == 0)
    def _():
        state_ref[...] = s0_ref[...]

    for s in range(state_ref.shape[0]):
        for d, (k_ref, l_ref, v_ref) in enumerate(((kf_ref, lf_ref, vf_ref), (kb_ref, lb_ref, vb_ref))):
            g_ref[s, d] = _hgrn_cumsum(l_ref[s], m_ref[d])
            for h in range(HG_HEADS):
                lanes = slice(h * LANES, (h + 1) * LANES)
                state_ref[s, d, h] = _hgrn_next_state(k_ref[s, :, lanes], v_ref[s, :, lanes],
                                                      g_ref.at[s, d], lanes, state_ref[s, d, h], bool(d))

    @pl.when(i == pl.num_programs(1) - 1)
    def _():
        sfin_ref[...] = state_ref[...]


def _hgrn_kernel(qf_ref, kf_ref, lf_ref, vf_ref, qb_ref, kb_ref, lb_ref, vb_ref, s0_ref, m_ref, lvl_ref,
                 of_ref, ob_ref, sfin_ref, state_ref, g_ref):
    i = pl.program_id(1)

    @pl.when(i == 0)
    def _():
        state_ref[...] = s0_ref[...]

    dirs = ((qf_ref, kf_ref, lf_ref, vf_ref, of_ref), (qb_ref, kb_ref, lb_ref, vb_ref, ob_ref))
    work = []
    for s in range(state_ref.shape[0]):
        for d, (q_ref, k_ref, l_ref, v_ref, o_ref) in enumerate(dirs):
            lg = l_ref[s]
            g_ref[s, d] = _hgrn_cumsum(lg, m_ref[d])
            for h in range(HG_HEADS):
                lanes = slice(h * LANES, (h + 1) * LANES)
                work.append((s, d, h, lanes, o_ref, q_ref[s, :, lanes], k_ref[s, :, lanes],
                             v_ref[s, :, lanes], lg[:, lanes], state_ref[s, d, h]))
    atts = [_hgrn_scores(q, k, lg, g_ref.at[s, d], lanes, lvl_ref[d], bool(d))
            for s, d, h, lanes, o_ref, q, k, v, lg, state in work]
    done = []
    for att, (s, d, h, lanes, o_ref, q, k, v, lg, state) in zip(atts, work):
        out, new_state = _hgrn_apply(q, k, v, att, g_ref.at[s, d], lanes, state, bool(d))
        done.append((s, d, h, lanes, o_ref, out, new_state))
    for s, d, h, lanes, o_ref, out, new_state in done:
        state_ref[s, d, h] = new_state
        o_ref[s, :, lanes] = out.astype(o_ref.dtype)

    @pl.when(i == pl.num_programs(1) - 1)
    def _():
        sfin_ref[...] = state_ref[...]


def _hgrn(p, s0, *, batch, seq, state_only=False):
    c = HG_CHUNK
    n = seq // c
    ns = math.gcd(batch, HG_SAMPLES)
    width = p["hv"].shape[1]
    mats, lvls = _hgrn_tables()
    fwd = pl.BlockSpec((ns, c, width), lambda b, i: (b, i, 0))
    bwd = pl.BlockSpec((ns, c, width), lambda b, i: (b, n - 1 - i, 0))
    st = pl.BlockSpec((ns, 2, HG_HEADS, HG_DK, LANES), lambda b, i: (b, 0, 0, 0, 0))
    scratch = [pltpu.VMEM((ns, 2, HG_HEADS, HG_DK, LANES), F32),
               pltpu.VMEM((ns, 2, mats.shape[1], width), F32)]
    arg = lambda name: p[name].reshape(batch, seq, width)
    if state_only:
        return pl.pallas_call(
            _hgrn_state_kernel,
            grid=(batch // ns, n),
            in_specs=[fwd, fwd, fwd, bwd, bwd, bwd, st, _resident(mats.shape)],
            out_specs=st,
            out_shape=jax.ShapeDtypeStruct(s0.shape, F32),
            scratch_shapes=scratch,
            compiler_params=_params("parallel", "arbitrary"),
            name="hgrn_state",
        )(arg("hkf"), arg("hlf"), arg("hv"), arg("hkb"), arg("hlb"), arg("hv"), s0, jnp.asarray(mats, BF16))
    of, ob, s_fin = pl.pallas_call(
        _hgrn_kernel,
        grid=(batch // ns, n),
        in_specs=[fwd, fwd, fwd, fwd, bwd, bwd, bwd, bwd, st, _resident(mats.shape), _resident(lvls.shape)],
        out_specs=[fwd, bwd, st],
        out_shape=[jax.ShapeDtypeStruct((batch, seq, width), BF16),
                   jax.ShapeDtypeStruct((batch, seq, width), BF16),
                   jax.ShapeDtypeStruct(s0.shape, F32)],
        scratch_shapes=scratch,
        compiler_params=_params("parallel", "arbitrary"),
        name="hgrn_scan",
    )(arg("hq"), arg("hkf"), arg("hlf"), arg("hv"), arg("hq"), arg("hkb"), arg("hlb"), arg("hv"), s0,
      jnp.asarray(mats, BF16), jnp.asarray(lvls))
    return of.reshape(batch * seq, width), ob.reshape(batch * seq, width), s_fin


def _merge_kernel(x_ref, mod_ref, a_ref, of_ref, ob_ref, hg_ref, mu_ref, mvn_ref,
                  g_ref, wgt_ref, hgn_ref, ws_ref, bs_ref, wbr_ref, wo_ref, o_ref):
    tm, d = x_ref.shape
    cw = MXU_COLS
    for r0 in range(0, tm, MERGE_ROWS):
        rows = pl.ds(r0, MERGE_ROWS)
        h = _rms_mod(x_ref[rows, :], g_ref[...], mod_ref[0, 1:2, :], mod_ref[0, 0:1, :]).astype(BF16)
        o = of_ref[rows, :].astype(F32) + ob_ref[rows, :].astype(F32)
        parts = []
        for head in range(HG_HEADS):
            oh = o[:, head * LANES:(head + 1) * LANES]
            parts.append(oh * lax.rsqrt(jnp.mean(oh * oh, axis=-1, keepdims=True) + EPS))
        b_br = (jnp.concatenate(parts, axis=1) * hgn_ref[...] * hg_ref[rows, :].astype(F32)).astype(BF16)
        gated = []
        for r in range(r0, r0 + MERGE_ROWS, MLP_CHUNK):
            cols = []
            for grp in range(MLP_GROUPS):
                cs = slice(grp * LANES, (grp + 1) * LANES)
                cols.append(_dot(ws_ref[grp], mvn_ref[pl.ds(r, MLP_CHUNK), cs]) + bs_ref[grp])
            gated.append(jnp.concatenate(cols, axis=1))
        c_br = (mu_ref[rows, :].astype(F32) * jnp.concatenate(gated, axis=0)).astype(BF16)
        branches = (a_ref[rows, :], b_br, c_br)
        merged = []
        for c in range(0, d, cw):
            m = jnp.zeros((MERGE_ROWS, cw), F32)
            for n, br in enumerate(branches):
                gate = _sigmoid(_dot(h, wgt_ref[:, n * d + c:n * d + c + cw]))
                m = m + gate * _dot(br, wbr_ref[n, :, c:c + cw])
            merged.append(m.astype(BF16))
        merged = jnp.concatenate(merged, axis=1)
        for c in range(0, d, cw):
            y = _dot(merged, wo_ref[:, c:c + cw])
            o_ref[rows, c:c + cw] = x_ref[rows, c:c + cw] + mod_ref[0, 2:3, c:c + cw] * y


def _merge(x2, mod, a, of, ob, p, norm_g, w_gates, hg_g, ws, bs, w_br, w_o, *, layer, seq, tm):
    rows, d = x2.shape
    d_br = d // 2
    per = seq // tm
    row = lambda w: pl.BlockSpec((tm, w), lambda i: (i, 0))
    weights = (norm_g, w_gates, hg_g, ws, bs, w_br, w_o)
    return pl.pallas_call(
        _merge_kernel,
        grid=(rows // tm,),
        in_specs=[
            row(d),
            pl.BlockSpec((1, 6, d), lambda i: (i // per, 0, 0)),
            row(d_br), row(d_br), row(d_br), row(d_br), row(d_br), row(d_br),
        ] + [_resident(w.shape, layer) for w in weights],
        out_specs=row(d),
        out_shape=jax.ShapeDtypeStruct((rows, d), F32),
        compiler_params=_params("parallel"),
        name="merge",
    )(x2, mod, a, of, ob, p["hg"], p["mu"], p["mvn"], *weights)


def _ffn_kernel(x_ref, mod_ref, g_ref, wg_ref, wu_ref, wd_ref, fg_ref, o_ref, act_ref, *, final, bf):
    for rows in (pl.ds(r, FFN_ROWS) for r in range(0, x_ref.shape[0], FFN_ROWS)):
        x = x_ref[rows, :]
        h = _rms_mod(x, g_ref[...], mod_ref[0, 4:5, :], mod_ref[0, 3:4, :]).astype(BF16)
        for c in range(wg_ref.shape[1] // bf):
            sl = slice(c * bf, (c + 1) * bf)
            act_ref[rows, sl] = (_silu(_dot(h, wg_ref[:, sl])) * _dot(h, wu_ref[:, sl])).astype(BF16)
        y = x + mod_ref[0, 5:6, :] * _dot(act_ref[rows, :], wd_ref[...])
        if final:
            y = y * lax.rsqrt(jnp.mean(y * y, axis=-1, keepdims=True) + EPS) * fg_ref[...]
        o_ref[rows, :] = y


def _ffn(x2, mod, norm_g, w_gate, w_up, w_down, final_g, *, layer, seq, tm, final):
    rows, d = x2.shape
    d_ff = w_gate.shape[2]
    per = seq // tm
    row = pl.BlockSpec((tm, d), lambda i: (i, 0))
    return pl.pallas_call(
        functools.partial(_ffn_kernel, final=final, bf=MXU_COLS),
        grid=(rows // tm,),
        in_specs=[
            row,
            pl.BlockSpec((1, 6, d), lambda i: (i // per, 0, 0)),
            _resident(norm_g.shape, layer),
            _resident(w_gate.shape, layer),
            _resident(w_up.shape, layer),
            _resident(w_down.shape, layer),
            _resident((1, d)),
        ],
        out_specs=row,
        out_shape=jax.ShapeDtypeStruct((rows, d), F32),
        scratch_shapes=[pltpu.VMEM((tm, d_ff), BF16)],
        compiler_params=_params("parallel"),
        name="ffn_final" if final else "ffn",
    )(x2, mod, norm_g, w_gate, w_up, w_down, final_g)


def _rope_tables(seq):
    pos = jnp.arange(seq)
    nf = HEAD_DIM // 4
    inv = ROPE_BASE ** (-jnp.arange(nf, dtype=F32) / nf)
    ang_r = (pos // GRID_W).astype(F32)[:, None] * inv[None, :]
    ang_c = (pos % GRID_W).astype(F32)[:, None] * inv[None, :]
    zero = jnp.zeros_like(ang_r)
    cos = jnp.concatenate([jnp.cos(ang_r)] * 2 + [jnp.cos(ang_c)] * 2, axis=1)
    sa = jnp.concatenate([zero, jnp.sin(ang_r), zero, jnp.sin(ang_c)], axis=1)
    sb = jnp.concatenate([-jnp.sin(ang_r), zero, -jnp.sin(ang_c), zero], axis=1)
    reps = LANES // HEAD_DIM
    return tuple(jnp.tile(t, (1, reps)) for t in (cos, sa, sb))


def kernel(x, c, ctx, c_ctx, w_ada, b_ada, norm1_g, norm2_g, w_in, attn_sink, hg_lb_logits, hg_norm_g, mlp_v_norm_g, mlp_v_norm_b, mlp_ws, mlp_bs, w_branch, w_out, w_ffn_gate, w_ffn_up, w_ffn_down, final_norm_g):
    batch, seq, d = x.shape
    ctx_len = ctx.shape[1]
    depth = w_ada.shape[0]
    d_br = d // 2
    tm_x, tm_c = min(ROW_TILE, seq), min(ROW_TILE // 4, ctx_len)
    assert batch + 1 <= MOD_ROWS and seq % tm_x == 0 and ctx_len % tm_c == 0 and tm_c % MERGE_ROWS == 0

    lb = jnp.cumsum(jax.nn.softmax(hg_lb_logits.astype(F32), axis=0), axis=0)
    lb = lb - lb[0]
    cs = jnp.concatenate([c, c_ctx[None, :], jnp.zeros((MOD_ROWS - batch - 1, d), F32)], axis=0)
    mod = _ada(cs, w_ada, b_ada).reshape(depth, MOD_ROWS, 6, d)
    tables = _rope_tables(seq)
    zero_state = jnp.zeros((batch, 2, HG_HEADS, HG_DK, LANES), F32)

    def reorder_heads(t, axis):
        rep = ATT_HEADS // ATT_KV_HEADS
        split = t.shape[:axis] + (ATT_KV_HEADS, rep, HEAD_DIM) + t.shape[axis + 1:]
        return jnp.swapaxes(t.reshape(split), axis, axis + 1).reshape(t.shape)

    d_gate = N_BRANCH * d
    w_proj = jnp.concatenate([reorder_heads(w_in[:, :, :d_br], 2), w_in[:, :, d_br:-d_gate]],
                             axis=2).astype(BF16)
    w_gates = w_in[:, :, -d_gate:].astype(BF16)
    w_br = jnp.concatenate([reorder_heads(w_branch[:, :1], 2), w_branch[:, 1:]], axis=1).astype(BF16)
    ws = mlp_ws.astype(BF16)
    bs = jnp.broadcast_to(mlp_bs[:, :, :, None], mlp_bs.shape + (LANES,))
    w_o, wg, wu, wd = (w.astype(BF16) for w in (w_out, w_ffn_gate, w_ffn_up, w_ffn_down))
    n1, n2, vg, vb, hgn = (t[:, None, :] for t in (norm1_g, norm2_g, mlp_v_norm_g, mlp_v_norm_b, hg_norm_g))
    fg = final_norm_g[None, :]

    xs = x.reshape(batch * seq, d)
    cx = ctx.reshape(batch * ctx_len, d)
    for l in range(depth):
        need_ctx = l < depth - 1
        mod_x = mod[l, :batch]
        mod_c = jnp.broadcast_to(mod[l, batch:batch + 1], (batch, 6, d))
        pc = _in_proj(cx, mod_c, n1, w_proj, lb, vg, vb, tables, layer=l, seq=ctx_len, tm=tm_c, rope=False,
                      only=None if need_ctx else STATE_ONLY_OUTPUTS)
        px = _in_proj(xs, mod_x, n1, w_proj, lb, vg, vb, tables, layer=l, seq=seq, tm=tm_x // 2, rope=True)
        a = _attention(px["qa"], px["ka"], px["va"], pc["ka"], pc["va"], attn_sink[l],
                       batch=batch, seq=seq, ctx_len=ctx_len, band=True)
        if need_ctx:
            ocf, ocb, s_ctx = _hgrn(pc, zero_state, batch=batch, seq=ctx_len)
        else:
            s_ctx = _hgrn(pc, zero_state, batch=batch, seq=ctx_len, state_only=True)
        of, ob, _ = _hgrn(px, s_ctx, batch=batch, seq=seq)
        xs = _merge(xs, mod_x, a, of, ob, px, n1, w_gates, hgn, ws, bs, w_br, w_o,
                    layer=l, seq=seq, tm=tm_x)
        xs = _ffn(xs, mod_x, n2, wg, wu, wd, fg, layer=l, seq=seq, tm=tm_x, final=not need_ctx)
        if need_ctx:
            ac = _attention(pc["qa"], None, None, pc["ka"], pc["va"], attn_sink[l],
                            batch=batch, seq=ctx_len, ctx_len=ctx_len, band=False)
            cx = _merge(cx, mod_c, ac, ocf, ocb, pc, n1, w_gates, hgn, ws, bs, w_br, w_o,
                        layer=l, seq=ctx_len, tm=tm_c)
            cx = _ffn(cx, mod_c, n2, wg, wu, wd, fg, layer=l, seq=ctx_len, tm=tm_c, final=False)
    return xs.reshape(batch, seq, d)
```

```python
import functools
import math

import jax
import jax.numpy as jnp
import numpy as np
from jax import lax
from jax.experimental import pallas as pl
from jax.experimental.pallas import tpu as pltpu

F32 = jnp.float32
BF16 = jnp.bfloat16

EPS = 1e-6
GRID_W = 64
HEAD_DIM = 64
ATT_HEADS = 8
ATT_KV_HEADS = 2
WINDOW = 128
ATT_BLOCK = 128
ROPE_BASE = 10000.0
HG_DK = 128
HG_HEADS = 4
HG_CHUNK = 128
MLP_CHUNK = 128
MLP_GROUPS = 4
N_BRANCH = 3

LANES = 128
MXU_COLS = 256
IN_PROJ_ROWS = 128
FFN_ROWS = 128
MERGE_ROWS = 256
ATT_SAMPLES = 4
ROW_TILE = 1024
MOD_ROWS = 24
VMEM_LIMIT = 56 * 1024 * 1024
NEG = -1e30
LOG2E = 1.4426950408889634

assert WINDOW == ATT_BLOCK


def _sigmoid(x):
    return 0.5 * jnp.tanh(0.5 * x) + 0.5


def _silu(x):
    return x * _sigmoid(x)


def _rms_mod(x, g, sc, sh):
    y = x * lax.rsqrt(jnp.mean(x * x, axis=-1, keepdims=True) + EPS)
    return (y * g) * (1.0 + sc) + sh


def _dot(a, b):
    return jnp.dot(a, b, preferred_element_type=F32)


def _dot_nt(a, b):
    return lax.dot_general(a, b, (((1,), (1,)), ((), ())), preferred_element_type=F32)


def _dot_tn(a, b):
    return lax.dot_general(a, b, (((0,), (0,)), ((), ())), preferred_element_type=F32)


def _resident(shape, layer=None):
    if layer is None:
        return pl.BlockSpec(shape, lambda *_: (0,) * len(shape), pipeline_mode=pl.Buffered(1))
    rest = tuple(shape[1:])
    return pl.BlockSpec((None,) + rest, lambda *_: (layer,) + (0,) * len(rest),
                        pipeline_mode=pl.Buffered(1))


def _params(*sem):
    return pltpu.CompilerParams(dimension_semantics=sem, vmem_limit_bytes=VMEM_LIMIT)


def _ada_kernel(c_ref, w_ref, b_ref, o_ref):
    s = _silu(c_ref[...])
    o_ref[...] = jnp.dot(s, w_ref[...], precision=lax.Precision.HIGHEST,
                         preferred_element_type=F32) + b_ref[...]


def _ada(cs, w_ada, b_ada):
    depth, d, d6 = w_ada.shape
    bn = d
    return pl.pallas_call(
        _ada_kernel,
        grid=(depth, d6 // bn),
        in_specs=[
            pl.BlockSpec((MOD_ROWS, d), lambda l, j: (0, 0)),
            pl.BlockSpec((None, d, bn), lambda l, j: (l, 0, j)),
            pl.BlockSpec((None, 1, bn), lambda l, j: (l, 0, j)),
        ],
        out_specs=pl.BlockSpec((None, MOD_ROWS, bn), lambda l, j: (l, 0, j)),
        out_shape=jax.ShapeDtypeStruct((depth, MOD_ROWS, d6), F32),
        compiler_params=_params("parallel", "parallel"),
        name="ada",
    )(cs, w_ada, b_ada.reshape(depth, 1, d6))


def _rope(x, cos, sa, sb):
    return x * cos + pltpu.roll(x, 16, 1) * sa + pltpu.roll(x, LANES - 16, 1) * sb


def _forget_terms(z, lb):
    z2 = z * LOG2E
    u = jnp.exp2(-jnp.abs(z2))
    one_u = 1.0 + u
    log_sig = jnp.minimum(z2, 0.0) - jnp.log2(one_u)
    la = jnp.maximum(jnp.log2(lb), NEG)
    b = jnp.log1p(-lb) * LOG2E + log_sig
    logf2 = jnp.maximum(la, b) + jnp.log2(1.0 + jnp.exp2(-jnp.abs(la - b)))
    r = 1.0 / one_u
    k = (1.0 - lb) * jnp.where(z >= 0.0, u * r, r)
    return logf2, k


def _in_proj_kernel(x_ref, mod_ref, g_ref, w_ref, lb_ref, vg_ref, vb_ref, cos_ref, sa_ref, sb_ref,
                    *out_refs, rope, d_br, names):
    out = dict(zip(names, out_refs))
    cw = MXU_COLS
    off_q, off_kv, off_hq = 0, d_br, d_br + cw
    off_f, off_b, off_v, off_g, off_u, off_cv = (off_hq + n * d_br for n in range(1, 7))
    for rows in (pl.ds(r, IN_PROJ_ROWS) for r in range(0, x_ref.shape[0], IN_PROJ_ROWS)):
        h = _rms_mod(x_ref[rows, :], g_ref[...], mod_ref[0, 1:2, :], mod_ref[0, 0:1, :]).astype(BF16)

        def proj(lo):
            return _dot(h, w_ref[:, lo:lo + cw])

        def maybe_rope(t):
            if not rope:
                return t
            return _rope(t, cos_ref[rows, :], sa_ref[rows, :], sb_ref[rows, :])

        def each(name, lo, fn):
            if name in out:
                for c in range(0, out[name].shape[1], cw):
                    out[name][rows, c:c + cw] = fn(proj(lo + c)).astype(out[name].dtype)

        def rope_lanes(acc):
            parts = [maybe_rope(acc[:, s:s + LANES]) for s in range(0, cw, LANES)]
            return jnp.concatenate(parts, axis=1) * (HEAD_DIM ** -0.5 * LOG2E)

        each("qa", off_q, rope_lanes)
        acc = proj(off_kv)
        out["ka"][rows, :] = maybe_rope(acc[:, :LANES]).astype(BF16)
        out["va"][rows, :] = acc[:, LANES:].astype(BF16)
        each("hq", off_hq, lambda a: _silu(a) * HG_DK ** -0.5)
        plain = iter([(name, o + c, c) for name, o in (("hv", off_v), ("mu", off_u)) if name in out
                      for c in range(0, d_br, cw)])
        for which, (l_name, k_name, lo) in enumerate((("hlf", "hkf", off_f), ("hlb", "hkb", off_b))):
            for c in range(0, d_br, cw):
                logf2, k = _forget_terms(proj(lo + c), lb_ref[which:which + 1, c:c + cw])
                out[l_name][rows, c:c + cw] = logf2
                out[k_name][rows, c:c + cw] = k.astype(BF16)
                filler = next(plain, None)
                if filler is not None:
                    p_name, p_lo, p_c = filler
                    out[p_name][rows, p_c:p_c + cw] = proj(p_lo).astype(BF16)
        each("hg", off_g, lambda a: _silu(a))
        if "mvn" in out:
            v = jnp.concatenate([proj(off_cv + c) for c in range(0, d_br, cw)], axis=1)
            mean = jnp.mean(v, axis=-1, keepdims=True)
            var = jnp.mean(jnp.square(v - mean), axis=-1, keepdims=True)
            out["mvn"][rows, :] = ((v - mean) * lax.rsqrt(var + EPS) * vg_ref[...]
                                   + vb_ref[...]).astype(BF16)


STATE_ONLY_OUTPUTS = ("ka", "va", "hkf", "hkb", "hlf", "hlb", "hv")


def _in_proj(x2, mod, norm_g, w_in, lb, v_g, v_b, tables, *, layer, seq, tm, rope, only=None):
    rows, d = x2.shape
    d_br = d // 2
    per = seq // tm
    row = lambda w: pl.BlockSpec((tm, w), lambda i: (i, 0))
    tab = pl.BlockSpec((tm, LANES), lambda i: (i % per, 0))
    widths = dict(qa=d_br, ka=LANES, va=LANES, hq=d_br, hkf=d_br, hkb=d_br, hlf=d_br, hlb=d_br,
                  hv=d_br, hg=d_br, mu=d_br, mvn=d_br)
    assert w_in.shape[2] == sum(widths.values()) - 2 * d_br
    if only is not None:
        widths = {k: w for k, w in widths.items() if k in only}
    dtypes = {k: (F32 if k in ("hlf", "hlb") else BF16) for k in widths}
    outs = pl.pallas_call(
        functools.partial(_in_proj_kernel, rope=rope, d_br=d_br, names=tuple(widths)),
        grid=(rows // tm,),
        in_specs=[
            row(d),
            pl.BlockSpec((1, 6, d), lambda i: (i // per, 0, 0)),
            _resident(norm_g.shape, layer),
            _resident(w_in.shape, layer),
            _resident(lb.shape, layer),
            _resident(v_g.shape, layer),
            _resident(v_b.shape, layer),
            tab, tab, tab,
        ],
        out_specs=[row(w) for w in widths.values()],
        out_shape=[jax.ShapeDtypeStruct((rows, w), dtypes[k]) for k, w in widths.items()],
        compiler_params=_params("parallel"),
        name="in_proj_rope" if rope else "in_proj_ctx",
    )(x2, mod, norm_g, w_in, lb, v_g, v_b, *tables)
    return dict(zip(widths, outs))


def _attn_kernel(sink_ref, q_ref, *refs, band):
    if band:
        kl_ref, km_ref, kr_ref, vl_ref, vm_ref, vr_ref, kc_ref, vc_ref, o_ref = refs
        j = pl.program_id(1)
        nq = pl.num_programs(1)
        qi = lax.broadcasted_iota(jnp.int32, (ATT_BLOCK, ATT_BLOCK), 0)
        kj = lax.broadcasted_iota(jnp.int32, (ATT_BLOCK, ATT_BLOCK), 1)
        left = jnp.where(jnp.logical_and(kj >= qi, j > 0), 0.0, NEG)
        right = jnp.where(jnp.logical_and(kj <= qi, j < nq - 1), 0.0, NEG)
    else:
        kc_ref, vc_ref, o_ref = refs
    tq = q_ref.shape[1]
    groups = q_ref.shape[2] // LANES
    rep = ATT_HEADS // ATT_KV_HEADS
    left_half = lax.broadcasted_iota(jnp.int32, (tq, LANES), 1) < HEAD_DIM
    top = lax.broadcasted_iota(jnp.int32, (2 * tq, 1), 0) < tq
    work = []
    for s in range(q_ref.shape[0]):
        if band:
            keys = jnp.concatenate([kl_ref[s], km_ref[s], kr_ref[s], kc_ref[s]], axis=0)
            vals = jnp.concatenate([vl_ref[s], vm_ref[s], vr_ref[s], vc_ref[s]], axis=0)
        else:
            keys, vals = kc_ref[s], vc_ref[s]
        for g in range(groups):
            qc = q_ref[s, :, g * LANES:(g + 1) * LANES]
            zero = jnp.zeros_like(qc)
            q2 = jnp.concatenate([jnp.where(left_half, qc, zero), jnp.where(left_half, zero, qc)], axis=0)
            work.append((s, g, _dot_nt(q2, keys), vals))
    for s, g, sc, vals in work:
        if band:
            two = lambda t: jnp.concatenate([t, t], axis=0)
            sc = jnp.concatenate([sc[:, :ATT_BLOCK] + two(left), sc[:, ATT_BLOCK:2 * ATT_BLOCK],
                                  sc[:, 2 * ATT_BLOCK:3 * ATT_BLOCK] + two(right), sc[:, 3 * ATT_BLOCK:]],
                                 axis=1)
        sink = jnp.where(top, sink_ref[g], sink_ref[g + rep]) * LOG2E
        m = jnp.maximum(jnp.max(sc, axis=-1, keepdims=True), sink)
        e = jnp.exp2(sc - m)
        den = jnp.sum(e, axis=-1, keepdims=True) + jnp.exp2(sink - m)
        o = _dot(e.astype(BF16), vals) * (1.0 / den)
        o_ref[s, :, g * LANES:(g + 1) * LANES] = jnp.where(left_half, o[:tq], o[tq:]).astype(BF16)


def _attention(q, k, v, kc, vc, sink, *, batch, seq, ctx_len, band):
    d_q = q.shape[1]
    tq = ATT_BLOCK
    nq = seq // tq
    ns = math.gcd(batch, ATT_SAMPLES)
    smem = pl.BlockSpec(memory_space=pltpu.SMEM)
    qspec = pl.BlockSpec((ns, tq, d_q), lambda b, j: (b, j, 0))
    cspec = pl.BlockSpec((ns, ctx_len, LANES), lambda b, j: (b, 0, 0))
    per_sample = lambda t, n: t.reshape(batch, n, t.shape[1])
    kc, vc = per_sample(kc, ctx_len), per_sample(vc, ctx_len)
    if band:
        kvspecs = [
            pl.BlockSpec((ns, tq, LANES), lambda b, j: (b, jnp.maximum(j - 1, 0), 0)),
            pl.BlockSpec((ns, tq, LANES), lambda b, j: (b, j, 0)),
            pl.BlockSpec((ns, tq, LANES), lambda b, j: (b, jnp.minimum(j + 1, nq - 1), 0)),
        ]
        in_specs = [smem, qspec] + kvspecs + kvspecs + [cspec, cspec]
        k, v = per_sample(k, seq), per_sample(v, seq)
        args = (sink, per_sample(q, seq), k, k, k, v, v, v, kc, vc)
    else:
        in_specs = [smem, qspec, cspec, cspec]
        args = (sink, per_sample(q, seq), kc, vc)
    return pl.pallas_call(
        functools.partial(_attn_kernel, band=band),
        grid=(batch // ns, nq),
        in_specs=in_specs,
        out_specs=qspec,
        out_shape=jax.ShapeDtypeStruct((batch, seq, d_q), BF16),
        compiler_params=_params("parallel", "parallel"),
        name="attn_band" if band else "attn_ctx",
    )(*args).reshape(batch * seq, d_q)


HG_SAMPLES = 4
HG_LEVELS = tuple(HG_CHUNK >> i for i in range(1, HG_CHUNK.bit_length()))


def _hgrn_tables():
    c = HG_CHUNK
    t = np.arange(c)[:, None]
    s = np.arange(c)[None, :]
    mats, lvls = [], []
    for rev in (False, True):
        lvl = np.full((c, c), -1, np.int32)
        lvl[np.arange(c), np.arange(c)] = 0
        for idx, b in enumerate(HG_LEVELS):
            same = (t // (2 * b)) == (s // (2 * b))
            t_hi, s_hi = (t % (2 * b)) >= b, (s % (2 * b)) >= b
            lvl[same & ((~t_hi & s_hi) if rev else (t_hi & ~s_hi))] = idx + 1
        mat = ((s >= t) if rev else (s <= t)).astype(np.float32)
        mats.append(np.concatenate([mat, mat], axis=1))
        n_bits = len(HG_LEVELS).bit_length()
        planes = [(np.maximum(lvl, 0) >> bit) & 1 for bit in range(n_bits)] + [lvl >= 0]
        lvls.append(np.stack(planes).astype(np.int32))
    return np.stack(mats), np.stack(lvls)


def _boundary_exponent(g, g_ref, lg, lanes, b, rev):
    c = g.shape[0]
    row = lax.broadcasted_iota(jnp.int32, (c, 1), 0)
    bound = lambda g0: g0 + b - (0 if rev else 1)
    if b == 1:
        return jnp.where(((row & 1) != 0) != rev, lg, 0.0)
    if b >= 8:
        pieces = []
        for g0 in range(0, c, 2 * b):
            gr = g_ref[bound(g0):bound(g0) + 1, lanes]
            first, second = g[g0:g0 + b], g[g0 + b:g0 + 2 * b]
            pieces += [first - gr, gr - second] if rev else [gr - first, second - gr]
        return jnp.concatenate(pieces, axis=0)
    sub = 8
    pieces = []
    for r0 in range(0, c, sub):
        refs = [jnp.broadcast_to(g_ref[bound(g0):bound(g0) + 1, lanes], (sub, LANES))
                for g0 in range(r0, r0 + sub, 2 * b)]
        gr = refs[0]
        for n in range(1, len(refs)):
            gr = jnp.where((row[:sub] & (sub - 1)) >= n * 2 * b, refs[n], gr)
        pieces.append(gr)
    return -jnp.abs(g - jnp.concatenate(pieces, axis=0))


def _mixed_rows(q, k, b, rev):
    c = q.shape[0]
    if b >= 8:
        pieces = []
        for g0 in range(0, c, 2 * b):
            lo, hi = slice(g0, g0 + b), slice(g0 + b, g0 + 2 * b)
            pieces += [q[lo], k[hi]] if rev else [k[lo], q[hi]]
        return jnp.concatenate(pieces, axis=0)
    later = (lax.broadcasted_iota(jnp.int32, (c, 1), 0) & b) != 0
    return jnp.where(later, k, q) if rev else jnp.where(later, q, k)


def _hgrn_scores(q, k, lg, g_ref, lanes, lvl, rev):
    c = HG_CHUNK
    g = g_ref[:, lanes]
    qf, kf = q.astype(F32), k.astype(F32)
    blocks = [(q, k.T)]
    for b in HG_LEVELS:
        e = jnp.exp2(_boundary_exponent(g, g_ref, lg, lanes, b, rev))
        y = (_mixed_rows(qf, kf, b, rev) * e).astype(BF16)
        blocks.append((y, y.T))
    zero = jnp.zeros((LANES, c), BF16)
    bits = [lvl[n] != 0 for n in range(lvl.shape[0] - 1)]
    picked = []
    for i in range(0, len(blocks), 2):
        (la, ra), (lb, rb) = blocks[i], blocks[i + 1]
        rhs = jnp.concatenate([jnp.concatenate([ra, zero], axis=1),
                               jnp.concatenate([zero, rb], axis=1)], axis=0)
        s = _dot(jnp.concatenate([la, lb], axis=1), rhs)
        picked.append(jnp.where(bits[0], s[:, c:], s[:, :c]))
    for bit in bits[1:]:
        picked = [jnp.where(bit, hi, lo) for lo, hi in zip(picked[0::2], picked[1::2])]
    return jnp.where(lvl[-1] != 0, picked[0], 0.0).astype(BF16)


def _hgrn_apply(q, k, v, att, g_ref, lanes, state, rev):
    lhs = jnp.concatenate([(q.astype(F32) * jnp.exp2(g_ref[:, lanes])).astype(BF16), att], axis=1)
    rhs = jnp.concatenate([state.astype(BF16), v], axis=0)
    return _dot(lhs, rhs), _hgrn_next_state(k, v, g_ref, lanes, state, rev)


def _hgrn_next_state(k, v, g_ref, lanes, state, rev):
    g = g_ref[:, lanes]
    end = 0 if rev else HG_CHUNK - 1
    g_end = g_ref[end:end + 1, lanes]
    dec_t = jnp.broadcast_to(jnp.exp2(g_end), (HG_DK, LANES)).T
    return dec_t * state + _dot_tn((k.astype(F32) * jnp.exp2(g_end - g)).astype(BF16), v)


def _hgrn_cumsum(lg, m):
    hi = lg.astype(BF16)
    lo = (lg - hi.astype(F32)).astype(BF16)
    return _dot(m, jnp.concatenate([hi, lo], axis=0))


def _hgrn_state_kernel(kf_ref, lf_ref, vf_ref, kb_ref, lb_ref, vb_ref, s0_ref, m_ref,
                       sfin_ref, state_ref, g_ref):
    i = pl.program_id(1)

    @pl.when(i == 0)
    def _():
        state_ref[...] = s0_ref[...]

    for s in range(state_ref.shape[0]):
        for d, (k_ref, l_ref, v_ref) in enumerate(((kf_ref, lf_ref, vf_ref), (kb_ref, lb_ref, vb_ref))):
            g_ref[s, d] = _hgrn_cumsum(l_ref[s], m_ref[d])
            for h in range(HG_HEADS):
                lanes = slice(h * LANES, (h + 1) * LANES)
                state_ref[s, d, h] = _hgrn_next_state(k_ref[s, :, lanes], v_ref[s, :, lanes],
                                                      g_ref.at[s, d], lanes, state_ref[s, d, h], bool(d))

    @pl.when(i == pl.num_programs(1) - 1)
    def _():
        sfin_ref[...] = state_ref[...]


def _hgrn_kernel(qf_ref, kf_ref, lf_ref, vf_ref, qb_ref, kb_ref, lb_ref, vb_ref, s0_ref, m_ref, lvl_ref,
                 of_ref, ob_ref, sfin_ref, state_ref, g_ref):
    i = pl.program_id(1)

    @pl.when(i == 0)
    def _():
        state_ref[...] = s0_ref[...]

    dirs = ((qf_ref, kf_ref, lf_ref, vf_ref, of_ref), (qb_ref, kb_ref, lb_ref, vb_ref, ob_ref))
    work = []
    for s in range(state_ref.shape[0]):
        for d, (q_ref, k_ref, l_ref, v_ref, o_ref) in enumerate(dirs):
            lg = l_ref[s]
            g_ref[s, d] = _hgrn_cumsum(lg, m_ref[d])
            for h in range(HG_HEADS):
                lanes = slice(h * LANES, (h + 1) * LANES)
                work.append((s, d, h, lanes, o_ref, q_ref[s, :, lanes], k_ref[s, :, lanes],
                             v_ref[s, :, lanes], lg[:, lanes], state_ref[s, d, h]))
    atts = [_hgrn_scores(q, k, lg, g_ref.at[s, d], lanes, lvl_ref[d], bool(d))
            for s, d, h, lanes, o_ref, q, k, v, lg, state in work]
    done = []
    for att, (s, d, h, lanes, o_ref, q, k, v, lg, state) in zip(atts, work):
        out, new_state = _hgrn_apply(q, k, v, att, g_ref.at[s, d], lanes, state, bool(d))
        done.append((s, d, h, lanes, o_ref, out, new_state))
    for s, d, h, lanes, o_ref, out, new_state in done:
        state_ref[s, d, h] = new_state
        o_ref[s, :, lanes] = out.astype(o_ref.dtype)

    @pl.when(i == pl.num_programs(1) - 1)
    def _():
        sfin_ref[...] = state_ref[...]


def _hgrn(p, s0, *, batch, seq, state_only=False):
    c = HG_CHUNK
    n = seq // c
    ns = math.gcd(batch, HG_SAMPLES)
    width = p["hv"].shape[1]
    mats, lvls = _hgrn_tables()
    fwd = pl.BlockSpec((ns, c, width), lambda b, i: (b, i, 0))
    bwd = pl.BlockSpec((ns, c, width), lambda b, i: (b, n - 1 - i, 0))
    st = pl.BlockSpec((ns, 2, HG_HEADS, HG_DK, LANES), lambda b, i: (b, 0, 0, 0, 0))
    scratch = [pltpu.VMEM((ns, 2, HG_HEADS, HG_DK, LANES), F32),
               pltpu.VMEM((ns, 2, mats.shape[1], width), F32)]
    arg = lambda name: p[name].reshape(batch, seq, width)
    if state_only:
        return pl.pallas_call(
            _hgrn_state_kernel,
            grid=(batch // ns, n),
            in_specs=[fwd, fwd, fwd, bwd, bwd, bwd, st, _resident(mats.shape)],
            out_specs=st,
            out_shape=jax.ShapeDtypeStruct(s0.shape, F32),
            scratch_shapes=scratch,
            compiler_params=_params("parallel", "arbitrary"),
            name="hgrn_state",
        )(arg("hkf"), arg("hlf"), arg("hv"), arg("hkb"), arg("hlb"), arg("hv"), s0, jnp.asarray(mats, BF16))
    of, ob, s_fin = pl.pallas_call(
        _hgrn_kernel,
        grid=(batch // ns, n),
        in_specs=[fwd, fwd, fwd, fwd, bwd, bwd, bwd, bwd, st, _resident(mats.shape), _resident(lvls.shape)],
        out_specs=[fwd, bwd, st],
        out_shape=[jax.ShapeDtypeStruct((batch, seq, width), BF16),
                   jax.ShapeDtypeStruct((batch, seq, width), BF16),
                   jax.ShapeDtypeStruct(s0.shape, F32)],
        scratch_shapes=scratch,
        compiler_params=_params("parallel", "arbitrary"),
        name="hgrn_scan",
    )(arg("hq"), arg("hkf"), arg("hlf"), arg("hv"), arg("hq"), arg("hkb"), arg("hlb"), arg("hv"), s0,
      jnp.asarray(mats, BF16), jnp.asarray(lvls))
    return of.reshape(batch * seq, width), ob.reshape(batch * seq, width), s_fin


def _merge_kernel(x_ref, mod_ref, a_ref, of_ref, ob_ref, hg_ref, mu_ref, mvn_ref,
                  g_ref, wgt_ref, hgn_ref, ws_ref, bs_ref, wbr_ref, wo_ref, o_ref):
    tm, d = x_ref.shape
    cw = MXU_COLS
    for r0 in range(0, tm, MERGE_ROWS):
        rows = pl.ds(r0, MERGE_ROWS)
        h = _rms_mod(x_ref[rows, :], g_ref[...], mod_ref[0, 1:2, :], mod_ref[0, 0:1, :]).astype(BF16)
        o = of_ref[rows, :].astype(F32) + ob_ref[rows, :].astype(F32)
        parts = []
        for head in range(HG_HEADS):
            oh = o[:, head * LANES:(head + 1) * LANES]
            parts.append(oh * lax.rsqrt(jnp.mean(oh * oh, axis=-1, keepdims=True) + EPS))
        b_br = (jnp.concatenate(parts, axis=1) * hgn_ref[...] * hg_ref[rows, :].astype(F32)).astype(BF16)
        gated = []
        for r in range(r0, r0 + MERGE_ROWS, MLP_CHUNK):
            cols = []
            for grp in range(MLP_GROUPS):
                cs = slice(grp * LANES, (grp + 1) * LANES)
                cols.append(_dot(ws_ref[grp], mvn_ref[pl.ds(r, MLP_CHUNK), cs]) + bs_ref[grp])
            gated.append(jnp.concatenate(cols, axis=1))
        c_br = (mu_ref[rows, :].astype(F32) * jnp.concatenate(gated, axis=0)).astype(BF16)
        branches = (a_ref[rows, :], b_br, c_br)
        merged = []
        for c in range(0, d, cw):
            m = jnp.zeros((MERGE_ROWS, cw), F32)
            for n, br in enumerate(branches):
                gate = _sigmoid(_dot(h, wgt_ref[:, n * d + c:n * d + c + cw]))
                m = m + gate * _dot(br, wbr_ref[n, :, c:c + cw])
            merged.append(m.astype(BF16))
        merged = jnp.concatenate(merged, axis=1)
        for c in range(0, d, cw):
            y = _dot(merged, wo_ref[:, c:c + cw])
            o_ref[rows, c:c + cw] = x_ref[rows, c:c + cw] + mod_ref[0, 2:3, c:c + cw] * y


def _merge(x2, mod, a, of, ob, p, norm_g, w_gates, hg_g, ws, bs, w_br, w_o, *, layer, seq, tm):
    rows, d = x2.shape
    d_br = d // 2
    per = seq // tm
    row = lambda w: pl.BlockSpec((tm, w), lambda i: (i, 0))
    weights = (norm_g, w_gates, hg_g, ws, bs, w_br, w_o)
    return pl.pallas_call(
        _merge_kernel,
        grid=(rows // tm,),
        in_specs=[
            row(d),
            pl.BlockSpec((1, 6, d), lambda i: (i // per, 0, 0)),
            row(d_br), row(d_br), row(d_br), row(d_br), row(d_br), row(d_br),
        ] + [_resident(w.shape, layer) for w in weights],
        out_specs=row(d),
        out_shape=jax.ShapeDtypeStruct((rows, d), F32),
        compiler_params=_params("parallel"),
        name="merge",
    )(x2, mod, a, of, ob, p["hg"], p["mu"], p["mvn"], *weights)


def _ffn_kernel(x_ref, mod_ref, g_ref, wg_ref, wu_ref, wd_ref, fg_ref, o_ref, act_ref, *, final, bf):
    for rows in (pl.ds(r, FFN_ROWS) for r in range(0, x_ref.shape[0], FFN_ROWS)):
        x = x_ref[rows, :]
        h = _rms_mod(x, g_ref[...], mod_ref[0, 4:5, :], mod_ref[0, 3:4, :]).astype(BF16)
        for c in range(wg_ref.shape[1] // bf):
            sl = slice(c * bf, (c + 1) * bf)
            act_ref[rows, sl] = (_silu(_dot(h, wg_ref[:, sl])) * _dot(h, wu_ref[:, sl])).astype(BF16)
        y = x + mod_ref[0, 5:6, :] * _dot(act_ref[rows, :], wd_ref[...])
        if final:
            y = y * lax.rsqrt(jnp.mean(y * y, axis=-1, keepdims=True) + EPS) * fg_ref[...]
        o_ref[rows, :] = y


def _ffn(x2, mod, norm_g, w_gate, w_up, w_down, final_g, *, layer, seq, tm, final):
    rows, d = x2.shape
    d_ff = w_gate.shape[2]
    per = seq // tm
    row = pl.BlockSpec((tm, d), lambda i: (i, 0))
    return pl.pallas_call(
        functools.partial(_ffn_kernel, final=final, bf=MXU_COLS),
        grid=(rows // tm,),
        in_specs=[
            row,
            pl.BlockSpec((1, 6, d), lambda i: (i // per, 0, 0)),
            _resident(norm_g.shape, layer),
            _resident(w_gate.shape, layer),
            _resident(w_up.shape, layer),
            _resident(w_down.shape, layer),
            _resident((1, d)),
        ],
        out_specs=row,
        out_shape=jax.ShapeDtypeStruct((rows, d), F32),
        scratch_shapes=[pltpu.VMEM((tm, d_ff), BF16)],
        compiler_params=_params("parallel"),
        name="ffn_final" if final else "ffn",
    )(x2, mod, norm_g, w_gate, w_up, w_down, final_g)


def _rope_tables(seq):
    pos = jnp.arange(seq)
    nf = HEAD_DIM // 4
    inv = ROPE_BASE ** (-jnp.arange(nf, dtype=F32) / nf)
    ang_r = (pos // GRID_W).astype(F32)[:, None] * inv[None, :]
    ang_c = (pos % GRID_W).astype(F32)[:, None] * inv[None, :]
    zero = jnp.zeros_like(ang_r)
    cos = jnp.concatenate([jnp.cos(ang_r)] * 2 + [jnp.cos(ang_c)] * 2, axis=1)
    sa = jnp.concatenate([zero, jnp.sin(ang_r), zero, jnp.sin(ang_c)], axis=1)
    sb = jnp.concatenate([-jnp.sin(ang_r), zero, -jnp.sin(ang_c), zero], axis=1)
    reps = LANES // HEAD_DIM
    return tuple(jnp.tile(t, (1, reps)) for t in (cos, sa, sb))


def kernel(x, c, ctx, c_ctx, w_ada, b_ada, norm1_g, norm2_g, w_in, attn_sink, hg_lb_logits, hg_norm_g, mlp_v_norm_g, mlp_v_norm_b, mlp_ws, mlp_bs, w_branch, w_out, w_ffn_gate, w_ffn_up, w_ffn_down, final_norm_g):
    batch, seq, d = x.shape
    ctx_len = ctx.shape[1]
    depth = w_ada.shape[0]
    d_br = d // 2
    tm_x, tm_c = min(ROW_TILE, seq), min(ROW_TILE // 4, ctx_len)
    assert batch + 1 <= MOD_ROWS and seq % tm_x == 0 and ctx_len % tm_c == 0 and tm_c % MERGE_ROWS == 0

    lb = jnp.cumsum(jax.nn.softmax(hg_lb_logits.astype(F32), axis=0), axis=0)
    lb = lb - lb[0]
    cs = jnp.concatenate([c, c_ctx[None, :], jnp.zeros((MOD_ROWS - batch - 1, d), F32)], axis=0)
    mod = _ada(cs, w_ada, b_ada).reshape(depth, MOD_ROWS, 6, d)
    tables = _rope_tables(seq)
    zero_state = jnp.zeros((batch, 2, HG_HEADS, HG_DK, LANES), F32)

    def reorder_heads(t, axis):
        rep = ATT_HEADS // ATT_KV_HEADS
        split = t.shape[:axis] + (ATT_KV_HEADS, rep, HEAD_DIM) + t.shape[axis + 1:]
        return jnp.swapaxes(t.reshape(split), axis, axis + 1).reshape(t.shape)

    d_gate = N_BRANCH * d
    w_proj = jnp.concatenate([reorder_heads(w_in[:, :, :d_br], 2), w_in[:, :, d_br:-d_gate]],
                             axis=2).astype(BF16)
    w_gates = w_in[:, :, -d_gate:].astype(BF16)
    w_br = jnp.concatenate([reorder_heads(w_branch[:, :1], 2), w_branch[:, 1:]], axis=1).astype(BF16)
    ws = mlp_ws.astype(BF16)
    bs = jnp.broadcast_to(mlp_bs[:, :, :, None], mlp_bs.shape + (LANES,))
    w_o, wg, wu, wd = (w.astype(BF16) for w in (w_out, w_ffn_gate, w_ffn_up, w_ffn_down))
    n1, n2, vg, vb, hgn = (t[:, None, :] for t in (norm1_g, norm2_g, mlp_v_norm_g, mlp_v_norm_b, hg_norm_g))
    fg = final_norm_g[None, :]

    xs = x.reshape(batch * seq, d)
    cx = ctx.reshape(batch * ctx_len, d)
    for l in range(depth):
        need_ctx = l < depth - 1
        mod_x = mod[l, :batch]
        mod_c = jnp.broadcast_to(mod[l, batch:batch + 1], (batch, 6, d))
        pc = _in_proj(cx, mod_c, n1, w_proj, lb, vg, vb, tables, layer=l, seq=ctx_len, tm=tm_c, rope=False,
                      only=None if need_ctx else STATE_ONLY_OUTPUTS)
        px = _in_proj(xs, mod_x, n1, w_proj, lb, vg, vb, tables, layer=l, seq=seq, tm=tm_x // 2, rope=True)
        a = _attention(px["qa"], px["ka"], px["va"], pc["ka"], pc["va"], attn_sink[l],
                       batch=batch, seq=seq, ctx_len=ctx_len, band=True)
        if need_ctx:
            ocf, ocb, s_ctx = _hgrn(pc, zero_state, batch=batch, seq=ctx_len)
        else:
            s_ctx = _hgrn(pc, zero_state, batch=batch, seq=ctx_len, state_only=True)
        of, ob, _ = _hgrn(px, s_ctx, batch=batch, seq=seq)
        xs = _merge(xs, mod_x, a, of, ob, px, n1, w_gates, hgn, ws, bs, w_br, w_o,
                    layer=l, seq=seq, tm=tm_x)
        xs = _ffn(xs, mod_x, n2, wg, wu, wd, fg, layer=l, seq=seq, tm=tm_x, final=not need_ctx)
        if need_ctx:
            ac = _attention(pc["qa"], None, None, pc["ka"], pc["va"], attn_sink[l],
                            batch=batch, seq=ctx_len, ctx_len=ctx_len, band=False)
            cx = _merge(cx, mod_c, ac, ocf, ocb, pc, n1, w_gates, hgn, ws, bs, w_br, w_o,
                        layer=l, seq=ctx_len, tm=tm_c)
            cx = _ffn(cx, mod_c, n2, wg, wu, wd, fg, layer=l, seq=ctx_len, tm=tm_c, final=False)
    return xs.reshape(batch, seq, d)
```

```python
import functools
import math

import jax
import jax.numpy as jnp
import numpy as np
from jax import lax
from jax.experimental import pallas as pl
from jax.experimental.pallas import tpu as pltpu

F32 = jnp.float32
BF16 = jnp.bfloat16

EPS = 1e-6
GRID_W = 64
HEAD_DIM = 64
ATT_HEADS = 8
ATT_KV_HEADS = 2
WINDOW = 128
ATT_BLOCK = 128
ROPE_BASE = 10000.0
HG_DK = 128
HG_HEADS = 4
HG_CHUNK = 128
MLP_CHUNK = 128
MLP_GROUPS = 4
N_BRANCH = 3

LANES = 128
MXU_COLS = 256
IN_PROJ_ROWS = 128
FFN_ROWS = 128
MERGE_ROWS = 256
ATT_SAMPLES = 4
ROW_TILE = 1024
MOD_ROWS = 24
VMEM_LIMIT = 56 * 1024 * 1024
NEG = -1e30
LOG2E = 1.4426950408889634

assert WINDOW == ATT_BLOCK


def _sigmoid(x):
    return 0.5 * jnp.tanh(0.5 * x) + 0.5


def _silu(x):
    return x * _sigmoid(x)


def _rms_mod(x, g, sc, sh):
    y = x * lax.rsqrt(jnp.mean(x * x, axis=-1, keepdims=True) + EPS)
    return (y * g) * (1.0 + sc) + sh


def _dot(a, b):
    return jnp.dot(a, b, preferred_element_type=F32)


def _dot_nt(a, b):
    return lax.dot_general(a, b, (((1,), (1,)), ((), ())), preferred_element_type=F32)


def _dot_tn(a, b):
    return lax.dot_general(a, b, (((0,), (0,)), ((), ())), preferred_element_type=F32)


def _resident(shape, layer=None):
    if layer is None:
        return pl.BlockSpec(shape, lambda *_: (0,) * len(shape), pipeline_mode=pl.Buffered(1))
    rest = tuple(shape[1:])
    return pl.BlockSpec((None,) + rest, lambda *_: (layer,) + (0,) * len(rest),
                        pipeline_mode=pl.Buffered(1))


def _params(*sem):
    return pltpu.CompilerParams(dimension_semantics=sem, vmem_limit_bytes=VMEM_LIMIT)


def _ada_kernel(c_ref, w_ref, b_ref, o_ref):
    s = _silu(c_ref[...])
    o_ref[...] = jnp.dot(s, w_ref[...], precision=lax.Precision.HIGHEST,
                         preferred_element_type=F32) + b_ref[...]


def _ada(cs, w_ada, b_ada):
    depth, d, d6 = w_ada.shape
    bn = d
    return pl.pallas_call(
        _ada_kernel,
        grid=(depth, d6 // bn),
        in_specs=[
            pl.BlockSpec((MOD_ROWS, d), lambda l, j: (0, 0)),
            pl.BlockSpec((None, d, bn), lambda l, j: (l, 0, j)),
            pl.BlockSpec((None, 1, bn), lambda l, j: (l, 0, j)),
        ],
        out_specs=pl.BlockSpec((None, MOD_ROWS, bn), lambda l, j: (l, 0, j)),
        out_shape=jax.ShapeDtypeStruct((depth, MOD_ROWS, d6), F32),
        compiler_params=_params("parallel", "parallel"),
        name="ada",
    )(cs, w_ada, b_ada.reshape(depth, 1, d6))


def _rope(x, cos, sa, sb):
    return x * cos + pltpu.roll(x, 16, 1) * sa + pltpu.roll(x, LANES - 16, 1) * sb


def _forget_terms(z, lb):
    z2 = z * LOG2E
    u = jnp.exp2(-jnp.abs(z2))
    one_u = 1.0 + u
    log_sig = jnp.minimum(z2, 0.0) - jnp.log2(one_u)
    la = jnp.maximum(jnp.log2(lb), NEG)
    b = jnp.log1p(-lb) * LOG2E + log_sig
    logf2 = jnp.maximum(la, b) + jnp.log2(1.0 + jnp.exp2(-jnp.abs(la - b)))
    return logf2


def _in_proj_kernel(x_ref, mod_ref, g_ref, w_ref, lb_ref, vg_ref, vb_ref, cos_ref, sa_ref, sb_ref,
                    *out_refs, rope, d_br, names):
    out = dict(zip(names, out_refs))
    cw = MXU_COLS
    off_q, off_kv, off_hq = 0, d_br, d_br + cw
    off_f, off_b, off_v, off_g, off_u, off_cv = (off_hq + n * d_br for n in range(1, 7))
    for rows in (pl.ds(r, IN_PROJ_ROWS) for r in range(0, x_ref.shape[0], IN_PROJ_ROWS)):
        h = _rms_mod(x_ref[rows, :], g_ref[...], mod_ref[0, 1:2, :], mod_ref[0, 0:1, :]).astype(BF16)

        def proj(lo):
            return _dot(h, w_ref[:, lo:lo + cw])

        def maybe_rope(t):
            if not rope:
                return t
            return _rope(t, cos_ref[rows, :], sa_ref[rows, :], sb_ref[rows, :])

        def each(name, lo, fn):
            if name in out:
                for c in range(0, out[name].shape[1], cw):
                    out[name][rows, c:c + cw] = fn(proj(lo + c)).astype(out[name].dtype)

        def rope_lanes(acc):
            parts = [maybe_rope(acc[:, s:s + LANES]) for s in range(0, cw, LANES)]
            return jnp.concatenate(parts, axis=1) * (HEAD_DIM ** -0.5 * LOG2E)

        each("qa", off_q, rope_lanes)
        acc = proj(off_kv)
        out["ka"][rows, :] = maybe_rope(acc[:, :LANES]).astype(BF16)
        out["va"][rows, :] = acc[:, LANES:].astype(BF16)
        each("hq", off_hq, lambda a: _silu(a) * HG_DK ** -0.5)
        plain = iter([(name, o + c, c) for name, o in (("hv", off_v), ("mu", off_u)) if name in out
                      for c in range(0, d_br, cw)])
        for which, (l_name, lo) in enumerate((("hlf", off_f), ("hlb", off_b))):
            for c in range(0, d_br, cw):
                out[l_name][rows, c:c + cw] = _forget_terms(proj(lo + c), lb_ref[which:which + 1, c:c + cw])
                filler = next(plain, None)
                if filler is not None:
                    p_name, p_lo, p_c = filler
                    out[p_name][rows, p_c:p_c + cw] = proj(p_lo).astype(BF16)
        each("hg", off_g, lambda a: _silu(a))
        if "mvn" in out:
            v = jnp.concatenate([proj(off_cv + c) for c in range(0, d_br, cw)], axis=1)
            mean = jnp.mean(v, axis=-1, keepdims=True)
            var = jnp.mean(jnp.square(v - mean), axis=-1, keepdims=True)
            out["mvn"][rows, :] = ((v - mean) * lax.rsqrt(var + EPS) * vg_ref[...]
                                   + vb_ref[...]).astype(BF16)


STATE_ONLY_OUTPUTS = ("ka", "va", "hlf", "hlb", "hv")


def _in_proj(x2, mod, norm_g, w_in, lb, v_g, v_b, tables, *, layer, seq, tm, rope, only=None):
    rows, d = x2.shape
    d_br = d // 2
    per = seq // tm
    row = lambda w: pl.BlockSpec((tm, w), lambda i: (i, 0))
    tab = pl.BlockSpec((tm, LANES), lambda i: (i % per, 0))
    widths = dict(qa=d_br, ka=LANES, va=LANES, hq=d_br, hlf=d_br, hlb=d_br,
                  hv=d_br, hg=d_br, mu=d_br, mvn=d_br)
    assert w_in.shape[2] == sum(widths.values())
    if only is not None:
        widths = {k: w for k, w in widths.items() if k in only}
    dtypes = {k: (F32 if k in ("hlf", "hlb") else BF16) for k in widths}
    outs = pl.pallas_call(
        functools.partial(_in_proj_kernel, rope=rope, d_br=d_br, names=tuple(widths)),
        grid=(rows // tm,),
        in_specs=[
            row(d),
            pl.BlockSpec((1, 6, d), lambda i: (i // per, 0, 0)),
            _resident(norm_g.shape, layer),
            _resident(w_in.shape, layer),
            _resident(lb.shape, layer),
            _resident(v_g.shape, layer),
            _resident(v_b.shape, layer),
            tab, tab, tab,
        ],
        out_specs=[row(w) for w in widths.values()],
        out_shape=[jax.ShapeDtypeStruct((rows, w), dtypes[k]) for k, w in widths.items()],
        compiler_params=_params("parallel"),
        name="in_proj_rope" if rope else "in_proj_ctx",
    )(x2, mod, norm_g, w_in, lb, v_g, v_b, *tables)
    return dict(zip(widths, outs))


def _attn_kernel(sink_ref, q_ref, *refs, band):
    if band:
        kl_ref, km_ref, kr_ref, vl_ref, vm_ref, vr_ref, kc_ref, vc_ref, o_ref = refs
        j = pl.program_id(1)
        nq = pl.num_programs(1)
        qi = lax.broadcasted_iota(jnp.int32, (ATT_BLOCK, ATT_BLOCK), 0)
        kj = lax.broadcasted_iota(jnp.int32, (ATT_BLOCK, ATT_BLOCK), 1)
        left = jnp.where(jnp.logical_and(kj >= qi, j > 0), 0.0, NEG)
        right = jnp.where(jnp.logical_and(kj <= qi, j < nq - 1), 0.0, NEG)
    else:
        kc_ref, vc_ref, o_ref = refs
    tq = q_ref.shape[1]
    groups = q_ref.shape[2] // LANES
    rep = ATT_HEADS // ATT_KV_HEADS
    left_half = lax.broadcasted_iota(jnp.int32, (tq, LANES), 1) < HEAD_DIM
    top = lax.broadcasted_iota(jnp.int32, (2 * tq, 1), 0) < tq
    work = []
    for s in range(q_ref.shape[0]):
        if band:
            keys = jnp.concatenate([kl_ref[s], km_ref[s], kr_ref[s], kc_ref[s]], axis=0)
            vals = jnp.concatenate([vl_ref[s], vm_ref[s], vr_ref[s], vc_ref[s]], axis=0)
        else:
            keys, vals = kc_ref[s], vc_ref[s]
        for g in range(groups):
            qc = q_ref[s, :, g * LANES:(g + 1) * LANES]
            zero = jnp.zeros_like(qc)
            q2 = jnp.concatenate([jnp.where(left_half, qc, zero), jnp.where(left_half, zero, qc)], axis=0)
            work.append((s, g, _dot_nt(q2, keys), vals))
    for s, g, sc, vals in work:
        if band:
            two = lambda t: jnp.concatenate([t, t], axis=0)
            sc = jnp.concatenate([sc[:, :ATT_BLOCK] + two(left), sc[:, ATT_BLOCK:2 * ATT_BLOCK],
                                  sc[:, 2 * ATT_BLOCK:3 * ATT_BLOCK] + two(right), sc[:, 3 * ATT_BLOCK:]],
                                 axis=1)
        sink = jnp.where(top, sink_ref[g], sink_ref[g + rep]) * LOG2E
        m = jnp.maximum(jnp.max(sc, axis=-1, keepdims=True), sink)
        e = jnp.exp2(sc - m)
        den = jnp.sum(e, axis=-1, keepdims=True) + jnp.exp2(sink - m)
        o = _dot(e.astype(BF16), vals) * (1.0 / den)
        o_ref[s, :, g * LANES:(g + 1) * LANES] = jnp.where(left_half, o[:tq], o[tq:]).astype(BF16)


def _attention(q, k, v, kc, vc, sink, *, batch, seq, ctx_len, band):
    d_q = q.shape[1]
    tq = ATT_BLOCK
    nq = seq // tq
    ns = math.gcd(batch, ATT_SAMPLES)
    smem = pl.BlockSpec(memory_space=pltpu.SMEM)
    qspec = pl.BlockSpec((ns, tq, d_q), lambda b, j: (b, j, 0))
    cspec = pl.BlockSpec((ns, ctx_len, LANES), lambda b, j: (b, 0, 0))
    per_sample = lambda t, n: t.reshape(batch, n, t.shape[1])
    kc, vc = per_sample(kc, ctx_len), per_sample(vc, ctx_len)
    if band:
        kvspecs = [
            pl.BlockSpec((ns, tq, LANES), lambda b, j: (b, jnp.maximum(j - 1, 0), 0)),
            pl.BlockSpec((ns, tq, LANES), lambda b, j: (b, j, 0)),
            pl.BlockSpec((ns, tq, LANES), lambda b, j: (b, jnp.minimum(j + 1, nq - 1), 0)),
        ]
        in_specs = [smem, qspec] + kvspecs + kvspecs + [cspec, cspec]
        k, v = per_sample(k, seq), per_sample(v, seq)
        args = (sink, per_sample(q, seq), k, k, k, v, v, v, kc, vc)
    else:
        in_specs = [smem, qspec, cspec, cspec]
        args = (sink, per_sample(q, seq), kc, vc)
    return pl.pallas_call(
        functools.partial(_attn_kernel, band=band),
        grid=(batch // ns, nq),
        in_specs=in_specs,
        out_specs=qspec,
        out_shape=jax.ShapeDtypeStruct((batch, seq, d_q), BF16),
        compiler_params=_params("parallel", "parallel"),
        name="attn_band" if band else "attn_ctx",
    )(*args).reshape(batch * seq, d_q)


HG_SAMPLES = 4
HG_LEVELS = tuple(HG_CHUNK >> i for i in range(1, HG_CHUNK.bit_length()))


def _hgrn_tables():
    c = HG_CHUNK
    t = np.arange(c)[:, None]
    s = np.arange(c)[None, :]
    mats, lvls = [], []
    for rev in (False, True):
        lvl = np.full((c, c), -1, np.int32)
        lvl[np.arange(c), np.arange(c)] = 0
        for idx, b in enumerate(HG_LEVELS):
            same = (t // (2 * b)) == (s // (2 * b))
            t_hi, s_hi = (t % (2 * b)) >= b, (s % (2 * b)) >= b
            lvl[same & ((~t_hi & s_hi) if rev else (t_hi & ~s_hi))] = idx + 1
        mat = ((s >= t) if rev else (s <= t)).astype(np.float32)
        mats.append(np.concatenate([mat, mat], axis=1))
        n_bits = len(HG_LEVELS).bit_length()
        planes = [(np.maximum(lvl, 0) >> bit) & 1 for bit in range(n_bits)] + [lvl >= 0]
        lvls.append(np.stack(planes).astype(np.int32))
    return np.stack(mats), np.stack(lvls)


def _boundary_exponent(g, g_ref, lg, lanes, b, rev):
    c = g.shape[0]
    row = lax.broadcasted_iota(jnp.int32, (c, 1), 0)
    bound = lambda g0: g0 + b - (0 if rev else 1)
    if b == 1:
        return jnp.where(((row & 1) != 0) != rev, lg, 0.0)
    if b >= 8:
        pieces = []
        for g0 in range(0, c, 2 * b):
            gr = g_ref[bound(g0):bound(g0) + 1, lanes]
            first, second = g[g0:g0 + b], g[g0 + b:g0 + 2 * b]
            pieces += [first - gr, gr - second] if rev else [gr - first, second - gr]
        return jnp.concatenate(pieces, axis=0)
    sub = 8
    pieces = []
    for r0 in range(0, c, sub):
        refs = [jnp.broadcast_to(g_ref[bound(g0):bound(g0) + 1, lanes], (sub, LANES))
                for g0 in range(r0, r0 + sub, 2 * b)]
        gr = refs[0]
        for n in range(1, len(refs)):
            gr = jnp.where((row[:sub] & (sub - 1)) >= n * 2 * b, refs[n], gr)
        pieces.append(gr)
    return -jnp.abs(g - jnp.concatenate(pieces, axis=0))


def _mixed_rows(q, k, b, rev):
    c = q.shape[0]
    if b >= 8:
        pieces = []
        for g0 in range(0, c, 2 * b):
            lo, hi = slice(g0, g0 + b), slice(g0 + b, g0 + 2 * b)
            pieces += [q[lo], k[hi]] if rev else [k[lo], q[hi]]
        return jnp.concatenate(pieces, axis=0)
    later = (lax.broadcasted_iota(jnp.int32, (c, 1), 0) & b) != 0
    return jnp.where(later, k, q) if rev else jnp.where(later, q, k)


def _hgrn_scores(q, k, lg, g_ref, lanes, lvl, rev):
    c = HG_CHUNK
    g = g_ref[:, lanes]
    qf, kf = q.astype(F32), k.astype(F32)
    blocks = [(q, k.T)]
    for b in HG_LEVELS:
        e = jnp.exp2(_boundary_exponent(g, g_ref, lg, lanes, b, rev))
        y = (_mixed_rows(qf, kf, b, rev) * e).astype(BF16)
        blocks.append((y, y.T))
    zero = jnp.zeros((LANES, c), BF16)
    bits = [lvl[n] != 0 for n in range(lvl.shape[0] - 1)]
    picked = []
    for i in range(0, len(blocks), 2):
        (la, ra), (lb, rb) = blocks[i], blocks[i + 1]
        rhs = jnp.concatenate([jnp.concatenate([ra, zero], axis=1),
                               jnp.concatenate([zero, rb], axis=1)], axis=0)
        s = _dot(jnp.concatenate([la, lb], axis=1), rhs)
        picked.append(jnp.where(bits[0], s[:, c:], s[:, :c]))
    for bit in bits[1:]:
        picked = [jnp.where(bit, hi, lo) for lo, hi in zip(picked[0::2], picked[1::2])]
    return jnp.where(lvl[-1] != 0, picked[0], 0.0).astype(BF16)


def _hgrn_apply(q, k, v, att, g_ref, lanes, state, rev):
    lhs = jnp.concatenate([(q.astype(F32) * jnp.exp2(g_ref[:, lanes])).astype(BF16), att], axis=1)
    rhs = jnp.concatenate([state.astype(BF16), v], axis=0)
    return _dot(lhs, rhs), _hgrn_next_state(k, v, g_ref, lanes, state, rev)


def _hgrn_next_state(k, v, g_ref, lanes, state, rev):
    g = g_ref[:, lanes]
    end = 0 if rev else HG_CHUNK - 1
    g_end = g_ref[end:end + 1, lanes]
    dec_t = jnp.broadcast_to(jnp.exp2(g_end), (HG_DK, LANES)).T
    return dec_t * state + _dot_tn((k.astype(F32) * jnp.exp2(g_end - g)).astype(BF16), v)


def _hgrn_cumsum(lg, m):
    hi = lg.astype(BF16)
    lo = (lg - hi.astype(F32)).astype(BF16)
    return _dot(m, jnp.concatenate([hi, lo], axis=0))


def _key_weight(lg):
    return (1.0 - jnp.exp2(lg)).astype(BF16)


def _hgrn_state_kernel(lf_ref, vf_ref, lb_ref, vb_ref, s0_ref, m_ref, sfin_ref, state_ref, g_ref):
    i = pl.program_id(1)

    @pl.when(i == 0)
    def _():
        state_ref[...] = s0_ref[...]

    for s in range(state_ref.shape[0]):
        for d, (l_ref, v_ref) in enumerate(((lf_ref, vf_ref), (lb_ref, vb_ref))):
            lg = l_ref[s]
            g_ref[s, d] = _hgrn_cumsum(lg, m_ref[d])
            for h in range(HG_HEADS):
                lanes = slice(h * LANES, (h + 1) * LANES)
                state_ref[s, d, h] = _hgrn_next_state(_key_weight(lg[:, lanes]), v_ref[s, :, lanes],
                                                      g_ref.at[s, d], lanes, state_ref[s, d, h], bool(d))

    @pl.when(i == pl.num_programs(1) - 1)
    def _():
        sfin_ref[...] = state_ref[...]


def _hgrn_kernel(qf_ref, lf_ref, vf_ref, qb_ref, lb_ref, vb_ref, s0_ref, m_ref, lvl_ref,
                 of_ref, ob_ref, sfin_ref, state_ref, g_ref):
    i = pl.program_id(1)

    @pl.when(i == 0)
    def _():
        state_ref[...] = s0_ref[...]

    dirs = ((qf_ref, lf_ref, vf_ref, of_ref), (qb_ref, lb_ref, vb_ref, ob_ref))
    work = []
    for s in range(state_ref.shape[0]):
        for d, (q_ref, l_ref, v_ref, o_ref) in enumerate(dirs):
            lg = l_ref[s]
            g_ref[s, d] = _hgrn_cumsum(lg, m_ref[d])
            for h in range(HG_HEADS):
                lanes = slice(h * LANES, (h + 1) * LANES)
                work.append((s, d, h, lanes, o_ref, q_ref[s, :, lanes], _key_weight(lg[:, lanes]),
                             v_ref[s, :, lanes], lg[:, lanes], state_ref[s, d, h]))
    atts = [_hgrn_scores(q, k, lg, g_ref.at[s, d], lanes, lvl_ref[d], bool(d))
            for s, d, h, lanes, o_ref, q, k, v, lg, state in work]
    done = []
    for att, (s, d, h, lanes, o_ref, q, k, v, lg, state) in zip(atts, work):
        out, new_state = _hgrn_apply(q, k, v, att, g_ref.at[s, d], lanes, state, bool(d))
        done.append((s, d, h, lanes, o_ref, out, new_state))
    for s, d, h, lanes, o_ref, out, new_state in done:
        state_ref[s, d, h] = new_state
        o_ref[s, :, lanes] = out.astype(o_ref.dtype)

    @pl.when(i == pl.num_programs(1) - 1)
    def _():
        sfin_ref[...] = state_ref[...]


def _hgrn(p, s0, *, batch, seq, state_only=False):
    c = HG_CHUNK
    n = seq // c
    ns = math.gcd(batch, HG_SAMPLES)
    width = p["hv"].shape[1]
    mats, lvls = _hgrn_tables()
    fwd = pl.BlockSpec((ns, c, width), lambda b, i: (b, i, 0))
    bwd = pl.BlockSpec((ns, c, width), lambda b, i: (b, n - 1 - i, 0))
    st = pl.BlockSpec((ns, 2, HG_HEADS, HG_DK, LANES), lambda b, i: (b, 0, 0, 0, 0))
    scratch = [pltpu.VMEM((ns, 2, HG_HEADS, HG_DK, LANES), F32),
               pltpu.VMEM((ns, 2, mats.shape[1], width), F32)]
    arg = lambda name: p[name].reshape(batch, seq, width)
    if state_only:
        return pl.pallas_call(
            _hgrn_state_kernel,
            grid=(batch // ns, n),
            in_specs=[fwd, fwd, bwd, bwd, st, _resident(mats.shape)],
            out_specs=st,
            out_shape=jax.ShapeDtypeStruct(s0.shape, F32),
            scratch_shapes=scratch,
            compiler_params=_params("parallel", "arbitrary"),
            name="hgrn_state",
        )(arg("hlf"), arg("hv"), arg("hlb"), arg("hv"), s0, jnp.asarray(mats, BF16))
    of, ob, s_fin = pl.pallas_call(
        _hgrn_kernel,
        grid=(batch // ns, n),
        in_specs=[fwd, fwd, fwd, bwd, bwd, bwd, st, _resident(mats.shape), _resident(lvls.shape)],
        out_specs=[fwd, bwd, st],
        out_shape=[jax.ShapeDtypeStruct((batch, seq, width), BF16),
                   jax.ShapeDtypeStruct((batch, seq, width), BF16),
                   jax.ShapeDtypeStruct(s0.shape, F32)],
        scratch_shapes=scratch,
        compiler_params=_params("parallel", "arbitrary"),
        name="hgrn_scan",
    )(arg("hq"), arg("hlf"), arg("hv"), arg("hq"), arg("hlb"), arg("hv"), s0,
      jnp.asarray(mats, BF16), jnp.asarray(lvls))
    return of.reshape(batch * seq, width), ob.reshape(batch * seq, width), s_fin


def _merge_kernel(x_ref, mod_ref, a_ref, of_ref, ob_ref, hg_ref, mu_ref, mvn_ref,
                  g_ref, wgt_ref, hgn_ref, ws_ref, bs_ref, wbr_ref, wo_ref, o_ref):
    tm, d = x_ref.shape
    cw = MXU_COLS
    for r0 in range(0, tm, MERGE_ROWS):
        rows = pl.ds(r0, MERGE_ROWS)
        h = _rms_mod(x_ref[rows, :], g_ref[...], mod_ref[0, 1:2, :], mod_ref[0, 0:1, :]).astype(BF16)
        o = of_ref[rows, :].astype(F32) + ob_ref[rows, :].astype(F32)
        parts = []
        for head in range(HG_HEADS):
            oh = o[:, head * LANES:(head + 1) * LANES]
            parts.append(oh * lax.rsqrt(jnp.mean(oh * oh, axis=-1, keepdims=True) + EPS))
        b_br = (jnp.concatenate(parts, axis=1) * hgn_ref[...] * hg_ref[rows, :].astype(F32)).astype(BF16)
        gated = []
        for r in range(r0, r0 + MERGE_ROWS, MLP_CHUNK):
            cols = []
            for grp in range(MLP_GROUPS):
                cs = slice(grp * LANES, (grp + 1) * LANES)
                cols.append(_dot(ws_ref[grp], mvn_ref[pl.ds(r, MLP_CHUNK), cs]) + bs_ref[grp])
            gated.append(jnp.concatenate(cols, axis=1))
        c_br = (mu_ref[rows, :].astype(F32) * jnp.concatenate(gated, axis=0)).astype(BF16)
        branches = (a_ref[rows, :], b_br, c_br)
        merged = []
        for c in range(0, d, cw):
            m = jnp.zeros((MERGE_ROWS, cw), F32)
            for n, br in enumerate(branches):
                gate = _sigmoid(_dot(h, wgt_ref[:, n * d + c:n * d + c + cw]))
                m = m + gate * _dot(br, wbr_ref[n, :, c:c + cw])
            merged.append(m.astype(BF16))
        merged = jnp.concatenate(merged, axis=1)
        for c in range(0, d, cw):
            y = _dot(merged, wo_ref[:, c:c + cw])
            o_ref[rows, c:c + cw] = x_ref[rows, c:c + cw] + mod_ref[0, 2:3, c:c + cw] * y


def _merge(x2, mod, a, of, ob, p, norm_g, w_gates, hg_g, ws, bs, w_br, w_o, *, layer, seq, tm):
    rows, d = x2.shape
    d_br = d // 2
    per = seq // tm
    row = lambda w: pl.BlockSpec((tm, w), lambda i: (i, 0))
    weights = (norm_g, w_gates, hg_g, ws, bs, w_br, w_o)
    return pl.pallas_call(
        _merge_kernel,
        grid=(rows // tm,),
        in_specs=[
            row(d),
            pl.BlockSpec((1, 6, d), lambda i: (i // per, 0, 0)),
            row(d_br), row(d_br), row(d_br), row(d_br), row(d_br), row(d_br),
        ] + [_resident(w.shape, layer) for w in weights],
        out_specs=row(d),
        out_shape=jax.ShapeDtypeStruct((rows, d), F32),
        compiler_params=_params("parallel"),
        name="merge",
    )(x2, mod, a, of, ob, p["hg"], p["mu"], p["mvn"], *weights)


def _ffn_kernel(x_ref, mod_ref, g_ref, wg_ref, wu_ref, wd_ref, fg_ref, o_ref, act_ref, *, final, bf):
    for rows in (pl.ds(r, FFN_ROWS) for r in range(0, x_ref.shape[0], FFN_ROWS)):
        x = x_ref[rows, :]
        h = _rms_mod(x, g_ref[...], mod_ref[0, 4:5, :], mod_ref[0, 3:4, :]).astype(BF16)
        for c in range(wg_ref.shape[1] // bf):
            sl = slice(c * bf, (c + 1) * bf)
            act_ref[rows, sl] = (_silu(_dot(h, wg_ref[:, sl])) * _dot(h, wu_ref[:, sl])).astype(BF16)
        y = x + mod_ref[0, 5:6, :] * _dot(act_ref[rows, :], wd_ref[...])
        if final:
            y = y * lax.rsqrt(jnp.mean(y * y, axis=-1, keepdims=True) + EPS) * fg_ref[...]
        o_ref[rows, :] = y


def _ffn(x2, mod, norm_g, w_gate, w_up, w_down, final_g, *, layer, seq, tm, final):
    rows, d = x2.shape
    d_ff = w_gate.shape[2]
    per = seq // tm
    row = pl.BlockSpec((tm, d), lambda i: (i, 0))
    return pl.pallas_call(
        functools.partial(_ffn_kernel, final=final, bf=MXU_COLS),
        grid=(rows // tm,),
        in_specs=[
            row,
            pl.BlockSpec((1, 6, d), lambda i: (i // per, 0, 0)),
            _resident(norm_g.shape, layer),
            _resident(w_gate.shape, layer),
            _resident(w_up.shape, layer),
            _resident(w_down.shape, layer),
            _resident((1, d)),
        ],
        out_specs=row,
        out_shape=jax.ShapeDtypeStruct((rows, d), F32),
        scratch_shapes=[pltpu.VMEM((tm, d_ff), BF16)],
        compiler_params=_params("parallel"),
        name="ffn_final" if final else "ffn",
    )(x2, mod, norm_g, w_gate, w_up, w_down, final_g)


def _rope_tables(seq):
    pos = jnp.arange(seq)
    nf = HEAD_DIM // 4
    inv = ROPE_BASE ** (-jnp.arange(nf, dtype=F32) / nf)
    ang_r = (pos // GRID_W).astype(F32)[:, None] * inv[None, :]
    ang_c = (pos % GRID_W).astype(F32)[:, None] * inv[None, :]
    zero = jnp.zeros_like(ang_r)
    cos = jnp.concatenate([jnp.cos(ang_r)] * 2 + [jnp.cos(ang_c)] * 2, axis=1)
    sa = jnp.concatenate([zero, jnp.sin(ang_r), zero, jnp.sin(ang_c)], axis=1)
    sb = jnp.concatenate([-jnp.sin(ang_r), zero, -jnp.sin(ang_c), zero], axis=1)
    reps = LANES // HEAD_DIM
    return tuple(jnp.tile(t, (1, reps)) for t in (cos, sa, sb))


def kernel(x, c, ctx, c_ctx, w_ada, b_ada, norm1_g, norm2_g, w_in, attn_sink, hg_lb_logits, hg_norm_g, mlp_v_norm_g, mlp_v_norm_b, mlp_ws, mlp_bs, w_branch, w_out, w_ffn_gate, w_ffn_up, w_ffn_down, final_norm_g):
    batch, seq, d = x.shape
    ctx_len = ctx.shape[1]
    depth = w_ada.shape[0]
    d_br = d // 2
    tm_x, tm_c = min(ROW_TILE, seq), min(ROW_TILE // 4, ctx_len)
    assert batch + 1 <= MOD_ROWS and seq % tm_x == 0 and ctx_len % tm_c == 0 and tm_c % MERGE_ROWS == 0

    lb = jnp.cumsum(jax.nn.softmax(hg_lb_logits.astype(F32), axis=0), axis=0)
    lb = lb - lb[0]
    cs = jnp.concatenate([c, c_ctx[None, :], jnp.zeros((MOD_ROWS - batch - 1, d), F32)], axis=0)
    mod = _ada(cs, w_ada, b_ada).reshape(depth, MOD_ROWS, 6, d)
    tables = _rope_tables(seq)
    zero_state = jnp.zeros((batch, 2, HG_HEADS, HG_DK, LANES), F32)

    def reorder_heads(t, axis):
        rep = ATT_HEADS // ATT_KV_HEADS
        split = t.shape[:axis] + (ATT_KV_HEADS, rep, HEAD_DIM) + t.shape[axis + 1:]
        return jnp.swapaxes(t.reshape(split), axis, axis + 1).reshape(t.shape)

    d_gate = N_BRANCH * d
    w_proj = jnp.concatenate([reorder_heads(w_in[:, :, :d_br], 2), w_in[:, :, d_br:-d_gate]],
                             axis=2).astype(BF16)
    w_gates = w_in[:, :, -d_gate:].astype(BF16)
    w_br = jnp.concatenate([reorder_heads(w_branch[:, :1], 2), w_branch[:, 1:]], axis=1).astype(BF16)
    ws = mlp_ws.astype(BF16)
    bs = jnp.broadcast_to(mlp_bs[:, :, :, None], mlp_bs.shape + (LANES,))
    w_o, wg, wu, wd = (w.astype(BF16) for w in (w_out, w_ffn_gate, w_ffn_up, w_ffn_down))
    n1, n2, vg, vb, hgn = (t[:, None, :] for t in (norm1_g, norm2_g, mlp_v_norm_g, mlp_v_norm_b, hg_norm_g))
    fg = final_norm_g[None, :]

    xs = x.reshape(batch * seq, d)
    cx = ctx.reshape(batch * ctx_len, d)
    for l in range(depth):
        need_ctx = l < depth - 1
        mod_x = mod[l, :batch]
        mod_c = jnp.broadcast_to(mod[l, batch:batch + 1], (batch, 6, d))
        pc = _in_proj(cx, mod_c, n1, w_proj, lb, vg, vb, tables, layer=l, seq=ctx_len, tm=tm_c, rope=False,
                      only=None if need_ctx else STATE_ONLY_OUTPUTS)
        px = _in_proj(xs, mod_x, n1, w_proj, lb, vg, vb, tables, layer=l, seq=seq, tm=tm_x // 2, rope=True)
        a = _attention(px["qa"], px["ka"], px["va"], pc["ka"], pc["va"], attn_sink[l],
                       batch=batch, seq=seq, ctx_len=ctx_len, band=True)
        if need_ctx:
            ocf, ocb, s_ctx = _hgrn(pc, zero_state, batch=batch, seq=ctx_len)
        else:
            s_ctx = _hgrn(pc, zero_state, batch=batch, seq=ctx_len, state_only=True)
        of, ob, _ = _hgrn(px, s_ctx, batch=batch, seq=seq)
        xs = _merge(xs, mod_x, a, of, ob, px, n1, w_gates, hgn, ws, bs, w_br, w_o,
                    layer=l, seq=seq, tm=tm_x)
        xs = _ffn(xs, mod_x, n2, wg, wu, wd, fg, layer=l, seq=seq, tm=tm_x, final=not need_ctx)
        if need_ctx:
            ac = _attention(pc["qa"], None, None, pc["ka"], pc["va"], attn_sink[l],
                            batch=batch, seq=ctx_len, ctx_len=ctx_len, band=False)
            cx = _merge(cx, mod_c, ac, ocf, ocb, pc, n1, w_gates, hgn, ws, bs, w_br, w_o,
                        layer=l, seq=ctx_len, tm=tm_c)
            cx = _ffn(cx, mod_c, n2, wg, wu, wd, fg, layer=l, seq=ctx_len, tm=tm_c, final=False)
    return xs.reshape(batch, seq, d)
```
